```python
import math
import jax, jax.numpy as jnp
from jax import lax
import numpy as np

D_MODEL = 1024
BATCH = 32
SEQ = 2048
DEPTH = 2
DEC_BATCH = 8
DEC_SEQ = 16
PAST_LEN = 2048

CHUNK = 64
Q_BLOCK = 128
EPS = 1e-6
NEG = -1e30
F_MIN = 1e-12
MLA_HEADS = 8
MLA_NOPE = 64
MLA_ROPE = 32
MLA_V = 64
MLA_Q_LORA = 384
MLA_KV_LORA = 256
ROPE_THETA = 10000.0
MLA_SCALE = (MLA_NOPE + MLA_ROPE) ** -0.5
HG_HEADS = 8
HG_DK = 64
HG_DV = 64
HG_BLOCK = 16
DF_HEADS = 8
DF_DH = 32
DF_DV = 2 * DF_DH
DF_SCALE = DF_DH ** -0.5
N_BRANCH = 3
BR_WIDTH = 512
D_FF = 2816
CONV_W = 3
IN_SIZES = (MLA_Q_LORA, MLA_KV_LORA, MLA_ROPE,
            HG_HEADS * HG_DK, HG_HEADS * HG_DK, HG_HEADS * HG_DV, HG_HEADS * HG_DV,
            DF_HEADS * 2 * DF_DH, DF_HEADS * 2 * DF_DH, DF_HEADS * DF_DV,
            N_BRANCH * D_MODEL)
IN_COLS = sum(IN_SIZES)

kernel_name = 'hybrid_stream_mla_hgrn2_diffattn'


def _rms(x, g):
    xf = x.astype(jnp.float32)
    r = lax.rsqrt(jnp.mean(xf * xf, axis=-1, keepdims=True) + EPS)
    return (xf * r).astype(x.dtype) * g


def _split_cols(z):
    out, start = [], 0
    for n in IN_SIZES:
        out.append(z[..., start:start + n])
        start += n
    return out


def _rope(x, pos):
    half = MLA_ROPE // 2
    inv = ROPE_THETA ** (-jnp.arange(half, dtype=jnp.float32) / half)
    ang = pos.astype(jnp.float32)[:, None] * inv[None, :]
    ang = ang.reshape((1, pos.shape[0]) + (1,) * (x.ndim - 3) + (half,))
    cos, sin = jnp.cos(ang), jnp.sin(ang)
    x1 = x[..., :half].astype(jnp.float32)
    x2 = x[..., half:].astype(jnp.float32)
    return jnp.concatenate([x1 * cos - x2 * sin, x2 * cos + x1 * sin], axis=-1).astype(x.dtype)


def _chunk_mask(q_pos, k_pos):
    return (k_pos[None, :] // CHUNK) <= (q_pos[:, None] // CHUNK)


def _over_query_blocks(fn, q_pos, qs):
    T = q_pos.shape[0]
    if T > Q_BLOCK and T % Q_BLOCK == 0:
        nb = T // Q_BLOCK
        pos_b = q_pos.reshape(nb, Q_BLOCK)
        qs_b = tuple(jnp.moveaxis(q.reshape((q.shape[0], nb, Q_BLOCK) + q.shape[2:]), 1, 0) for q in qs)
        out = lax.map(lambda a: fn(a[0], *a[1]), (pos_b, qs_b))
        out = jnp.moveaxis(out, 0, 1)
        return out.reshape((out.shape[0], T) + out.shape[3:])
    return fn(q_pos, *qs)


def _hgrn2(q, k, v, logf, S0):
    B, T, H, DK = q.shape
    DV = v.shape[-1]
    n = -(-T // HG_BLOCK)
    pad = n * HG_BLOCK - T

    def prep(a):
        a = jnp.pad(a.astype(jnp.float32), ((0, 0), (0, pad), (0, 0), (0, 0)))
        return jnp.moveaxis(a.reshape(B, n, HG_BLOCK, H, a.shape[-1]), 1, 0)

    tril = jnp.tril(jnp.ones((HG_BLOCK, HG_BLOCK), dtype=bool))[None, :, :, None, None]

    def step(S, blk):
        qb, kb, vb, gb = blk
        b = jnp.cumsum(gb, axis=1)
        o_inter = jnp.einsum('blhk,bhkv->blhv', qb * jnp.exp(b), S)
        d = b[:, :, None] - b[:, None, :]
        dec = jnp.where(tril, jnp.exp(jnp.where(tril, d, 0.0)), 0.0)
        att = jnp.einsum('bthk,btshk,bshk->bhts', qb, dec, kb)
        o_intra = jnp.einsum('bhts,bshv->bthv', att, vb)
        b_last = b[:, -1]
        S = jnp.exp(b_last)[..., None] * S + jnp.einsum(
            'bshk,bshv->bhkv', kb * jnp.exp(b_last[:, None] - b), vb)
        return S, o_inter + o_intra

    S, o = lax.scan(step, S0.astype(jnp.float32), (prep(q), prep(k), prep(v), prep(logf)))
    o = jnp.moveaxis(o, 0, 1).reshape(B, n * HG_BLOCK, H, DV)[:, :T]
    return o.astype(v.dtype), S


def _layer(l, x, pos, past, S0, conv0, lb, prm):
    (norm_mix_g, w_in, mla_q_norm_g, mla_w_uq, mla_kv_norm_g, mla_w_ukv, hgrn_norm_g,
     diff_lambda, diff_norm_g, w_branch, w_out, norm_ffn_g, ffn_w_up, ffn_conv_w,
     ffn_conv_b, ffn_w_down) = prm
    B, T, _ = x.shape
    h = _rms(x, norm_mix_g)
    z = h @ w_in
    q_lat, kv_lat, k_rot, hq, hf, hi, hg, dq, dk, dv, gate = _split_cols(z)

    q = (_rms(q_lat, mla_q_norm_g) @ mla_w_uq).reshape(B, T, MLA_HEADS, MLA_NOPE + MLA_ROPE)
    q_nope = q[..., :MLA_NOPE]
    q_rot = _rope(q[..., MLA_NOPE:], pos)
    ckv_new = _rms(kv_lat, mla_kv_norm_g)
    krot_new = _rope(k_rot, pos)
    if past is None:
        key_pos = pos
        ckv_all, krot_all = ckv_new, krot_new
        dk_all_flat = dk.reshape(B, T, DF_HEADS, 2 * DF_DH)
        dv_all = dv.reshape(B, T, DF_HEADS, DF_DV)
    else:
        key_pos = jnp.arange(past[0].shape[1] + T)
        ckv_all = jnp.concatenate([past[0], ckv_new], axis=1)
        krot_all = jnp.concatenate([past[1], krot_new], axis=1)
        dk_all_flat = jnp.concatenate([past[2], dk.reshape(B, T, DF_HEADS, 2 * DF_DH)], axis=1)
        dv_all = jnp.concatenate([past[3], dv.reshape(B, T, DF_HEADS, DF_DV)], axis=1)
    K = key_pos.shape[0]
    kv = (ckv_all @ mla_w_ukv).reshape(B, K, MLA_HEADS, MLA_NOPE + MLA_V)
    k_nope, v_mla = kv[..., :MLA_NOPE], kv[..., MLA_NOPE:]

    def mla_fn(qp, qn, qr):
        s = (jnp.einsum('bqhd,bkhd->bhqk', qn, k_nope)
             + jnp.einsum('bqhr,bkr->bhqk', qr, krot_all)).astype(jnp.float32) * MLA_SCALE
        s = jnp.where(_chunk_mask(qp, key_pos), s, NEG)
        p = jax.nn.softmax(s, axis=-1).astype(v_mla.dtype)
        return jnp.einsum('bhqk,bkhd->bqhd', p, v_mla)

    o_a = _over_query_blocks(mla_fn, pos, (q_nope, q_rot)).reshape(B, T, BR_WIDTH)

    fpre = hf.astype(jnp.float32).reshape(B, T, HG_HEADS, HG_DK)
    lbh = lb.reshape(HG_HEADS, HG_DK)
    sig_neg = jax.nn.sigmoid(-fpre)
    f_h = jax.nn.sigmoid(fpre) + lbh * sig_neg
    logf = jnp.log(jnp.maximum(f_h, F_MIN))
    k_h = (1.0 - lbh) * sig_neg
    q_h = jax.nn.silu(hq).reshape(B, T, HG_HEADS, HG_DK)
    v_h = hi.reshape(B, T, HG_HEADS, HG_DV)
    o_h, S_new = _hgrn2(q_h, k_h, v_h, logf, S0)
    o_b = (_rms(o_h, hgrn_norm_g) * jax.nn.silu(hg).reshape(B, T, HG_HEADS, HG_DV)).reshape(B, T, BR_WIDTH)

    lam_init = 0.8 - 0.6 * math.exp(-0.3 * l)
    lamf = diff_lambda.astype(jnp.float32)
    lam = jnp.exp(jnp.sum(lamf[0] * lamf[1])) - jnp.exp(jnp.sum(lamf[2] * lamf[3])) + lam_init
    dq_h = dq.reshape(B, T, DF_HEADS, 2, DF_DH)
    dk_all = dk_all_flat.reshape(B, K, DF_HEADS, 2, DF_DH)

    def diff_fn(qp, qd):
        s = jnp.einsum('bqhjd,bkhjd->bjhqk', qd, dk_all).astype(jnp.float32) * DF_SCALE
        s = jnp.where(_chunk_mask(qp, key_pos), s, NEG)
        p = jax.nn.softmax(s, axis=-1)
        a = (p[:, 0] - lam * p[:, 1]).astype(dv_all.dtype)
        return jnp.einsum('bhqk,bkhd->bqhd', a, dv_all)

    o_c = _over_query_blocks(diff_fn, pos, (dq_h,))
    o_c = (_rms(o_c, diff_norm_g) * (1.0 - lam_init)).reshape(B, T, BR_WIDTH)

    g = jax.nn.sigmoid(gate.astype(jnp.float32)).astype(x.dtype).reshape(B, T, N_BRANCH, D_MODEL)
    br = jnp.stack([o_a, o_b, o_c], axis=2)
    proj = jnp.einsum('btnw,nwd->btnd', br, w_branch)
    x = x + jnp.sum(g * proj, axis=2) @ w_out

    u = _rms(x, norm_ffn_g) @ ffn_w_up
    a_up, v_up = u[..., :D_FF], u[..., D_FF:]
    ext = jnp.concatenate([conv0.astype(a_up.dtype), a_up], axis=1)
    c = ffn_conv_b + sum(ext[:, j:j + T] * ffn_conv_w[j] for j in range(CONV_W))
    x = x + (jax.nn.silu(c) * v_up) @ ffn_w_down
    conv_new = ext[:, T:]

    return x, (ckv_new, krot_new, dk.reshape(B, T, DF_HEADS, 2 * DF_DH),
               dv.reshape(B, T, DF_HEADS, DF_DV), S_new, conv_new)


def _nrm(k, shape, scale):
    return jax.random.normal(k, shape, jnp.float32) * scale


def setup_inputs(seed: int = 0) -> dict:
    key = jax.random.key(seed)
    ks = jax.random.split(key, 26)
    return {
        'x_prompt': _nrm(ks[0], (BATCH, SEQ, D_MODEL), 1.0),
        'x_sample': _nrm(ks[1], (DEC_BATCH, DEC_SEQ, D_MODEL), 1.0),
        'cache_mla_ckv': _nrm(ks[2], (DEPTH, DEC_BATCH, PAST_LEN, MLA_KV_LORA), 1.0),
        'cache_mla_krope': _nrm(ks[3], (DEPTH, DEC_BATCH, PAST_LEN, MLA_ROPE), 1.0),
        'cache_diff_k': _nrm(ks[4], (DEPTH, DEC_BATCH, PAST_LEN, DF_HEADS, 2 * DF_DH), 1.0),
        'cache_diff_v': _nrm(ks[5], (DEPTH, DEC_BATCH, PAST_LEN, DF_HEADS, DF_DV), 1.0),
        'state_hgrn': _nrm(ks[6], (DEPTH, DEC_BATCH, HG_HEADS, HG_DK, HG_DV), 0.5),
        'state_ffn_conv': _nrm(ks[7], (DEPTH, DEC_BATCH, CONV_W - 1, D_FF), 1.0),
        'norm_mix_g': 1.0 + _nrm(ks[8], (DEPTH, D_MODEL), 0.01),
        'w_in': _nrm(ks[9], (DEPTH, D_MODEL, IN_COLS), D_MODEL ** -0.5),
        'mla_q_norm_g': 1.0 + _nrm(ks[10], (DEPTH, MLA_Q_LORA), 0.01),
        'mla_w_uq': _nrm(ks[11], (DEPTH, MLA_Q_LORA, MLA_HEADS * (MLA_NOPE + MLA_ROPE)), MLA_Q_LORA ** -0.5),
        'mla_kv_norm_g': 1.0 + _nrm(ks[12], (DEPTH, MLA_KV_LORA), 0.01),
        'mla_w_ukv': _nrm(ks[13], (DEPTH, MLA_KV_LORA, MLA_HEADS * (MLA_NOPE + MLA_V)), MLA_KV_LORA ** -0.5),
        'hgrn_lb_logits': _nrm(ks[14], (DEPTH, HG_HEADS * HG_DK), 1.0),
        'hgrn_norm_g': 1.0 + _nrm(ks[15], (DEPTH, HG_DV), 0.01),
        'diff_lambda': _nrm(ks[16], (DEPTH, 4, DF_DH), 0.1),
        'diff_norm_g': 1.0 + _nrm(ks[17], (DEPTH, DF_DV), 0.01),
        'w_branch': _nrm(ks[18], (DEPTH, N_BRANCH, BR_WIDTH, D_MODEL), BR_WIDTH ** -0.5),
        'w_out': _nrm(ks[19], (DEPTH, D_MODEL, D_MODEL), D_MODEL ** -0.5),
        'norm_ffn_g': 1.0 + _nrm(ks[20], (DEPTH, D_MODEL), 0.01),
        'ffn_w_up': _nrm(ks[21], (DEPTH, D_MODEL, 2 * D_FF), D_MODEL ** -0.5),
        'ffn_conv_w': _nrm(ks[22], (DEPTH, CONV_W, D_FF), CONV_W ** -0.5),
        'ffn_conv_b': _nrm(ks[23], (DEPTH, D_FF), 0.01),
        'ffn_w_down': _nrm(ks[24], (DEPTH, D_FF, D_MODEL), D_FF ** -0.5),
        'norm_final_g': 1.0 + _nrm(ks[25], (D_MODEL,), 0.01),
    }


def _stack(states, i):
    return jnp.stack([s[i] for s in states], axis=0)


def reference(x_prompt, x_sample, cache_mla_ckv, cache_mla_krope, cache_diff_k, cache_diff_v,
              state_hgrn, state_ffn_conv, norm_mix_g, w_in, mla_q_norm_g, mla_w_uq, mla_kv_norm_g,
              mla_w_ukv, hgrn_lb_logits, hgrn_norm_g, diff_lambda, diff_norm_g, w_branch, w_out,
              norm_ffn_g, ffn_w_up, ffn_conv_w, ffn_conv_b, ffn_w_down, norm_final_g):
    B, T = x_prompt.shape[0], x_prompt.shape[1]
    Ts = x_sample.shape[1]
    P = cache_mla_ckv.shape[2]
    pos_p = jnp.arange(T)
    pos_s = P + jnp.arange(Ts)
    lb_soft = jax.nn.softmax(hgrn_lb_logits.astype(jnp.float32), axis=0)
    lb_all = jnp.cumsum(lb_soft, axis=0) - lb_soft[0]
    yp, ys = x_prompt, x_sample
    sp, ss = [], []
    for l in range(DEPTH):
        prm = (norm_mix_g[l], w_in[l], mla_q_norm_g[l], mla_w_uq[l], mla_kv_norm_g[l], mla_w_ukv[l],
               hgrn_norm_g[l], diff_lambda[l], diff_norm_g[l], w_branch[l], w_out[l], norm_ffn_g[l],
               ffn_w_up[l], ffn_conv_w[l], ffn_conv_b[l], ffn_w_down[l])
        yp, st_p = _layer(l, yp, pos_p, None,
                          jnp.zeros((B, HG_HEADS, HG_DK, HG_DV), jnp.float32),
                          jnp.zeros((B, CONV_W - 1, D_FF), x_prompt.dtype), lb_all[l], prm)
        sp.append(st_p)
        ys, st_s = _layer(l, ys, pos_s,
                          (cache_mla_ckv[l], cache_mla_krope[l], cache_diff_k[l], cache_diff_v[l]),
                          state_hgrn[l], state_ffn_conv[l], lb_all[l], prm)
        ss.append(st_s)
    y_prompt = _rms(yp, norm_final_g)
    y_sample = _rms(ys, norm_final_g)
    p_ckv, p_krope, p_dk, p_dv = _stack(sp, 0), _stack(sp, 1), _stack(sp, 2), _stack(sp, 3)
    p_hgrn, p_conv = _stack(sp, 4), _stack(sp, 5)
    s_ckv, s_krope, s_dk, s_dv = _stack(ss, 0), _stack(ss, 1), _stack(ss, 2), _stack(ss, 3)
    s_hgrn, s_conv = _stack(ss, 4), _stack(ss, 5)
    return (y_prompt, y_sample, p_ckv, p_krope, p_dk, p_dv, p_hgrn, p_conv,
            s_ckv, s_krope, s_dk, s_dv, s_hgrn, s_conv)
```

```python
import functools
import math

import jax
import jax.numpy as jnp
from jax import lax
from jax.experimental import pallas as pl
from jax.experimental.pallas import tpu as pltpu

CHUNK = 64
EPS = 1e-6
NEG = -1e30
F_MIN = 1e-12
MLA_HEADS = 8
MLA_NOPE = 64
MLA_ROPE = 32
MLA_V = 64
MLA_Q_LORA = 384
MLA_KV_LORA = 256
ROPE_THETA = 10000.0
MLA_SCALE = (MLA_NOPE + MLA_ROPE) ** -0.5
HG_HEADS = 8
HG_DK = 64
HG_DV = 64
DF_HEADS = 8
DF_DH = 32
DF_DV = 2 * DF_DH
DF_SCALE = DF_DH ** -0.5
N_BRANCH = 3
BR_WIDTH = 512
CONV_W = 3
LOG2E = 1.4426950408889634

LANE = 128
VMEM_LIMIT = 56 * 1024 * 1024
KV_BLOCK = 256
HG_CHUNK = 64
FF_BLOCK = 256

BF16 = jnp.bfloat16
F32 = jnp.float32


def _cparams(sem):
    return pltpu.CompilerParams(dimension_semantics=sem, vmem_limit_bytes=VMEM_LIMIT)


def _dot(a, b):
    return jnp.dot(a, b, preferred_element_type=F32)


def _dot_nt(a, b):
    return lax.dot_general(a, b, (((1,), (1,)), ((), ())), preferred_element_type=F32)


def _rms(x, g):
    r = lax.rsqrt(jnp.mean(x * x, axis=-1, keepdims=True) + EPS)
    return x * r * g


def _sigmoid(x):
    return 1.0 / (1.0 + jnp.exp(-x))


def _const_spec(shape):
    nd = len(shape)
    return pl.BlockSpec(shape, lambda *_: (0,) * nd)


def _mla_proj_kernel(x_ref, gmix_ref, wm_ref, gq_ref, wq_ref, gkv_ref, cq_ref, sq_ref, ck_ref, sk_ref,
                     q_ref, ckv_ref, krot_ref):
    x = x_ref[0]
    h = _rms(x, gmix_ref[...]).astype(BF16)
    z = _dot(h, wm_ref[...])
    qn = _rms(z[:, :MLA_Q_LORA], gq_ref[...]).astype(BF16)
    q2 = _dot(qn, wq_ref[...])
    nq = MLA_HEADS * LANE
    cq = jnp.concatenate([cq_ref[...]] * MLA_HEADS, axis=1)
    sq = jnp.concatenate([sq_ref[...]] * MLA_HEADS, axis=1)
    q_ref[0] = (q2[:, :nq] * cq + q2[:, nq:] * sq).astype(BF16)
    ckv_ref[0] = _rms(z[:, MLA_Q_LORA:MLA_Q_LORA + MLA_KV_LORA], gkv_ref[...])
    o = MLA_Q_LORA + MLA_KV_LORA
    kr = z[:, o:o + LANE] * ck_ref[...] + z[:, o + LANE:o + 2 * LANE] * sk_ref[...]
    krot_ref[0] = kr[:, :MLA_ROPE]


def _mla_proj(x, gmix, wm, gq, wq, gkv, cq, sq, ck, sk, tm):
    B, T, D = x.shape
    grid = (B, T // tm)
    tok = lambda w: pl.BlockSpec((1, tm, w), lambda b, t: (b, t, 0))
    tab = pl.BlockSpec((tm, LANE), lambda b, t: (t, 0))
    return pl.pallas_call(
        _mla_proj_kernel,
        grid=grid,
        in_specs=[tok(D), _const_spec(gmix.shape), _const_spec(wm.shape), _const_spec(gq.shape),
                  _const_spec(wq.shape), _const_spec(gkv.shape), tab, tab, tab, tab],
        out_specs=[tok(MLA_HEADS * LANE), tok(MLA_KV_LORA), tok(MLA_ROPE)],
        out_shape=[jax.ShapeDtypeStruct((B, T, MLA_HEADS * LANE), BF16),
                   jax.ShapeDtypeStruct((B, T, MLA_KV_LORA), F32),
                   jax.ShapeDtypeStruct((B, T, MLA_ROPE), F32)],
        compiler_params=_cparams(("parallel", "parallel")),
        name="mla_proj",
    )(x, gmix, wm, gq, wq, gkv, cq, sq, ck, sk)


def _mla_kv_kernel(ckv_ref, krot_ref, wk_ref, e_ref, wvt_ref, k_ref, vt_ref):
    c = ckv_ref[0].astype(BF16)
    kr = krot_ref[0].astype(BF16)
    k_ref[0] = (_dot(c, wk_ref[...]) + _dot(kr, e_ref[...])).astype(BF16)
    vt_ref[0, 0] = _dot_nt(wvt_ref[...], c).astype(BF16)


def _mla_kv(ckv, krot, wk, e, wvt):
    B, Tk, _ = ckv.shape
    nkb = Tk // KV_BLOCK
    return pl.pallas_call(
        _mla_kv_kernel,
        grid=(B, nkb),
        in_specs=[pl.BlockSpec((1, KV_BLOCK, MLA_KV_LORA), lambda b, t: (b, t, 0)),
                  pl.BlockSpec((1, KV_BLOCK, MLA_ROPE), lambda b, t: (b, t, 0)),
                  _const_spec(wk.shape), _const_spec(e.shape), _const_spec(wvt.shape)],
        out_specs=[pl.BlockSpec((1, KV_BLOCK, MLA_HEADS * LANE), lambda b, t: (b, t, 0)),
                   pl.BlockSpec((1, 1, MLA_HEADS * MLA_V, KV_BLOCK), lambda b, t: (b, t, 0, 0))],
        out_shape=[jax.ShapeDtypeStruct((B, Tk, MLA_HEADS * LANE), BF16),
                   jax.ShapeDtypeStruct((B, nkb, MLA_HEADS * MLA_V, KV_BLOCK), BF16)],
        compiler_params=_cparams(("parallel", "parallel")),
        name="mla_kv",
    )(ckv, krot, wk, e, wvt)


def _visible_blocks(q0, tq, kvalid, nkb):
    n_full = jnp.minimum((((q0 >> 6) + 1) * CHUNK) // KV_BLOCK, kvalid // KV_BLOCK)
    last = (((q0 + tq - 1) >> 6) + 1) * CHUNK
    n_vis = jnp.minimum((last + KV_BLOCK - 1) // KV_BLOCK, nkb)
    return n_full, n_vis


def _flash_t(qm, k_ref, ksl, vt_ref, vsl, n_full, n_vis, q0, tq, kvalid):
    ones = jnp.ones((16, KV_BLOCK), BF16)

    def step(kb, carry, masked):
        m, acc = carry
        k0 = pl.multiple_of(kb * KV_BLOCK, KV_BLOCK)
        s = _dot_nt(k_ref[0, pl.ds(k0, KV_BLOCK), ksl], qm)
        if masked:
            kpos = k0 + lax.broadcasted_iota(jnp.int32, (KV_BLOCK, tq), 0)
            qpos = q0 + lax.broadcasted_iota(jnp.int32, (KV_BLOCK, tq), 1)
            kchunk = jnp.where(kpos < kvalid, kpos >> 6, jnp.int32(2 ** 30))
            s = jnp.where(kchunk <= (qpos >> 6), s, NEG)
        m_new = jnp.maximum(m, jnp.max(s, axis=0, keepdims=True))
        alpha = jnp.exp2(m - m_new)
        p = jnp.exp2(s - m_new).astype(BF16)
        vt = jnp.concatenate([vt_ref[0, kb, vsl, :], ones], axis=0)
        return m_new, alpha * acc + _dot(vt, p)

    carry = (jnp.full((1, tq), -jnp.inf, F32), jnp.zeros((MLA_V + 16, tq), F32))
    carry = lax.fori_loop(0, n_full, lambda kb, c: step(kb, c, False), carry)
    carry = lax.fori_loop(n_full, n_vis, lambda kb, c: step(kb, c, True), carry)
    return carry[1]


def _attn_mla_kernel(q_ref, k_ref, vt_ref, o_ref, *, tq, q_off, kvalid, nkb):
    q0 = q_off + pl.program_id(2) * tq
    n_full, n_vis = _visible_blocks(q0, tq, kvalid, nkb)
    outs = []
    for hh in range(2):
        lanes = slice(hh * LANE, (hh + 1) * LANE)
        acc = _flash_t(q_ref[0, :, lanes], k_ref, lanes, vt_ref, slice(hh * MLA_V, (hh + 1) * MLA_V),
                       n_full, n_vis, q0, tq, kvalid)
        outs.append(acc[:MLA_V] * (1.0 / acc[MLA_V:MLA_V + 1]))
    o_ref[0] = jnp.concatenate(outs, axis=0).T.astype(BF16)


def _attn_mla(q, k, vt, tq, q_off, kvalid):
    B, T, _ = q.shape
    Tk = k.shape[1]
    nkb = Tk // KV_BLOCK
    kern = functools.partial(_attn_mla_kernel, tq=tq, q_off=q_off, kvalid=kvalid, nkb=nkb)
    return pl.pallas_call(
        kern,
        grid=(B, MLA_HEADS // 2, T // tq),
        in_specs=[pl.BlockSpec((1, tq, 2 * LANE), lambda b, p, i: (b, i, p)),
                  pl.BlockSpec((1, Tk, 2 * LANE), lambda b, p, i: (b, 0, p)),
                  pl.BlockSpec((1, nkb, 2 * MLA_V, KV_BLOCK), lambda b, p, i: (b, 0, p, 0))],
        out_specs=pl.BlockSpec((1, tq, 2 * MLA_V), lambda b, p, i: (b, i, p)),
        out_shape=jax.ShapeDtypeStruct((B, T, MLA_HEADS * MLA_V), BF16),
        compiler_params=_cparams(("parallel", "parallel", "arbitrary")),
        name="attn_mla",
    )(q, k, vt)


def _attn_diff_kernel(q_ref, k_ref, vt_ref, lam_ref, g_ref, o_ref, *, tq, q_off, kvalid, nkb, lam_init):
    q0 = q_off + pl.program_id(2) * tq
    n_full, n_vis = _visible_blocks(q0, tq, kvalid, nkb)
    lamv = lam_ref[...]
    lam = (jnp.exp(jnp.sum(lamv[0:1] * lamv[1:2], axis=-1, keepdims=True))
           - jnp.exp(jnp.sum(lamv[2:3] * lamv[3:4], axis=-1, keepdims=True)) + lam_init)
    q = q_ref[0].astype(F32)
    lane = lax.broadcasted_iota(jnp.int32, (tq, LANE), 1)
    outs = []
    for hh in range(2):
        maps = []
        for j in range(2):
            lo = hh * 2 * DF_DH + j * DF_DH
            qm = jnp.where((lane >= lo) & (lane < lo + DF_DH), q, 0.0).astype(BF16)
            acc = _flash_t(qm, k_ref, slice(0, LANE), vt_ref, slice(hh * DF_DV, (hh + 1) * DF_DV),
                           n_full, n_vis, q0, tq, kvalid)
            maps.append(acc[:DF_DV] * (1.0 / acc[DF_DV:DF_DV + 1]))
        o = maps[0] - lam * maps[1]
        outs.append(o * lax.rsqrt(jnp.mean(o * o, axis=0, keepdims=True) + EPS))
    ot = jnp.concatenate(outs, axis=0).T
    o_ref[0] = (ot * g_ref[...] * (1.0 - lam_init)).astype(BF16)


def _attn_diff(q, k, vt, lam_rows, g2, tq, q_off, kvalid, lam_init):
    B, T, _ = q.shape
    Tk = k.shape[1]
    nkb = Tk // KV_BLOCK
    kern = functools.partial(_attn_diff_kernel, tq=tq, q_off=q_off, kvalid=kvalid, nkb=nkb,
                             lam_init=lam_init)
    return pl.pallas_call(
        kern,
        grid=(B, DF_HEADS // 2, T // tq),
        in_specs=[pl.BlockSpec((1, tq, LANE), lambda b, p, i: (b, i, p)),
                  pl.BlockSpec((1, Tk, LANE), lambda b, p, i: (b, 0, p)),
                  pl.BlockSpec((1, nkb, 2 * DF_DV, KV_BLOCK), lambda b, p, i: (b, 0, p, 0)),
                  _const_spec(lam_rows.shape), _const_spec(g2.shape)],
        out_specs=pl.BlockSpec((1, tq, 2 * DF_DV), lambda b, p, i: (b, i, p)),
        out_shape=jax.ShapeDtypeStruct((B, T, DF_HEADS * DF_DV), BF16),
        compiler_params=_cparams(("parallel", "parallel", "arbitrary")),
        name="attn_diff",
    )(q, k, vt, lam_rows, g2)


def _diff_proj_kernel(x_ref, gmix_ref, wd_ref, q_ref, dk_ref, dv_ref):
    h = _rms(x_ref[0], gmix_ref[...]).astype(BF16)
    z = _dot(h, wd_ref[...])
    n = DF_HEADS * 2 * DF_DH
    q_ref[0] = (z[:, :n] * (DF_SCALE * LOG2E)).astype(BF16)
    dk_ref[0] = z[:, n:2 * n]
    dv_ref[0] = z[:, 2 * n:]


def _diff_proj(x, gmix, wd, tm):
    B, T, D = x.shape
    n = DF_HEADS * 2 * DF_DH
    tok = lambda w: pl.BlockSpec((1, tm, w), lambda b, t: (b, t, 0))
    return pl.pallas_call(
        _diff_proj_kernel,
        grid=(B, T // tm),
        in_specs=[tok(D), _const_spec(gmix.shape), _const_spec(wd.shape)],
        out_specs=[tok(n), tok(n), tok(n)],
        out_shape=[jax.ShapeDtypeStruct((B, T, n), BF16),
                   jax.ShapeDtypeStruct((B, T, n), F32),
                   jax.ShapeDtypeStruct((B, T, n), F32)],
        compiler_params=_cparams(("parallel", "parallel")),
        name="diff_proj",
    )(x, gmix, wd)


def _diff_kv_kernel(dk_ref, dv_ref, k_ref, vt_ref):
    k_ref[0] = dk_ref[0].astype(BF16)
    vt_ref[0, 0] = dv_ref[0].T.astype(BF16)


def _diff_kv(dk, dv):
    B, Tk, n = dk.shape
    nkb = Tk // KV_BLOCK
    tok = pl.BlockSpec((1, KV_BLOCK, n), lambda b, t: (b, t, 0))
    return pl.pallas_call(
        _diff_kv_kernel,
        grid=(B, nkb),
        in_specs=[tok, tok],
        out_specs=[tok, pl.BlockSpec((1, 1, n, KV_BLOCK), lambda b, t: (b, t, 0, 0))],
        out_shape=[jax.ShapeDtypeStruct((B, Tk, n), BF16),
                   jax.ShapeDtypeStruct((B, nkb, n, KV_BLOCK), BF16)],
        compiler_params=_cparams(("parallel", "parallel")),
        name="diff_kv",
    )(dk, dv)


def _split3(x):
    a = x.astype(BF16)
    r = x - a.astype(F32)
    b = r.astype(BF16)
    c = (r - b.astype(F32)).astype(BF16)
    return a, b, c


def _level_ref_rows(b, size):
    half = size // 2
    C = b.shape[0]
    pieces = [jnp.broadcast_to(b[i * size + half - 1:i * size + half, :], (size, b.shape[1]))
              for i in range(C // size)]
    return pieces[0] if len(pieces) == 1 else jnp.concatenate(pieces, axis=0)


def _base_ref_rows(b):
    C, W = b.shape
    sub = lax.broadcasted_iota(jnp.int32, (8, W), 0)
    pieces = [jnp.where(sub < 4, jnp.broadcast_to(b[8 * i:8 * i + 1, :], (8, W)),
                        jnp.broadcast_to(b[8 * i + 4:8 * i + 5, :], (8, W))) for i in range(C // 8)]
    return jnp.concatenate(pieces, axis=0)


def _hgrn_kernel(x_ref, gmix_ref, wh_ref, lbl_ref, st0_ref, gh_ref, gavg_ref, ob_ref, st_ref, z_s,
                 *, layer, tm):
    C = min(HG_CHUNK, tm)
    W = HG_HEADS * HG_DK
    ti = pl.program_id(1)

    @pl.when(ti == 0)
    def _():
        st_ref[0] = st0_ref[0]

    h = _rms(x_ref[0], gmix_ref[...]).astype(BF16)
    z_s[...] = _dot(h, wh_ref[...])

    lg = lbl_ref[...]
    e = jnp.exp(lg - jnp.max(lg, axis=0, keepdims=True))
    sm = e / jnp.sum(e, axis=0, keepdims=True)
    lb = jnp.zeros((1, W), F32)
    for i in range(1, layer + 1):
        lb = lb + sm[i:i + 1]

    row = lax.broadcasted_iota(jnp.int32, (C, C), 0)
    col = lax.broadcasted_iota(jnp.int32, (C, C), 1)
    tril = (col <= row).astype(BF16)
    rsub = lax.broadcasted_iota(jnp.int32, (C, LANE), 0)
    lane = lax.broadcasted_iota(jnp.int32, (C, LANE), 1)
    head_lo = lane < HG_DK
    sizes = [s for s in (64, 32, 16, 8) if s <= C]
    blk_masks = {s: (row // s) == (col // s) for s in sizes}
    base_mask = ((row // 4) == (col // 4)) & (col <= row)
    diag = ((lax.broadcasted_iota(jnp.int32, (LANE, LANE), 0) // HG_DK)
            == (lax.broadcasted_iota(jnp.int32, (LANE, LANE), 1) // HG_DK))

    def chunk(c, carry):
        r0 = pl.multiple_of(c * C, C)
        hq = z_s[pl.ds(r0, C), 0:W]
        hf = z_s[pl.ds(r0, C), W:2 * W]
        v = z_s[pl.ds(r0, C), 2 * W:3 * W]
        hg = z_s[pl.ds(r0, C), 3 * W:4 * W]
        sig_neg = _sigmoid(-hf)
        f = _sigmoid(hf) + lb * sig_neg
        logf = jnp.log(jnp.maximum(f, F_MIN))
        k = (1.0 - lb) * sig_neg
        q = hq * _sigmoid(hq)
        l1, l2, l3 = _split3(logf)
        b = _dot(tril, l1) + _dot(tril, l2) + _dot(tril, l3)
        b_end = b[C - 1:C, :]
        qe = q * jnp.exp(b)
        ke = k * jnp.exp(b_end - b)
        d_end = jnp.exp(b_end)

        o_pairs = []
        for p in range(HG_HEADS // 2):
            sl = slice(p * LANE, (p + 1) * LANE)
            bp, qp, kp, vp = b[:, sl], q[:, sl], k[:, sl], v[:, sl]
            a_h = [jnp.zeros((C, C), F32), jnp.zeros((C, C), F32)]
            for s in sizes:
                ref = _level_ref_rows(bp, s)
                upper = (rsub & (s - 1)) >= (s // 2)
                fq = jnp.exp(jnp.where(upper, bp - ref, NEG))
                fk = jnp.exp(jnp.where(upper, NEG, ref - bp))
                qt = qp * fq
                kt = (kp * fk).astype(BF16)
                for hh in range(2):
                    qm = jnp.where(head_lo if hh == 0 else ~head_lo, qt, 0.0).astype(BF16)
                    pr = _dot_nt(qm, kt)
                    a_h[hh] = a_h[hh] + (pr if s == C else jnp.where(blk_masks[s], pr, 0.0))
            ref = _base_ref_rows(bp)
            qt = qp * jnp.exp(bp - ref)
            kt = (kp * jnp.exp(ref - bp)).astype(BF16)
            for hh in range(2):
                qm = jnp.where(head_lo if hh == 0 else ~head_lo, qt, 0.0).astype(BF16)
                a_h[hh] = a_h[hh] + jnp.where(base_mask, _dot_nt(qm, kt), 0.0)
            o_p = (_dot(a_h[0].astype(BF16), jnp.where(head_lo, vp, 0.0).astype(BF16))
                   + _dot(a_h[1].astype(BF16), jnp.where(head_lo, 0.0, vp).astype(BF16)))
            st = st_ref[0, p]
            o_p = o_p + _dot_nt(qe[:, sl].astype(BF16), st.astype(BF16))
            upd = _dot(vp.T.astype(BF16), ke[:, sl].astype(BF16))
            st_ref[0, p] = st * d_end[:, sl] + jnp.where(diag, upd, 0.0)
            o_pairs.append(o_p)
        o = jnp.concatenate(o_pairs, axis=1)
        o2 = o * o
        o2h = o2.astype(BF16)
        o2l = (o2 - o2h.astype(F32)).astype(BF16)
        ms = _dot(o2h, gavg_ref[...]) + _dot(o2l, gavg_ref[...])
        ob = o * lax.rsqrt(ms + EPS) * gh_ref[...] * (hg * _sigmoid(hg))
        ob_ref[0, pl.ds(r0, C), :] = ob.astype(BF16)
        return carry

    lax.fori_loop(0, tm // C, chunk, 0)


def _hgrn(x, gmix, wh, lbl, st0, gh, gavg, layer, tm):
    B, T, D = x.shape
    W = HG_HEADS * HG_DK
    st_spec = pl.BlockSpec((1, HG_HEADS // 2, LANE, LANE), lambda b, t: (b, 0, 0, 0))
    return pl.pallas_call(
        functools.partial(_hgrn_kernel, layer=layer, tm=tm),
        grid=(B, T // tm),
        in_specs=[pl.BlockSpec((1, tm, D), lambda b, t: (b, t, 0)), _const_spec(gmix.shape),
                  _const_spec(wh.shape), _const_spec(lbl.shape), st_spec, _const_spec(gh.shape),
                  _const_spec(gavg.shape)],
        out_specs=[pl.BlockSpec((1, tm, W), lambda b, t: (b, t, 0)), st_spec],
        out_shape=[jax.ShapeDtypeStruct((B, T, W), BF16),
                   jax.ShapeDtypeStruct((B, HG_HEADS // 2, LANE, LANE), F32)],
        scratch_shapes=[pltpu.VMEM((tm, 4 * W), F32)],
        compiler_params=_cparams(("parallel", "arbitrary")),
        name="hgrn",
    )(x, gmix, wh, lbl, st0, gh, gavg)


def _post_kernel(x_ref, oa_ref, ob_ref, oc_ref, gmix_ref, wg_ref, wb_ref, wo_ref, gffn_ref, wup_ref,
                 cw_ref, cb_ref, wdn_ref, conv0_ref, gfin_ref, y_ref, conv_ref, carry_s, *, tm, d_ff, final):
    ti = pl.program_id(1)
    x = x_ref[0]
    D = x.shape[1]
    h = _rms(x, gmix_ref[...]).astype(BF16)
    mixed = jnp.zeros((tm, D), F32)
    for n, o_ref in enumerate((oa_ref, ob_ref, oc_ref)):
        gate = _sigmoid(_dot(h, wg_ref[:, n * D:(n + 1) * D]))
        mixed = mixed + gate * _dot(o_ref[0], wb_ref[n])
    x1 = x + _dot(mixed.astype(BF16), wo_ref[...])
    xn = _rms(x1, gffn_ref[...]).astype(BF16)

    @pl.when(ti == 0)
    def _():
        carry_s[0:CONV_W - 1, :] = conv0_ref[0]

    rows = lax.broadcasted_iota(jnp.int32, (tm, FF_BLOCK), 0)
    acc = jnp.zeros((tm, D), F32)
    for f in range(0, d_ff, FF_BLOCK):
        a = _dot(xn, wup_ref[:, f:f + FF_BLOCK])
        vv = _dot(xn, wup_ref[:, d_ff + f:d_ff + f + FF_BLOCK])
        p0 = carry_s[0:1, f:f + FF_BLOCK]
        p1 = carry_s[1:2, f:f + FF_BLOCK]
        r1 = pltpu.roll(a, 1, 0)
        r2 = pltpu.roll(a, 2, 0)
        a1 = jnp.where(rows < 1, p1, r1)
        a2 = jnp.where(rows < 1, p0, jnp.where(rows < 2, p1, r2))
        cc = (cb_ref[:, f:f + FF_BLOCK] + a2 * cw_ref[0:1, f:f + FF_BLOCK]
              + a1 * cw_ref[1:2, f:f + FF_BLOCK] + a * cw_ref[2:3, f:f + FF_BLOCK])
        act = (cc * _sigmoid(cc) * vv).astype(BF16)
        acc = acc + _dot(act, wdn_ref[f:f + FF_BLOCK, :])
        last2 = r2[0:8, :]
        carry_s[:, f:f + FF_BLOCK] = last2
        conv_ref[0, :, f:f + FF_BLOCK] = last2[0:CONV_W - 1, :]
    x2 = x1 + acc
    y_ref[0] = _rms(x2, gfin_ref[...]) if final else x2


def _post(x, oa, ob, oc, gmix, wg, wb, wo, gffn, wup, cw, cb, wdn, conv0, gfin, tm, final):
    B, T, D = x.shape
    d_ff = wdn.shape[0]
    tok = lambda w: pl.BlockSpec((1, tm, w), lambda b, t: (b, t, 0))
    cspec = pl.BlockSpec((1, CONV_W - 1, d_ff), lambda b, t: (b, 0, 0))
    one = lambda a: pl.BlockSpec(a.shape, lambda *_: (0,) * a.ndim, pipeline_mode=pl.Buffered(1))
    return pl.pallas_call(
        functools.partial(_post_kernel, tm=tm, d_ff=d_ff, final=final),
        grid=(B, T // tm),
        in_specs=[tok(D), tok(BR_WIDTH), tok(BR_WIDTH), tok(BR_WIDTH), one(gmix), one(wg), one(wb), one(wo),
                  one(gffn), one(wup), one(cw), one(cb), one(wdn), cspec, one(gfin)],
        out_specs=[tok(D), cspec],
        out_shape=[jax.ShapeDtypeStruct((B, T, D), F32),
                   jax.ShapeDtypeStruct((B, CONV_W - 1, d_ff), F32)],
        scratch_shapes=[pltpu.VMEM((8, d_ff), F32)],
        compiler_params=_cparams(("parallel", "arbitrary")),
        name="post",
    )(x, oa, ob, oc, gmix, wg, wb, wo, gffn, wup, cw, cb, wdn, conv0, gfin)


def _rot_half_cols(w):
    half = w.shape[-1] // 2
    return jnp.concatenate([-w[..., half:], w[..., :half]], axis=-1)


def _pad_cols(w, width):
    return jnp.pad(w, [(0, 0)] * (w.ndim - 1) + [(0, width - w.shape[-1])])


def _layer_weights(w_in, w_uq, w_ukv):
    D = w_in.shape[0]
    o_q, o_kv, o_kr = 0, MLA_Q_LORA, MLA_Q_LORA + MLA_KV_LORA
    o_h = o_kr + MLA_ROPE
    o_d = o_h + 4 * HG_HEADS * HG_DK
    o_g = o_d + 3 * DF_HEADS * DF_DV
    w_kr = w_in[:, o_kr:o_h]
    wm = jnp.concatenate([w_in[:, o_q:o_kr], _pad_cols(w_kr, LANE), _pad_cols(_rot_half_cols(w_kr), LANE)],
                         axis=1)
    wh = w_in[:, o_h:o_d]
    wd = w_in[:, o_d:o_g]
    wg = w_in[:, o_g:]
    uq = w_uq.reshape(MLA_Q_LORA, MLA_HEADS, MLA_NOPE + MLA_ROPE)
    nope, rope = uq[..., :MLA_NOPE], uq[..., MLA_NOPE:]
    zr = jnp.zeros((MLA_Q_LORA, MLA_HEADS, LANE - MLA_NOPE - MLA_ROPE), w_uq.dtype)
    wq1 = jnp.concatenate([nope, rope, zr], axis=-1).reshape(MLA_Q_LORA, MLA_HEADS * LANE)
    wq2 = jnp.concatenate([jnp.zeros_like(nope), _rot_half_cols(rope), zr], axis=-1)
    wq = jnp.concatenate([wq1, wq2.reshape(MLA_Q_LORA, MLA_HEADS * LANE)], axis=1)
    ukv = w_ukv.reshape(MLA_KV_LORA, MLA_HEADS, MLA_NOPE + MLA_V)
    wk = _pad_cols(ukv[..., :MLA_NOPE], LANE).reshape(MLA_KV_LORA, MLA_HEADS * LANE)
    wvt = ukv[..., MLA_NOPE:].reshape(MLA_KV_LORA, MLA_HEADS * MLA_V).T
    c = lambda a: a.astype(BF16)
    return c(wm), c(wh), c(wd), c(wg), c(wq), c(wk), c(wvt)


def _rope_tables(pos):
    half = MLA_ROPE // 2
    inv = ROPE_THETA ** (-jnp.arange(half, dtype=F32) / half)
    ang = pos.astype(F32)[:, None] * inv[None, :]
    cos2 = jnp.concatenate([jnp.cos(ang)] * 2, axis=1)
    sin2 = jnp.concatenate([jnp.sin(ang)] * 2, axis=1)
    T = pos.shape[0]
    c = MLA_SCALE * LOG2E
    zq = jnp.zeros((T, LANE - MLA_NOPE - MLA_ROPE), F32)
    cq = jnp.concatenate([jnp.full((T, MLA_NOPE), c, F32), c * cos2, zq], axis=1)
    sq = jnp.concatenate([jnp.zeros((T, MLA_NOPE), F32), c * sin2, zq], axis=1)
    return cq, sq, _pad_cols(cos2, LANE), _pad_cols(sin2, LANE)


def _rope_place():
    r = jnp.arange(MLA_ROPE)
    e = jnp.zeros((MLA_ROPE, MLA_HEADS, LANE), F32)
    e = e.at[r, :, MLA_NOPE + r].set(1.0)
    return e.reshape(MLA_ROPE, MLA_HEADS * LANE).astype(BF16)


def _pair_state(s):
    B = s.shape[0]
    st = jnp.swapaxes(s, -1, -2).reshape(B, HG_HEADS // 2, 2, HG_DV, HG_DK)
    z = jnp.zeros_like(st[:, :, 0])
    top = jnp.concatenate([st[:, :, 0], z], axis=-1)
    bot = jnp.concatenate([z, st[:, :, 1]], axis=-1)
    return jnp.concatenate([top, bot], axis=-2)


def _unpair_state(sp):
    a = sp[:, :, :HG_DV, :HG_DK]
    b = sp[:, :, HG_DV:, HG_DK:]
    st = jnp.stack([a, b], axis=2).reshape(sp.shape[0], HG_HEADS, HG_DV, HG_DK)
    return jnp.swapaxes(st, -1, -2)


def _pad_rows(a, n):
    return jnp.pad(a, ((0, 0), (0, n - a.shape[1]), (0, 0)))


def _layer(l, x, pos, past, s0, conv0, lw, prm, final):
    (norm_mix_g, mla_q_norm_g, mla_kv_norm_g, hgrn_lb_logits, hgrn_norm_g, diff_lambda, diff_norm_g,
     w_branch, w_out, norm_ffn_g, ffn_w_up, ffn_conv_w, ffn_conv_b, ffn_w_down, norm_final_g) = prm
    wm, wh, wd, wg, wq, wk, wvt = lw
    B, T, D = x.shape
    row = lambda a: a.reshape(1, -1)
    gmix = row(norm_mix_g)
    tm = min(512, T)

    n_tok = B * T
    tmf = min(512, n_tok)
    xf = x.reshape(1, n_tok, D)
    cq, sq, ck, sk = _rope_tables(jnp.tile(pos, B) if tmf > T else pos)
    if tmf <= T:
        q, ckv, krot = _mla_proj(x, gmix, wm, row(mla_q_norm_g), wq, row(mla_kv_norm_g), cq, sq, ck, sk, tmf)
        dq, dk, dv = _diff_proj(x, gmix, wd, tmf)
    else:
        q, ckv, krot = _mla_proj(xf, gmix, wm, row(mla_q_norm_g), wq, row(mla_kv_norm_g), cq, sq, ck, sk, tmf)
        dq, dk, dv = _diff_proj(xf, gmix, wd, tmf)
        q, ckv, krot, dq, dk, dv = (a.reshape(B, T, -1) for a in (q, ckv, krot, dq, dk, dv))

    if past is None:
        q_off, kvalid = 0, T
        ckv_all, krot_all, dk_all, dv_all = ckv, krot, dk, dv
        tq = min(KV_BLOCK, T)
        q_in, dq_in = q, dq
    else:
        P = past[0].shape[1]
        q_off, kvalid = P, P + T
        kpad = -(-kvalid // KV_BLOCK) * KV_BLOCK
        cat = lambda a, b: _pad_rows(jnp.concatenate([a.reshape(B, P, -1), b], axis=1), kpad)
        ckv_all, krot_all = cat(past[0], ckv), cat(past[1], krot)
        dk_all, dv_all = cat(past[2], dk), cat(past[3], dv)
        tq = LANE
        q_in, dq_in = _pad_rows(q, tq), _pad_rows(dq, tq)
    k_mla, vt_mla = _mla_kv(ckv_all, krot_all, wk, _rope_place(), wvt)
    o_a = _attn_mla(q_in, k_mla, vt_mla, tq, q_off, kvalid)[:, :T]
    k_d, vt_d = _diff_kv(dk_all, dv_all)
    lam_init = 0.8 - 0.6 * math.exp(-0.3 * l)
    o_c = _attn_diff(dq_in, k_d, vt_d, diff_lambda, row(jnp.tile(diff_norm_g, 2)), tq, q_off, kvalid,
                     lam_init)[:, :T]

    gavg = jnp.kron(jnp.eye(HG_HEADS, dtype=F32), jnp.full((HG_DV, HG_DV), 1.0 / HG_DV, F32)).astype(BF16)
    o_b, st = _hgrn(x, gmix, wh, hgrn_lb_logits, _pair_state(s0), row(jnp.tile(hgrn_norm_g, HG_HEADS)),
                    gavg, l, tm)

    y, conv_new = _post(x, o_a, o_b, o_c, gmix, wg, w_branch.astype(BF16), w_out.astype(BF16),
                        row(norm_ffn_g), ffn_w_up.astype(BF16), ffn_conv_w, row(ffn_conv_b),
                        ffn_w_down.astype(BF16), conv0, row(norm_final_g), tm, final)
    nh = lambda a, w: a.reshape(B, T, -1, w)
    return y, (ckv, krot, nh(dk, 2 * DF_DH), nh(dv, DF_DV), _unpair_state(st), conv_new)


def kernel(x_prompt, x_sample, cache_mla_ckv, cache_mla_krope, cache_diff_k, cache_diff_v, state_hgrn,
           state_ffn_conv, norm_mix_g, w_in, mla_q_norm_g, mla_w_uq, mla_kv_norm_g, mla_w_ukv, hgrn_lb_logits,
           hgrn_norm_g, diff_lambda, diff_norm_g, w_branch, w_out, norm_ffn_g, ffn_w_up, ffn_conv_w,
           ffn_conv_b, ffn_w_down, norm_final_g):
    depth = w_in.shape[0]
    B, T, _ = x_prompt.shape
    Bs, Ts, _ = x_sample.shape
    P = cache_mla_ckv.shape[2]
    d_ff = ffn_w_down.shape[1]
    pos_p = jnp.arange(T)
    pos_s = P + jnp.arange(Ts)
    yp, ys = x_prompt, x_sample
    sp, ss = [], []
    for l in range(depth):
        lw = _layer_weights(w_in[l], mla_w_uq[l], mla_w_ukv[l])
        prm = (norm_mix_g[l], mla_q_norm_g[l], mla_kv_norm_g[l], hgrn_lb_logits, hgrn_norm_g[l], diff_lambda[l],
               diff_norm_g[l], w_branch[l], w_out[l], norm_ffn_g[l], ffn_w_up[l], ffn_conv_w[l], ffn_conv_b[l],
               ffn_w_down[l], norm_final_g)
        final = l == depth - 1
        yp, st_p = _layer(l, yp, pos_p, None, jnp.zeros((B, HG_HEADS, HG_DK, HG_DV), F32),
                          jnp.zeros((B, CONV_W - 1, d_ff), F32), lw, prm, final)
        sp.append(st_p)
        ys, st_s = _layer(l, ys, pos_s,
                          (cache_mla_ckv[l], cache_mla_krope[l], cache_diff_k[l], cache_diff_v[l]),
                          state_hgrn[l], state_ffn_conv[l], lw, prm, final)
        ss.append(st_s)
    stk = lambda states, i: jnp.stack([s[i] for s in states], axis=0)
    return ((yp, ys) + tuple(stk(sp, i) for i in range(6)) + tuple(stk(ss, i) for i in range(6)))
```

```python
import functools
import math

import jax
import jax.numpy as jnp
from jax import lax
from jax.experimental import pallas as pl
from jax.experimental.pallas import tpu as pltpu

CHUNK = 64
EPS = 1e-6
NEG = -1e30
F_MIN = 1e-12
MLA_HEADS = 8
MLA_NOPE = 64
MLA_ROPE = 32
MLA_V = 64
MLA_Q_LORA = 384
MLA_KV_LORA = 256
ROPE_THETA = 10000.0
MLA_SCALE = (MLA_NOPE + MLA_ROPE) ** -0.5
HG_HEADS = 8
HG_DK = 64
HG_DV = 64
DF_HEADS = 8
DF_DH = 32
DF_DV = 2 * DF_DH
DF_SCALE = DF_DH ** -0.5
N_BRANCH = 3
BR_WIDTH = 512
CONV_W = 3
LOG2E = 1.4426950408889634

LANE = 128
VMEM_LIMIT = 56 * 1024 * 1024
KV_BLOCK = 256
HG_CHUNK = 64
FF_BLOCK = 256
MLA_HEADS_PER_STEP = 8
DF_PAIRS_PER_STEP = 2
FULL_BLOCKS_PER_STEP = 2

BF16 = jnp.bfloat16
F32 = jnp.float32


def _cparams(sem):
    return pltpu.CompilerParams(dimension_semantics=sem, vmem_limit_bytes=VMEM_LIMIT)


def _dot(a, b):
    return jnp.dot(a, b, preferred_element_type=F32)


def _dot_nt(a, b):
    return lax.dot_general(a, b, (((1,), (1,)), ((), ())), preferred_element_type=F32)


def _rms(x, g):
    r = lax.rsqrt(jnp.mean(x * x, axis=-1, keepdims=True) + EPS)
    return x * r * g


def _sigmoid(x):
    return 1.0 / (1.0 + jnp.exp(-x))


def _const_spec(shape):
    nd = len(shape)
    return pl.BlockSpec(shape, lambda *_: (0,) * nd)


def _mla_proj_kernel(x_ref, gmix_ref, wm_ref, gq_ref, wq_ref, gkv_ref, cq_ref, sq_ref, ck_ref, sk_ref,
                     q_ref, ckv_ref, krot_ref):
    x = x_ref[0]
    h = _rms(x, gmix_ref[...]).astype(BF16)
    z = _dot(h, wm_ref[...])
    qn = _rms(z[:, :MLA_Q_LORA], gq_ref[...]).astype(BF16)
    q2 = _dot(qn, wq_ref[...])
    nq = MLA_HEADS * LANE
    cq = jnp.concatenate([cq_ref[...]] * MLA_HEADS, axis=1)
    sq = jnp.concatenate([sq_ref[...]] * MLA_HEADS, axis=1)
    q_ref[0] = (q2[:, :nq] * cq + q2[:, nq:] * sq).astype(BF16)
    ckv_ref[0] = _rms(z[:, MLA_Q_LORA:MLA_Q_LORA + MLA_KV_LORA], gkv_ref[...])
    o = MLA_Q_LORA + MLA_KV_LORA
    kr = z[:, o:o + LANE] * ck_ref[...] + z[:, o + LANE:o + 2 * LANE] * sk_ref[...]
    krot_ref[0] = kr[:, :MLA_ROPE]


def _mla_proj(x, gmix, wm, gq, wq, gkv, cq, sq, ck, sk, tm):
    B, T, D = x.shape
    grid = (B, T // tm)
    tok = lambda w: pl.BlockSpec((1, tm, w), lambda b, t: (b, t, 0))
    tab = pl.BlockSpec((tm, LANE), lambda b, t: (t, 0))
    return pl.pallas_call(
        _mla_proj_kernel,
        grid=grid,
        in_specs=[tok(D), _const_spec(gmix.shape), _const_spec(wm.shape), _const_spec(gq.shape),
                  _const_spec(wq.shape), _const_spec(gkv.shape), tab, tab, tab, tab],
        out_specs=[tok(MLA_HEADS * LANE), tok(MLA_KV_LORA), tok(MLA_ROPE)],
        out_shape=[jax.ShapeDtypeStruct((B, T, MLA_HEADS * LANE), BF16),
                   jax.ShapeDtypeStruct((B, T, MLA_KV_LORA), F32),
                   jax.ShapeDtypeStruct((B, T, MLA_ROPE), F32)],
        compiler_params=_cparams(("parallel", "parallel")),
        name="mla_proj",
    )(x, gmix, wm, gq, wq, gkv, cq, sq, ck, sk)


def _mla_kv_kernel(ckv_ref, krot_ref, wk_ref, e_ref, wvt_ref, k_ref, vt_ref):
    c = ckv_ref[0].astype(BF16)
    kr = krot_ref[0].astype(BF16)
    k_ref[0] = (_dot(c, wk_ref[...]) + _dot(kr, e_ref[...])).astype(BF16)
    vt_ref[0, 0] = _dot_nt(wvt_ref[...], c).astype(BF16)


def _mla_kv(ckv, krot, wk, e, wvt):
    B, Tk, _ = ckv.shape
    nkb = Tk // KV_BLOCK
    return pl.pallas_call(
        _mla_kv_kernel,
        grid=(B, nkb),
        in_specs=[pl.BlockSpec((1, KV_BLOCK, MLA_KV_LORA), lambda b, t: (b, t, 0)),
                  pl.BlockSpec((1, KV_BLOCK, MLA_ROPE), lambda b, t: (b, t, 0)),
                  _const_spec(wk.shape), _const_spec(e.shape), _const_spec(wvt.shape)],
        out_specs=[pl.BlockSpec((1, KV_BLOCK, MLA_HEADS * LANE), lambda b, t: (b, t, 0)),
                   pl.BlockSpec((1, 1, MLA_HEADS * MLA_V, KV_BLOCK), lambda b, t: (b, t, 0, 0))],
        out_shape=[jax.ShapeDtypeStruct((B, Tk, MLA_HEADS * LANE), BF16),
                   jax.ShapeDtypeStruct((B, nkb, MLA_HEADS * MLA_V, KV_BLOCK), BF16)],
        compiler_params=_cparams(("parallel", "parallel")),
        name="mla_kv",
    )(ckv, krot, wk, e, wvt)


def _visible_blocks(q0, tq, kvalid, nkb):
    n_full = jnp.minimum((((q0 >> 6) + 1) * CHUNK) // KV_BLOCK, kvalid // KV_BLOCK)
    last = (((q0 + tq - 1) >> 6) + 1) * CHUNK
    n_vis = jnp.minimum((last + KV_BLOCK - 1) // KV_BLOCK, nkb)
    return n_full, n_vis


def _flash_t(streams, k_ref, vt_ref, n_full, n_vis, q0, tq, kvalid):
    ones = jnp.ones((16, KV_BLOCK), BF16)
    n = len(streams)

    def step(kbs, carry, masked):
        ss = []
        for qm, ksl, _ in streams:
            ss.append([_dot_nt(k_ref[0, pl.ds(pl.multiple_of(kb * KV_BLOCK, KV_BLOCK), KV_BLOCK), ksl], qm)
                       for kb in kbs])
        if masked:
            qchunk = (q0 + lax.broadcasted_iota(jnp.int32, (KV_BLOCK, tq), 1)) >> 6
            for j, kb in enumerate(kbs):
                kpos = kb * KV_BLOCK + lax.broadcasted_iota(jnp.int32, (KV_BLOCK, tq), 0)
                vis = jnp.where(kpos < kvalid, kpos >> 6, jnp.int32(2 ** 30)) <= qchunk
                for s in ss:
                    s[j] = jnp.where(vis, s[j], NEG)
        ms = []
        for s, (m, _) in zip(ss, carry):
            for sj in s:
                m = jnp.maximum(m, jnp.max(sj, axis=0, keepdims=True))
            ms.append(m)
        ps = [[jnp.exp2(sj - m_new).astype(BF16) for sj in s] for s, m_new in zip(ss, ms)]
        out = []
        for (_, _, vsl), (m, acc), m_new, p in zip(streams, carry, ms, ps):
            acc = jnp.exp2(m - m_new) * acc
            for kb, pj in zip(kbs, p):
                vt = jnp.concatenate([vt_ref[0, kb, vsl, :], ones], axis=0)
                acc = acc + _dot(vt, pj)
            out.append((m_new, acc))
        return tuple(out)

    carry = tuple((jnp.full((1, tq), -jnp.inf, F32), jnp.zeros((MLA_V + 16, tq), F32)) for _ in range(n))
    nb = FULL_BLOCKS_PER_STEP
    n_multi = n_full // nb
    carry = lax.fori_loop(0, n_multi, lambda j, c: step([j * nb + i for i in range(nb)], c, False), carry)
    carry = lax.fori_loop(n_multi * nb, n_vis, lambda kb, c: step([kb], c, True), carry)
    return [c[1] for c in carry]


def _attn_mla_kernel(q_ref, k_ref, vt_ref, o_ref, *, tq, q_off, kvalid, nkb):
    q0 = q_off + pl.program_id(2) * tq
    n_full, n_vis = _visible_blocks(q0, tq, kvalid, nkb)
    streams = []
    for hh in range(MLA_HEADS_PER_STEP):
        lanes = slice(hh * LANE, (hh + 1) * LANE)
        streams.append((q_ref[0, :, lanes], lanes, slice(hh * MLA_V, (hh + 1) * MLA_V)))
    accs = _flash_t(streams, k_ref, vt_ref, n_full, n_vis, q0, tq, kvalid)
    outs = [acc[:MLA_V] * (1.0 / acc[MLA_V:MLA_V + 1]) for acc in accs]
    for i in range(0, MLA_HEADS_PER_STEP, 2):
        o_ref[0, :, i * MLA_V:(i + 2) * MLA_V] = jnp.concatenate(outs[i:i + 2], axis=0).T.astype(BF16)


def _attn_mla(q, k, vt, tq, q_off, kvalid):
    B, T, _ = q.shape
    Tk = k.shape[1]
    nkb = Tk // KV_BLOCK
    hps = MLA_HEADS_PER_STEP
    kern = functools.partial(_attn_mla_kernel, tq=tq, q_off=q_off, kvalid=kvalid, nkb=nkb)
    return pl.pallas_call(
        kern,
        grid=(B, MLA_HEADS // hps, T // tq),
        in_specs=[pl.BlockSpec((1, tq, hps * LANE), lambda b, p, i: (b, i, p)),
                  pl.BlockSpec((1, Tk, hps * LANE), lambda b, p, i: (b, 0, p)),
                  pl.BlockSpec((1, nkb, hps * MLA_V, KV_BLOCK), lambda b, p, i: (b, 0, p, 0))],
        out_specs=pl.BlockSpec((1, tq, hps * MLA_V), lambda b, p, i: (b, i, p)),
        out_shape=jax.ShapeDtypeStruct((B, T, MLA_HEADS * MLA_V), BF16),
        compiler_params=_cparams(("parallel", "parallel", "arbitrary")),
        name="attn_mla",
    )(q, k, vt)


def _attn_diff_kernel(q_ref, k_ref, vt_ref, lam_ref, g_ref, o_ref, *, tq, q_off, kvalid, nkb, lam_init):
    q0 = q_off + pl.program_id(2) * tq
    n_full, n_vis = _visible_blocks(q0, tq, kvalid, nkb)
    lamv = lam_ref[...]
    lam = (jnp.exp(jnp.sum(lamv[0:1] * lamv[1:2], axis=-1, keepdims=True))
           - jnp.exp(jnp.sum(lamv[2:3] * lamv[3:4], axis=-1, keepdims=True)) + lam_init)
    lane = lax.broadcasted_iota(jnp.int32, (tq, LANE), 1)
    streams = []
    for pr in range(DF_PAIRS_PER_STEP):
        lanes = slice(pr * LANE, (pr + 1) * LANE)
        q = q_ref[0, :, lanes].astype(F32)
        for hh in range(2):
            for j in range(2):
                lo = hh * 2 * DF_DH + j * DF_DH
                qm = jnp.where((lane >= lo) & (lane < lo + DF_DH), q, 0.0).astype(BF16)
                streams.append((qm, lanes, slice((2 * pr + hh) * DF_DV, (2 * pr + hh + 1) * DF_DV)))
    accs = _flash_t(streams, k_ref, vt_ref, n_full, n_vis, q0, tq, kvalid)
    maps = [acc[:DF_DV] * (1.0 / acc[DF_DV:DF_DV + 1]) for acc in accs]
    for pr in range(DF_PAIRS_PER_STEP):
        outs = []
        for hh in range(2):
            i = 4 * pr + 2 * hh
            o = maps[i] - lam * maps[i + 1]
            outs.append(o * lax.rsqrt(jnp.mean(o * o, axis=0, keepdims=True) + EPS))
        ot = jnp.concatenate(outs, axis=0).T
        o_ref[0, :, pr * LANE:(pr + 1) * LANE] = (ot * g_ref[...] * (1.0 - lam_init)).astype(BF16)


def _attn_diff(q, k, vt, lam_rows, g2, tq, q_off, kvalid, lam_init):
    B, T, _ = q.shape
    Tk = k.shape[1]
    nkb = Tk // KV_BLOCK
    pps = DF_PAIRS_PER_STEP
    kern = functools.partial(_attn_diff_kernel, tq=tq, q_off=q_off, kvalid=kvalid, nkb=nkb,
                             lam_init=lam_init)
    return pl.pallas_call(
        kern,
        grid=(B, DF_HEADS // (2 * pps), T // tq),
        in_specs=[pl.BlockSpec((1, tq, pps * LANE), lambda b, p, i: (b, i, p)),
                  pl.BlockSpec((1, Tk, pps * LANE), lambda b, p, i: (b, 0, p)),
                  pl.BlockSpec((1, nkb, pps * 2 * DF_DV, KV_BLOCK), lambda b, p, i: (b, 0, p, 0)),
                  _const_spec(lam_rows.shape), _const_spec(g2.shape)],
        out_specs=pl.BlockSpec((1, tq, pps * 2 * DF_DV), lambda b, p, i: (b, i, p)),
        out_shape=jax.ShapeDtypeStruct((B, T, DF_HEADS * DF_DV), BF16),
        compiler_params=_cparams(("parallel", "parallel", "arbitrary")),
        name="attn_diff",
    )(q, k, vt, lam_rows, g2)


def _diff_proj_kernel(x_ref, gmix_ref, wd_ref, q_ref, dk_ref, dv_ref):
    h = _rms(x_ref[0], gmix_ref[...]).astype(BF16)
    z = _dot(h, wd_ref[...])
    n = DF_HEADS * 2 * DF_DH
    q_ref[0] = (z[:, :n] * (DF_SCALE * LOG2E)).astype(BF16)
    dk_ref[0] = z[:, n:2 * n]
    dv_ref[0] = z[:, 2 * n:]


def _diff_proj(x, gmix, wd, tm):
    B, T, D = x.shape
    n = DF_HEADS * 2 * DF_DH
    tok = lambda w: pl.BlockSpec((1, tm, w), lambda b, t: (b, t, 0))
    return pl.pallas_call(
        _diff_proj_kernel,
        grid=(B, T // tm),
        in_specs=[tok(D), _const_spec(gmix.shape), _const_spec(wd.shape)],
        out_specs=[tok(n), tok(n), tok(n)],
        out_shape=[jax.ShapeDtypeStruct((B, T, n), BF16),
                   jax.ShapeDtypeStruct((B, T, n), F32),
                   jax.ShapeDtypeStruct((B, T, n), F32)],
        compiler_params=_cparams(("parallel", "parallel")),
        name="diff_proj",
    )(x, gmix, wd)


def _diff_kv_kernel(dk_ref, dv_ref, k_ref, vt_ref):
    k_ref[0] = dk_ref[0].astype(BF16)
    vt_ref[0, 0] = dv_ref[0].T.astype(BF16)


def _diff_kv(dk, dv):
    B, Tk, n = dk.shape
    nkb = Tk // KV_BLOCK
    tok = pl.BlockSpec((1, KV_BLOCK, n), lambda b, t: (b, t, 0))
    return pl.pallas_call(
        _diff_kv_kernel,
        grid=(B, nkb),
        in_specs=[tok, tok],
        out_specs=[tok, pl.BlockSpec((1, 1, n, KV_BLOCK), lambda b, t: (b, t, 0, 0))],
        out_shape=[jax.ShapeDtypeStruct((B, Tk, n), BF16),
                   jax.ShapeDtypeStruct((B, nkb, n, KV_BLOCK), BF16)],
        compiler_params=_cparams(("parallel", "parallel")),
        name="diff_kv",
    )(dk, dv)


def _split3(x):
    a = x.astype(BF16)
    r = x - a.astype(F32)
    b = r.astype(BF16)
    c = (r - b.astype(F32)).astype(BF16)
    return a, b, c


def _level_ref_rows(b, size):
    half = size // 2
    C = b.shape[0]
    pieces = [jnp.broadcast_to(b[i * size + half - 1:i * size + half, :], (size, b.shape[1]))
              for i in range(C // size)]
    return pieces[0] if len(pieces) == 1 else jnp.concatenate(pieces, axis=0)


def _base_ref_rows(b):
    C, W = b.shape
    sub = lax.broadcasted_iota(jnp.int32, (8, W), 0)
    pieces = [jnp.where(sub < 4, jnp.broadcast_to(b[8 * i:8 * i + 1, :], (8, W)),
                        jnp.broadcast_to(b[8 * i + 4:8 * i + 5, :], (8, W))) for i in range(C // 8)]
    return jnp.concatenate(pieces, axis=0)


def _hgrn_kernel(x_ref, gmix_ref, wh_ref, lbl_ref, st0_ref, gh_ref, gavg_ref, ob_ref, st_ref, z_s,
                 *, layer, tm):
    C = min(HG_CHUNK, tm)
    W = HG_HEADS * HG_DK
    ti = pl.program_id(1)

    @pl.when(ti == 0)
    def _():
        st_ref[0] = st0_ref[0]

    h = _rms(x_ref[0], gmix_ref[...]).astype(BF16)
    z_s[...] = _dot(h, wh_ref[...])

    lg = lbl_ref[...]
    e = jnp.exp(lg - jnp.max(lg, axis=0, keepdims=True))
    sm = e / jnp.sum(e, axis=0, keepdims=True)
    lb = jnp.zeros((1, W), F32)
    for i in range(1, layer + 1):
        lb = lb + sm[i:i + 1]

    row = lax.broadcasted_iota(jnp.int32, (C, C), 0)
    col = lax.broadcasted_iota(jnp.int32, (C, C), 1)
    tril = (col <= row).astype(BF16)
    rsub = lax.broadcasted_iota(jnp.int32, (C, LANE), 0)
    lane = lax.broadcasted_iota(jnp.int32, (C, LANE), 1)
    head_lo = lane < HG_DK
    sizes = [s for s in (64, 32, 16, 8) if s <= C]
    blk_masks = {s: (row // s) == (col // s) for s in sizes}
    base_mask = ((row // 4) == (col // 4)) & (col <= row)
    diag = ((lax.broadcasted_iota(jnp.int32, (LANE, LANE), 0) // HG_DK)
            == (lax.broadcasted_iota(jnp.int32, (LANE, LANE), 1) // HG_DK))

    def chunk(c, carry):
        r0 = pl.multiple_of(c * C, C)
        hq = z_s[pl.ds(r0, C), 0:W]
        hf = z_s[pl.ds(r0, C), W:2 * W]
        v = z_s[pl.ds(r0, C), 2 * W:3 * W]
        hg = z_s[pl.ds(r0, C), 3 * W:4 * W]
        sig_neg = _sigmoid(-hf)
        f = _sigmoid(hf) + lb * sig_neg
        logf = jnp.log(jnp.maximum(f, F_MIN))
        k = (1.0 - lb) * sig_neg
        q = hq * _sigmoid(hq)
        l1, l2, l3 = _split3(logf)
        b = _dot(tril, l1) + _dot(tril, l2) + _dot(tril, l3)
        b_end = b[C - 1:C, :]
        qe = q * jnp.exp(b)
        ke = k * jnp.exp(b_end - b)
        d_end = jnp.exp(b_end)

        o_pairs = []
        for p in range(HG_HEADS // 2):
            sl = slice(p * LANE, (p + 1) * LANE)
            bp, qp, kp, vp = b[:, sl], q[:, sl], k[:, sl], v[:, sl]
            a_h = [jnp.zeros((C, C), F32), jnp.zeros((C, C), F32)]
            for s in sizes:
                ref = _level_ref_rows(bp, s)
                upper = (rsub & (s - 1)) >= (s // 2)
                fq = jnp.exp(jnp.where(upper, bp - ref, NEG))
                fk = jnp.exp(jnp.where(upper, NEG, ref - bp))
                qt = qp * fq
                kt = (kp * fk).astype(BF16)
                for hh in range(2):
                    qm = jnp.where(head_lo if hh == 0 else ~head_lo, qt, 0.0).astype(BF16)
                    pr = _dot_nt(qm, kt)
                    a_h[hh] = a_h[hh] + (pr if s == C else jnp.where(blk_masks[s], pr, 0.0))
            ref = _base_ref_rows(bp)
            qt = qp * jnp.exp(bp - ref)
            kt = (kp * jnp.exp(ref - bp)).astype(BF16)
            for hh in range(2):
                qm = jnp.where(head_lo if hh == 0 else ~head_lo, qt, 0.0).astype(BF16)
                a_h[hh] = a_h[hh] + jnp.where(base_mask, _dot_nt(qm, kt), 0.0)
            o_p = (_dot(a_h[0].astype(BF16), jnp.where(head_lo, vp, 0.0).astype(BF16))
                   + _dot(a_h[1].astype(BF16), jnp.where(head_lo, 0.0, vp).astype(BF16)))
            st = st_ref[0, p]
            o_p = o_p + _dot_nt(qe[:, sl].astype(BF16), st.astype(BF16))
            upd = _dot(vp.T.astype(BF16), ke[:, sl].astype(BF16))
            st_ref[0, p] = st * d_end[:, sl] + jnp.where(diag, upd, 0.0)
            o_pairs.append(o_p)
        o = jnp.concatenate(o_pairs, axis=1)
        o2 = o * o
        o2h = o2.astype(BF16)
        o2l = (o2 - o2h.astype(F32)).astype(BF16)
        ms = _dot(o2h, gavg_ref[...]) + _dot(o2l, gavg_ref[...])
        ob = o * lax.rsqrt(ms + EPS) * gh_ref[...] * (hg * _sigmoid(hg))
        ob_ref[0, pl.ds(r0, C), :] = ob.astype(BF16)
        return carry

    lax.fori_loop(0, tm // C, chunk, 0)


def _hgrn(x, gmix, wh, lbl, st0, gh, gavg, layer, tm):
    B, T, D = x.shape
    W = HG_HEADS * HG_DK
    st_spec = pl.BlockSpec((1, HG_HEADS // 2, LANE, LANE), lambda b, t: (b, 0, 0, 0))
    return pl.pallas_call(
        functools.partial(_hgrn_kernel, layer=layer, tm=tm),
        grid=(B, T // tm),
        in_specs=[pl.BlockSpec((1, tm, D), lambda b, t: (b, t, 0)), _const_spec(gmix.shape),
                  _const_spec(wh.shape), _const_spec(lbl.shape), st_spec, _const_spec(gh.shape),
                  _const_spec(gavg.shape)],
        out_specs=[pl.BlockSpec((1, tm, W), lambda b, t: (b, t, 0)), st_spec],
        out_shape=[jax.ShapeDtypeStruct((B, T, W), BF16),
                   jax.ShapeDtypeStruct((B, HG_HEADS // 2, LANE, LANE), F32)],
        scratch_shapes=[pltpu.VMEM((tm, 4 * W), F32)],
        compiler_params=_cparams(("parallel", "arbitrary")),
        name="hgrn",
    )(x, gmix, wh, lbl, st0, gh, gavg)


def _post_kernel(x_ref, oa_ref, ob_ref, oc_ref, gmix_ref, wg_ref, wb_ref, wo_ref, gffn_ref, wup_ref,
                 cw_ref, cb_ref, wdn_ref, conv0_ref, gfin_ref, y_ref, conv_ref, carry_s, *, tm, d_ff, final):
    ti = pl.program_id(1)
    x = x_ref[0]
    D = x.shape[1]
    h = _rms(x, gmix_ref[...]).astype(BF16)
    mixed = jnp.zeros((tm, D), F32)
    for n, o_ref in enumerate((oa_ref, ob_ref, oc_ref)):
        gate = _sigmoid(_dot(h, wg_ref[:, n * D:(n + 1) * D]))
        mixed = mixed + gate * _dot(o_ref[0], wb_ref[n])
    x1 = x + _dot(mixed.astype(BF16), wo_ref[...])
    xn = _rms(x1, gffn_ref[...]).astype(BF16)

    @pl.when(ti == 0)
    def _():
        carry_s[0:CONV_W - 1, :] = conv0_ref[0]

    rows = lax.broadcasted_iota(jnp.int32, (tm, FF_BLOCK), 0)
    acc = jnp.zeros((tm, D), F32)
    for f in range(0, d_ff, FF_BLOCK):
        a = _dot(xn, wup_ref[:, f:f + FF_BLOCK])
        vv = _dot(xn, wup_ref[:, d_ff + f:d_ff + f + FF_BLOCK])
        p0 = carry_s[0:1, f:f + FF_BLOCK]
        p1 = carry_s[1:2, f:f + FF_BLOCK]
        r1 = pltpu.roll(a, 1, 0)
        r2 = pltpu.roll(a, 2, 0)
        a1 = jnp.where(rows < 1, p1, r1)
        a2 = jnp.where(rows < 1, p0, jnp.where(rows < 2, p1, r2))
        cc = (cb_ref[:, f:f + FF_BLOCK] + a2 * cw_ref[0:1, f:f + FF_BLOCK]
              + a1 * cw_ref[1:2, f:f + FF_BLOCK] + a * cw_ref[2:3, f:f + FF_BLOCK])
        act = (cc * _sigmoid(cc) * vv).astype(BF16)
        acc = acc + _dot(act, wdn_ref[f:f + FF_BLOCK, :])
        last2 = r2[0:8, :]
        carry_s[:, f:f + FF_BLOCK] = last2
        conv_ref[0, :, f:f + FF_BLOCK] = last2[0:CONV_W - 1, :]
    x2 = x1 + acc
    y_ref[0] = _rms(x2, gfin_ref[...]) if final else x2


def _post(x, oa, ob, oc, gmix, wg, wb, wo, gffn, wup, cw, cb, wdn, conv0, gfin, tm, final):
    B, T, D = x.shape
    d_ff = wdn.shape[0]
    tok = lambda w: pl.BlockSpec((1, tm, w), lambda b, t: (b, t, 0))
    cspec = pl.BlockSpec((1, CONV_W - 1, d_ff), lambda b, t: (b, 0, 0))
    one = lambda a: pl.BlockSpec(a.shape, lambda *_: (0,) * a.ndim, pipeline_mode=pl.Buffered(1))
    return pl.pallas_call(
        functools.partial(_post_kernel, tm=tm, d_ff=d_ff, final=final),
        grid=(B, T // tm),
        in_specs=[tok(D), tok(BR_WIDTH), tok(BR_WIDTH), tok(BR_WIDTH), one(gmix), one(wg), one(wb), one(wo),
                  one(gffn), one(wup), one(cw), one(cb), one(wdn), cspec, one(gfin)],
        out_specs=[tok(D), cspec],
        out_shape=[jax.ShapeDtypeStruct((B, T, D), F32),
                   jax.ShapeDtypeStruct((B, CONV_W - 1, d_ff), F32)],
        scratch_shapes=[pltpu.VMEM((8, d_ff), F32)],
        compiler_params=_cparams(("parallel", "arbitrary")),
        name="post",
    )(x, oa, ob, oc, gmix, wg, wb, wo, gffn, wup, cw, cb, wdn, conv0, gfin)


def _rot_half_cols(w):
    half = w.shape[-1] // 2
    return jnp.concatenate([-w[..., half:], w[..., :half]], axis=-1)


def _pad_cols(w, width):
    return jnp.pad(w, [(0, 0)] * (w.ndim - 1) + [(0, width - w.shape[-1])])


def _layer_weights(w_in, w_uq, w_ukv):
    D = w_in.shape[0]
    o_q, o_kv, o_kr = 0, MLA_Q_LORA, MLA_Q_LORA + MLA_KV_LORA
    o_h = o_kr + MLA_ROPE
    o_d = o_h + 4 * HG_HEADS * HG_DK
    o_g = o_d + 3 * DF_HEADS * DF_DV
    w_kr = w_in[:, o_kr:o_h]
    wm = jnp.concatenate([w_in[:, o_q:o_kr], _pad_cols(w_kr, LANE), _pad_cols(_rot_half_cols(w_kr), LANE)],
                         axis=1)
    wh = w_in[:, o_h:o_d]
    wd = w_in[:, o_d:o_g]
    wg = w_in[:, o_g:]
    uq = w_uq.reshape(MLA_Q_LORA, MLA_HEADS, MLA_NOPE + MLA_ROPE)
    nope, rope = uq[..., :MLA_NOPE], uq[..., MLA_NOPE:]
    zr = jnp.zeros((MLA_Q_LORA, MLA_HEADS, LANE - MLA_NOPE - MLA_ROPE), w_uq.dtype)
    wq1 = jnp.concatenate([nope, rope, zr], axis=-1).reshape(MLA_Q_LORA, MLA_HEADS * LANE)
    wq2 = jnp.concatenate([jnp.zeros_like(nope), _rot_half_cols(rope), zr], axis=-1)
    wq = jnp.concatenate([wq1, wq2.reshape(MLA_Q_LORA, MLA_HEADS * LANE)], axis=1)
    ukv = w_ukv.reshape(MLA_KV_LORA, MLA_HEADS, MLA_NOPE + MLA_V)
    wk = _pad_cols(ukv[..., :MLA_NOPE], LANE).reshape(MLA_KV_LORA, MLA_HEADS * LANE)
    wvt = ukv[..., MLA_NOPE:].reshape(MLA_KV_LORA, MLA_HEADS * MLA_V).T
    c = lambda a: a.astype(BF16)
    return c(wm), c(wh), c(wd), c(wg), c(wq), c(wk), c(wvt)


def _rope_tables(pos):
    half = MLA_ROPE // 2
    inv = ROPE_THETA ** (-jnp.arange(half, dtype=F32) / half)
    ang = pos.astype(F32)[:, None] * inv[None, :]
    cos2 = jnp.concatenate([jnp.cos(ang)] * 2, axis=1)
    sin2 = jnp.concatenate([jnp.sin(ang)] * 2, axis=1)
    T = pos.shape[0]
    c = MLA_SCALE * LOG2E
    zq = jnp.zeros((T, LANE - MLA_NOPE - MLA_ROPE), F32)
    cq = jnp.concatenate([jnp.full((T, MLA_NOPE), c, F32), c * cos2, zq], axis=1)
    sq = jnp.concatenate([jnp.zeros((T, MLA_NOPE), F32), c * sin2, zq], axis=1)
    return cq, sq, _pad_cols(cos2, LANE), _pad_cols(sin2, LANE)


def _rope_place():
    r = jnp.arange(MLA_ROPE)
    e = jnp.zeros((MLA_ROPE, MLA_HEADS, LANE), F32)
    e = e.at[r, :, MLA_NOPE + r].set(1.0)
    return e.reshape(MLA_ROPE, MLA_HEADS * LANE).astype(BF16)


def _pair_state(s):
    B = s.shape[0]
    st = jnp.swapaxes(s, -1, -2).reshape(B, HG_HEADS // 2, 2, HG_DV, HG_DK)
    z = jnp.zeros_like(st[:, :, 0])
    top = jnp.concatenate([st[:, :, 0], z], axis=-1)
    bot = jnp.concatenate([z, st[:, :, 1]], axis=-1)
    return jnp.concatenate([top, bot], axis=-2)


def _unpair_state(sp):
    a = sp[:, :, :HG_DV, :HG_DK]
    b = sp[:, :, HG_DV:, HG_DK:]
    st = jnp.stack([a, b], axis=2).reshape(sp.shape[0], HG_HEADS, HG_DV, HG_DK)
    return jnp.swapaxes(st, -1, -2)


def _pad_rows(a, n):
    return jnp.pad(a, ((0, 0), (0, n - a.shape[1]), (0, 0)))


def _layer(l, x, pos, past, s0, conv0, lw, prm, final):
    (norm_mix_g, mla_q_norm_g, mla_kv_norm_g, hgrn_lb_logits, hgrn_norm_g, diff_lambda, diff_norm_g,
     w_branch, w_out, norm_ffn_g, ffn_w_up, ffn_conv_w, ffn_conv_b, ffn_w_down, norm_final_g) = prm
    wm, wh, wd, wg, wq, wk, wvt = lw
    B, T, D = x.shape
    row = lambda a: a.reshape(1, -1)
    gmix = row(norm_mix_g)
    tm = min(512, T)

    n_tok = B * T
    tmf = min(512, n_tok)
    xf = x.reshape(1, n_tok, D)
    cq, sq, ck, sk = _rope_tables(jnp.tile(pos, B) if tmf > T else pos)
    if tmf <= T:
        q, ckv, krot = _mla_proj(x, gmix, wm, row(mla_q_norm_g), wq, row(mla_kv_norm_g), cq, sq, ck, sk, tmf)
        dq, dk, dv = _diff_proj(x, gmix, wd, tmf)
    else:
        q, ckv, krot = _mla_proj(xf, gmix, wm, row(mla_q_norm_g), wq, row(mla_kv_norm_g), cq, sq, ck, sk, tmf)
        dq, dk, dv = _diff_proj(xf, gmix, wd, tmf)
        q, ckv, krot, dq, dk, dv = (a.reshape(B, T, -1) for a in (q, ckv, krot, dq, dk, dv))

    if past is None:
        q_off, kvalid = 0, T
        ckv_all, krot_all, dk_all, dv_all = ckv, krot, dk, dv
        tq = min(KV_BLOCK, T)
        q_in, dq_in = q, dq
    else:
        P = past[0].shape[1]
        q_off, kvalid = P, P + T
        kpad = -(-kvalid // KV_BLOCK) * KV_BLOCK
        cat = lambda a, b: _pad_rows(jnp.concatenate([a.reshape(B, P, -1), b], axis=1), kpad)
        ckv_all, krot_all = cat(past[0], ckv), cat(past[1], krot)
        dk_all, dv_all = cat(past[2], dk), cat(past[3], dv)
        tq = LANE
        q_in, dq_in = _pad_rows(q, tq), _pad_rows(dq, tq)
    k_mla, vt_mla = _mla_kv(ckv_all, krot_all, wk, _rope_place(), wvt)
    o_a = _attn_mla(q_in, k_mla, vt_mla, tq, q_off, kvalid)[:, :T]
    k_d, vt_d = _diff_kv(dk_all, dv_all)
    lam_init = 0.8 - 0.6 * math.exp(-0.3 * l)
    o_c = _attn_diff(dq_in, k_d, vt_d, diff_lambda, row(jnp.tile(diff_norm_g, 2)), tq, q_off, kvalid,
                     lam_init)[:, :T]

    gavg = jnp.kron(jnp.eye(HG_HEADS, dtype=F32), jnp.full((HG_DV, HG_DV), 1.0 / HG_DV, F32)).astype(BF16)
    o_b, st = _hgrn(x, gmix, wh, hgrn_lb_logits, _pair_state(s0), row(jnp.tile(hgrn_norm_g, HG_HEADS)),
                    gavg, l, tm)

    y, conv_new = _post(x, o_a, o_b, o_c, gmix, wg, w_branch.astype(BF16), w_out.astype(BF16),
                        row(norm_ffn_g), ffn_w_up.astype(BF16), ffn_conv_w, row(ffn_conv_b),
                        ffn_w_down.astype(BF16), conv0, row(norm_final_g), tm, final)
    nh = lambda a, w: a.reshape(B, T, -1, w)
    return y, (ckv, krot, nh(dk, 2 * DF_DH), nh(dv, DF_DV), _unpair_state(st), conv_new)


def kernel(x_prompt, x_sample, cache_mla_ckv, cache_mla_krope, cache_diff_k, cache_diff_v, state_hgrn,
           state_ffn_conv, norm_mix_g, w_in, mla_q_norm_g, mla_w_uq, mla_kv_norm_g, mla_w_ukv, hgrn_lb_logits,
           hgrn_norm_g, diff_lambda, diff_norm_g, w_branch, w_out, norm_ffn_g, ffn_w_up, ffn_conv_w,
           ffn_conv_b, ffn_w_down, norm_final_g):
    depth = w_in.shape[0]
    B, T, _ = x_prompt.shape
    Bs, Ts, _ = x_sample.shape
    P = cache_mla_ckv.shape[2]
    d_ff = ffn_w_down.shape[1]
    pos_p = jnp.arange(T)
    pos_s = P + jnp.arange(Ts)
    yp, ys = x_prompt, x_sample
    sp, ss = [], []
    for l in range(depth):
        lw = _layer_weights(w_in[l], mla_w_uq[l], mla_w_ukv[l])
        prm = (norm_mix_g[l], mla_q_norm_g[l], mla_kv_norm_g[l], hgrn_lb_logits, hgrn_norm_g[l], diff_lambda[l],
               diff_norm_g[l], w_branch[l], w_out[l], norm_ffn_g[l], ffn_w_up[l], ffn_conv_w[l], ffn_conv_b[l],
               ffn_w_down[l], norm_final_g)
        final = l == depth - 1
        yp, st_p = _layer(l, yp, pos_p, None, jnp.zeros((B, HG_HEADS, HG_DK, HG_DV), F32),
                          jnp.zeros((B, CONV_W - 1, d_ff), F32), lw, prm, final)
        sp.append(st_p)
        ys, st_s = _layer(l, ys, pos_s,
                          (cache_mla_ckv[l], cache_mla_krope[l], cache_diff_k[l], cache_diff_v[l]),
                          state_hgrn[l], state_ffn_conv[l], lw, prm, final)
        ss.append(st_s)
    stk = lambda states, i: jnp.stack([s[i] for s in states], axis=0)
    return ((yp, ys) + tuple(stk(sp, i) for i in range(6)) + tuple(stk(ss, i) for i in range(6)))
```

```python
import functools
import math

import jax
import jax.numpy as jnp
from jax import lax
from jax.experimental import pallas as pl
from jax.experimental.pallas import tpu as pltpu

CHUNK = 64
EPS = 1e-6
NEG = -1e30
F_MIN = 1e-12
MLA_HEADS = 8
MLA_NOPE = 64
MLA_ROPE = 32
MLA_V = 64
MLA_Q_LORA = 384
MLA_KV_LORA = 256
ROPE_THETA = 10000.0
MLA_SCALE = (MLA_NOPE + MLA_ROPE) ** -0.5
HG_HEADS = 8
HG_DK = 64
HG_DV = 64
DF_HEADS = 8
DF_DH = 32
DF_DV = 2 * DF_DH
DF_SCALE = DF_DH ** -0.5
N_BRANCH = 3
BR_WIDTH = 512
CONV_W = 3
LOG2E = 1.4426950408889634

LANE = 128
VMEM_LIMIT = 56 * 1024 * 1024
KV_BLOCK = 256
HG_CHUNK = 64
HG_CHUNKS_PER_STEP = 2
HG_SINGLE_REF_LIMIT = 43.0
FF_BLOCK = 1024
MLA_HEADS_PER_STEP = 8
DF_PAIRS_PER_STEP = 2
FULL_BLOCKS_PER_STEP = 2

BF16 = jnp.bfloat16
F32 = jnp.float32


def _cparams(sem):
    return pltpu.CompilerParams(dimension_semantics=sem, vmem_limit_bytes=VMEM_LIMIT)


def _dot(a, b):
    return jnp.dot(a, b, preferred_element_type=F32)


def _dot_nt(a, b):
    return lax.dot_general(a, b, (((1,), (1,)), ((), ())), preferred_element_type=F32)


def _rms(x, g):
    r = lax.rsqrt(jnp.mean(x * x, axis=-1, keepdims=True) + EPS)
    return x * r * g


def _sigmoid(x):
    return 0.5 * jnp.tanh(0.5 * x) + 0.5


def _sigmoid_pair(x):
    e = jnp.exp(-jnp.abs(x))
    r = 1.0 / (1.0 + e)
    er = e * r
    pos = x >= 0.0
    return jnp.where(pos, r, er), jnp.where(pos, er, r)


def _const_spec(shape):
    nd = len(shape)
    return pl.BlockSpec(shape, lambda *_: (0,) * nd)


def _mla_qkv(h, wm_ref, gq_ref, wq_ref, gkv_ref, cq_ref, sq_ref, ck_ref, sk_ref):
    z = _dot(h, wm_ref[...])
    qn = _rms(z[:, :MLA_Q_LORA], gq_ref[...]).astype(BF16)
    q2 = _dot(qn, wq_ref[...])
    nq = MLA_HEADS * LANE
    cq = jnp.concatenate([cq_ref[...]] * MLA_HEADS, axis=1)
    sq = jnp.concatenate([sq_ref[...]] * MLA_HEADS, axis=1)
    q = (q2[:, :nq] * cq + q2[:, nq:] * sq).astype(BF16)
    ckv = _rms(z[:, MLA_Q_LORA:MLA_Q_LORA + MLA_KV_LORA], gkv_ref[...])
    o = MLA_Q_LORA + MLA_KV_LORA
    kr = z[:, o:o + LANE] * ck_ref[...] + z[:, o + LANE:o + 2 * LANE] * sk_ref[...]
    return q, ckv, kr[:, :MLA_ROPE]


def _mla_expand(ckv, krot, wk_ref, e_ref, wvt_ref):
    c = ckv.astype(BF16)
    k = (_dot(c, wk_ref[...]) + _dot(krot.astype(BF16), e_ref[...])).astype(BF16)
    return k, _dot_nt(wvt_ref[...], c).astype(BF16)


def _diff_cols(h, wd_ref):
    z = _dot(h, wd_ref[...])
    n = DF_HEADS * 2 * DF_DH
    return (z[:, :n] * (DF_SCALE * LOG2E)).astype(BF16), z[:, n:2 * n], z[:, 2 * n:]


def _proj_kernel(*refs, tm, aliased):
    (x_ref, gmix_ref, wm_ref, gq_ref, wq_ref, gkv_ref, cq_ref, sq_ref, ck_ref, sk_ref, wk_ref, e_ref, wvt_ref,
     wd_ref) = refs[:14]
    (q_ref, kc_ref, vtm_ref, dq_ref, kd_ref, vtd_ref, ckv_ref, krot_ref, dk_ref, dv_ref) = refs[14 + aliased:]
    h = _rms(x_ref[0], gmix_ref[...]).astype(BF16)
    q, ckv, krot = _mla_qkv(h, wm_ref, gq_ref, wq_ref, gkv_ref, cq_ref, sq_ref, ck_ref, sk_ref)
    k, vt = _mla_expand(ckv, krot, wk_ref, e_ref, wvt_ref)
    dq, dk, dv = _diff_cols(h, wd_ref)
    dvt = dv.T.astype(BF16)
    q_ref[0] = q
    kc_ref[0] = k
    dq_ref[0] = dq
    kd_ref[0] = dk.astype(BF16)
    for i in range(tm // KV_BLOCK):
        vtm_ref[0, i] = vt[:, i * KV_BLOCK:(i + 1) * KV_BLOCK]
        vtd_ref[0, i] = dvt[:, i * KV_BLOCK:(i + 1) * KV_BLOCK]
    ckv_ref[0, 0] = ckv
    krot_ref[0, 0] = krot
    dk_ref[0, 0] = dk
    dv_ref[0, 0] = dv


def _proj(x, gmix, wm, gq, wq, gkv, cq, sq, ck, sk, wk, e, wvt, wd, layer, depth, prev, tm):
    B, T, D = x.shape
    nd = DF_HEADS * 2 * DF_DH
    nk = tm // KV_BLOCK
    tok = lambda w: pl.BlockSpec((1, tm, w), lambda b, t: (b, t, 0))
    tab = pl.BlockSpec((tm, LANE), lambda b, t: (t, 0))
    vts = lambda r: pl.BlockSpec((1, nk, r, KV_BLOCK), lambda b, t: (b, t, 0, 0))
    st = lambda w: pl.BlockSpec((1, 1, tm, w), lambda b, t: (layer, b, t, 0))
    consts = (gmix, wm, gq, wq, gkv)
    consts2 = (wk, e, wvt, wd)
    in_specs = ([tok(D)] + [_const_spec(a.shape) for a in consts] + [tab] * 4
                + [_const_spec(a.shape) for a in consts2])
    args = (x,) + consts + (cq, sq, ck, sk) + consts2
    aliases = {}
    if prev is not None:
        in_specs += [pl.BlockSpec(memory_space=pl.ANY)] * 4
        aliases = {len(args) + i: 6 + i for i in range(4)}
        args += tuple(prev)
    widths = (MLA_KV_LORA, MLA_ROPE, nd, nd)
    outs = pl.pallas_call(
        functools.partial(_proj_kernel, tm=tm, aliased=4 if prev is not None else 0),
        grid=(B, T // tm),
        in_specs=in_specs,
        out_specs=[tok(MLA_HEADS * LANE), tok(MLA_HEADS * LANE), vts(MLA_HEADS * MLA_V), tok(nd), tok(nd), vts(nd)]
        + [st(w) for w in widths],
        out_shape=[jax.ShapeDtypeStruct((B, T, MLA_HEADS * LANE), BF16),
                   jax.ShapeDtypeStruct((B, T, MLA_HEADS * LANE), BF16),
                   jax.ShapeDtypeStruct((B, T // KV_BLOCK, MLA_HEADS * MLA_V, KV_BLOCK), BF16),
                   jax.ShapeDtypeStruct((B, T, nd), BF16),
                   jax.ShapeDtypeStruct((B, T, nd), BF16),
                   jax.ShapeDtypeStruct((B, T // KV_BLOCK, nd, KV_BLOCK), BF16)]
        + [jax.ShapeDtypeStruct((depth, B, T, w), F32) for w in widths],
        input_output_aliases=aliases,
        compiler_params=_cparams(("parallel", "parallel")),
        name="proj",
    )(*args)
    return outs[:6], outs[6:]


def _mla_proj_kernel(x_ref, gmix_ref, wm_ref, gq_ref, wq_ref, gkv_ref, cq_ref, sq_ref, ck_ref, sk_ref,
                     q_ref, ckv_ref, krot_ref):
    h = _rms(x_ref[0], gmix_ref[...]).astype(BF16)
    q_ref[0], ckv_ref[0], krot_ref[0] = _mla_qkv(h, wm_ref, gq_ref, wq_ref, gkv_ref, cq_ref, sq_ref, ck_ref,
                                                  sk_ref)


def _mla_proj(x, gmix, wm, gq, wq, gkv, cq, sq, ck, sk, tm):
    B, T, D = x.shape
    grid = (B, T // tm)
    tok = lambda w: pl.BlockSpec((1, tm, w), lambda b, t: (b, t, 0))
    tab = pl.BlockSpec((tm, LANE), lambda b, t: (t, 0))
    return pl.pallas_call(
        _mla_proj_kernel,
        grid=grid,
        in_specs=[tok(D), _const_spec(gmix.shape), _const_spec(wm.shape), _const_spec(gq.shape),
                  _const_spec(wq.shape), _const_spec(gkv.shape), tab, tab, tab, tab],
        out_specs=[tok(MLA_HEADS * LANE), tok(MLA_KV_LORA), tok(MLA_ROPE)],
        out_shape=[jax.ShapeDtypeStruct((B, T, MLA_HEADS * LANE), BF16),
                   jax.ShapeDtypeStruct((B, T, MLA_KV_LORA), F32),
                   jax.ShapeDtypeStruct((B, T, MLA_ROPE), F32)],
        compiler_params=_cparams(("parallel", "parallel")),
        name="mla_proj",
    )(x, gmix, wm, gq, wq, gkv, cq, sq, ck, sk)


def _mla_kv_kernel(ckv_ref, krot_ref, wk_ref, e_ref, wvt_ref, k_ref, vt_ref):
    k_ref[0], vt_ref[0, 0] = _mla_expand(ckv_ref[0], krot_ref[0], wk_ref, e_ref, wvt_ref)


def _mla_kv(ckv, krot, wk, e, wvt):
    B, Tk, _ = ckv.shape
    nkb = Tk // KV_BLOCK
    return pl.pallas_call(
        _mla_kv_kernel,
        grid=(B, nkb),
        in_specs=[pl.BlockSpec((1, KV_BLOCK, MLA_KV_LORA), lambda b, t: (b, t, 0)),
                  pl.BlockSpec((1, KV_BLOCK, MLA_ROPE), lambda b, t: (b, t, 0)),
                  _const_spec(wk.shape), _const_spec(e.shape), _const_spec(wvt.shape)],
        out_specs=[pl.BlockSpec((1, KV_BLOCK, MLA_HEADS * LANE), lambda b, t: (b, t, 0)),
                   pl.BlockSpec((1, 1, MLA_HEADS * MLA_V, KV_BLOCK), lambda b, t: (b, t, 0, 0))],
        out_shape=[jax.ShapeDtypeStruct((B, Tk, MLA_HEADS * LANE), BF16),
                   jax.ShapeDtypeStruct((B, nkb, MLA_HEADS * MLA_V, KV_BLOCK), BF16)],
        compiler_params=_cparams(("parallel", "parallel")),
        name="mla_kv",
    )(ckv, krot, wk, e, wvt)


def _visible_blocks(q0, tq, kvalid, nkb):
    n_full = jnp.minimum((((q0 >> 6) + 1) * CHUNK) // KV_BLOCK, kvalid // KV_BLOCK)
    last = (((q0 + tq - 1) >> 6) + 1) * CHUNK
    n_vis = jnp.minimum((last + KV_BLOCK - 1) // KV_BLOCK, nkb)
    return n_full, n_vis


def _flash_t(streams, k_ref, vt_ref, n_full, n_vis, q0, tq, kvalid):
    ones = jnp.ones((16, KV_BLOCK), BF16)
    n = len(streams)

    def step(kbs, carry, masked):
        ss = []
        for qm, ksl, _ in streams:
            ss.append([_dot_nt(k_ref[0, pl.ds(pl.multiple_of(kb * KV_BLOCK, KV_BLOCK), KV_BLOCK), ksl], qm)
                       for kb in kbs])
        if masked:
            qchunk = (q0 + lax.broadcasted_iota(jnp.int32, (KV_BLOCK, tq), 1)) >> 6
            for j, kb in enumerate(kbs):
                kpos = kb * KV_BLOCK + lax.broadcasted_iota(jnp.int32, (KV_BLOCK, tq), 0)
                vis = jnp.where(kpos < kvalid, kpos >> 6, jnp.int32(2 ** 30)) <= qchunk
                for s in ss:
                    s[j] = jnp.where(vis, s[j], NEG)
        ms = []
        for s, (m, _) in zip(ss, carry):
            for sj in s:
                m = jnp.maximum(m, jnp.max(sj, axis=0, keepdims=True))
            ms.append(m)
        ps = [[jnp.exp2(sj - m_new).astype(BF16) for sj in s] for s, m_new in zip(ss, ms)]
        out = []
        for (_, _, vsl), (m, acc), m_new, p in zip(streams, carry, ms, ps):
            acc = jnp.exp2(m - m_new) * acc
            for kb, pj in zip(kbs, p):
                vt = jnp.concatenate([vt_ref[0, kb, vsl, :], ones], axis=0)
                acc = acc + _dot(vt, pj)
            out.append((m_new, acc))
        return tuple(out)

    carry = tuple((jnp.full((1, tq), -jnp.inf, F32), jnp.zeros((MLA_V + 16, tq), F32)) for _ in range(n))
    nb = FULL_BLOCKS_PER_STEP
    n_multi = n_full // nb
    carry = lax.fori_loop(0, n_multi, lambda j, c: step([j * nb + i for i in range(nb)], c, False), carry)
    carry = lax.fori_loop(n_multi * nb, n_vis, lambda kb, c: step([kb], c, True), carry)
    return [c[1] for c in carry]


def _attn_mla_kernel(q_ref, k_ref, vt_ref, o_ref, *, tq, q_off, kvalid, nkb):
    q0 = q_off + pl.program_id(2) * tq
    n_full, n_vis = _visible_blocks(q0, tq, kvalid, nkb)
    streams = []
    for hh in range(MLA_HEADS_PER_STEP):
        lanes = slice(hh * LANE, (hh + 1) * LANE)
        streams.append((q_ref[0, :, lanes], lanes, slice(hh * MLA_V, (hh + 1) * MLA_V)))
    accs = _flash_t(streams, k_ref, vt_ref, n_full, n_vis, q0, tq, kvalid)
    outs = [acc[:MLA_V] * (1.0 / acc[MLA_V:MLA_V + 1]) for acc in accs]
    for i in range(0, MLA_HEADS_PER_STEP, 2):
        o_ref[0, :, i * MLA_V:(i + 2) * MLA_V] = jnp.concatenate(outs[i:i + 2], axis=0).T.astype(BF16)


def _attn_mla(q, k, vt, tq, q_off, kvalid):
    B, T, _ = q.shape
    Tk = k.shape[1]
    nkb = Tk // KV_BLOCK
    hps = MLA_HEADS_PER_STEP
    kern = functools.partial(_attn_mla_kernel, tq=tq, q_off=q_off, kvalid=kvalid, nkb=nkb)
    return pl.pallas_call(
        kern,
        grid=(B, MLA_HEADS // hps, T // tq),
        in_specs=[pl.BlockSpec((1, tq, hps * LANE), lambda b, p, i: (b, i, p)),
                  pl.BlockSpec((1, Tk, hps * LANE), lambda b, p, i: (b, 0, p)),
                  pl.BlockSpec((1, nkb, hps * MLA_V, KV_BLOCK), lambda b, p, i: (b, 0, p, 0))],
        out_specs=pl.BlockSpec((1, tq, hps * MLA_V), lambda b, p, i: (b, i, p)),
        out_shape=jax.ShapeDtypeStruct((B, T, MLA_HEADS * MLA_V), BF16),
        compiler_params=_cparams(("parallel", "parallel", "arbitrary")),
        name="attn_mla",
    )(q, k, vt)


def _attn_diff_kernel(q_ref, k_ref, vt_ref, lam_ref, g_ref, o_ref, *, tq, q_off, kvalid, nkb, lam_init):
    q0 = q_off + pl.program_id(2) * tq
    n_full, n_vis = _visible_blocks(q0, tq, kvalid, nkb)
    lamv = lam_ref[...]
    lam = (jnp.exp(jnp.sum(lamv[0:1] * lamv[1:2], axis=-1, keepdims=True))
           - jnp.exp(jnp.sum(lamv[2:3] * lamv[3:4], axis=-1, keepdims=True)) + lam_init)
    lane = lax.broadcasted_iota(jnp.int32, (tq, LANE), 1)
    streams = []
    for pr in range(DF_PAIRS_PER_STEP):
        lanes = slice(pr * LANE, (pr + 1) * LANE)
        q = q_ref[0, :, lanes].astype(F32)
        for hh in range(2):
            for j in range(2):
                lo = hh * 2 * DF_DH + j * DF_DH
                qm = jnp.where((lane >= lo) & (lane < lo + DF_DH), q, 0.0).astype(BF16)
                streams.append((qm, lanes, slice((2 * pr + hh) * DF_DV, (2 * pr + hh + 1) * DF_DV)))
    accs = _flash_t(streams, k_ref, vt_ref, n_full, n_vis, q0, tq, kvalid)
    maps = [acc[:DF_DV] * (1.0 / acc[DF_DV:DF_DV + 1]) for acc in accs]
    for pr in range(DF_PAIRS_PER_STEP):
        outs = []
        for hh in range(2):
            i = 4 * pr + 2 * hh
            o = maps[i] - lam * maps[i + 1]
            outs.append(o * lax.rsqrt(jnp.mean(o * o, axis=0, keepdims=True) + EPS))
        ot = jnp.concatenate(outs, axis=0).T
        o_ref[0, :, pr * LANE:(pr + 1) * LANE] = (ot * g_ref[...] * (1.0 - lam_init)).astype(BF16)


def _attn_diff(q, k, vt, lam_rows, g2, tq, q_off, kvalid, lam_init):
    B, T, _ = q.shape
    Tk = k.shape[1]
    nkb = Tk // KV_BLOCK
    pps = DF_PAIRS_PER_STEP
    kern = functools.partial(_attn_diff_kernel, tq=tq, q_off=q_off, kvalid=kvalid, nkb=nkb,
                             lam_init=lam_init)
    return pl.pallas_call(
        kern,
        grid=(B, DF_HEADS // (2 * pps), T // tq),
        in_specs=[pl.BlockSpec((1, tq, pps * LANE), lambda b, p, i: (b, i, p)),
                  pl.BlockSpec((1, Tk, pps * LANE), lambda b, p, i: (b, 0, p)),
                  pl.BlockSpec((1, nkb, pps * 2 * DF_DV, KV_BLOCK), lambda b, p, i: (b, 0, p, 0)),
                  _const_spec(lam_rows.shape), _const_spec(g2.shape)],
        out_specs=pl.BlockSpec((1, tq, pps * 2 * DF_DV), lambda b, p, i: (b, i, p)),
        out_shape=jax.ShapeDtypeStruct((B, T, DF_HEADS * DF_DV), BF16),
        compiler_params=_cparams(("parallel", "parallel", "arbitrary")),
        name="attn_diff",
    )(q, k, vt, lam_rows, g2)


def _diff_proj_kernel(x_ref, gmix_ref, wd_ref, q_ref, dk_ref, dv_ref):
    h = _rms(x_ref[0], gmix_ref[...]).astype(BF16)
    q_ref[0], dk_ref[0], dv_ref[0] = _diff_cols(h, wd_ref)


def _diff_proj(x, gmix, wd, tm):
    B, T, D = x.shape
    n = DF_HEADS * 2 * DF_DH
    tok = lambda w: pl.BlockSpec((1, tm, w), lambda b, t: (b, t, 0))
    return pl.pallas_call(
        _diff_proj_kernel,
        grid=(B, T // tm),
        in_specs=[tok(D), _const_spec(gmix.shape), _const_spec(wd.shape)],
        out_specs=[tok(n), tok(n), tok(n)],
        out_shape=[jax.ShapeDtypeStruct((B, T, n), BF16),
                   jax.ShapeDtypeStruct((B, T, n), F32),
                   jax.ShapeDtypeStruct((B, T, n), F32)],
        compiler_params=_cparams(("parallel", "parallel")),
        name="diff_proj",
    )(x, gmix, wd)


def _diff_kv_kernel(dk_ref, dv_ref, k_ref, vt_ref):
    k_ref[0] = dk_ref[0].astype(BF16)
    vt_ref[0, 0] = dv_ref[0].T.astype(BF16)


def _diff_kv(dk, dv):
    B, Tk, n = dk.shape
    nkb = Tk // KV_BLOCK
    tok = pl.BlockSpec((1, KV_BLOCK, n), lambda b, t: (b, t, 0))
    return pl.pallas_call(
        _diff_kv_kernel,
        grid=(B, nkb),
        in_specs=[tok, tok],
        out_specs=[tok, pl.BlockSpec((1, 1, n, KV_BLOCK), lambda b, t: (b, t, 0, 0))],
        out_shape=[jax.ShapeDtypeStruct((B, Tk, n), BF16),
                   jax.ShapeDtypeStruct((B, nkb, n, KV_BLOCK), BF16)],
        compiler_params=_cparams(("parallel", "parallel")),
        name="diff_kv",
    )(dk, dv)


def _split3(x):
    a = x.astype(BF16)
    r = x - a.astype(F32)
    b = r.astype(BF16)
    c = (r - b.astype(F32)).astype(BF16)
    return a, b, c


def _level_ref_rows(b, size):
    half = size // 2
    C = b.shape[0]
    pieces = [jnp.broadcast_to(b[i * size + half - 1:i * size + half, :], (size, b.shape[1]))
              for i in range(C // size)]
    return pieces[0] if len(pieces) == 1 else jnp.concatenate(pieces, axis=0)


def _base_ref_rows(b):
    C, W = b.shape
    sub = lax.broadcasted_iota(jnp.int32, (8, W), 0)
    pieces = [jnp.where(sub < 4, jnp.broadcast_to(b[8 * i:8 * i + 1, :], (8, W)),
                        jnp.broadcast_to(b[8 * i + 4:8 * i + 5, :], (8, W))) for i in range(C // 8)]
    return jnp.concatenate(pieces, axis=0)


def _hgrn_kernel(x_ref, gmix_ref, wh_ref, lbl_ref, st0_ref, gh_ref, gavg_ref, ob_ref, st_ref, z_s, b_s,
                 *, layer, tm):
    C = min(HG_CHUNK, tm)
    W = HG_HEADS * HG_DK
    ti = pl.program_id(1)

    @pl.when(ti == 0)
    def _():
        st_ref[0] = st0_ref[0]

    h = _rms(x_ref[0], gmix_ref[...]).astype(BF16)
    z_s[...] = _dot(h, wh_ref[...])

    lg = lbl_ref[...]
    e = jnp.exp(lg - jnp.max(lg, axis=0, keepdims=True))
    sm = e / jnp.sum(e, axis=0, keepdims=True)
    lb = jnp.zeros((1, W), F32)
    for i in range(1, layer + 1):
        lb = lb + sm[i:i + 1]

    row = lax.broadcasted_iota(jnp.int32, (C, C), 0)
    col = lax.broadcasted_iota(jnp.int32, (C, C), 1)
    tril_b = col <= row
    tril = tril_b.astype(BF16)
    rsub = lax.broadcasted_iota(jnp.int32, (C, LANE), 0)
    lane = lax.broadcasted_iota(jnp.int32, (C, LANE), 1)
    head_lo = lane < HG_DK
    sizes = [s for s in (64, 32, 16, 8) if s <= C]
    blk_masks = {s: (row // s) == (col // s) for s in sizes}
    base_mask = ((row // 4) == (col // 4)) & tril_b
    diag = ((lax.broadcasted_iota(jnp.int32, (LANE, LANE), 0) // HG_DK)
            == (lax.broadcasted_iota(jnp.int32, (LANE, LANE), 1) // HG_DK))
    mid = C // 2 - 1
    n_pairs = HG_HEADS // 2

    def prep(c, spread):
        r0 = pl.multiple_of(c * C, C)
        hq = z_s[pl.ds(r0, C), 0:W]
        hf = z_s[pl.ds(r0, C), W:2 * W]
        hg = z_s[pl.ds(r0, C), 3 * W:4 * W]
        sig, sig_neg = _sigmoid_pair(hf)
        f = sig + lb * sig_neg
        l1, l2, l3 = _split3(jnp.log(jnp.maximum(f, F_MIN)))
        b = _dot(tril, l1) + _dot(tril, l2) + _dot(tril, l3)
        z_s[pl.ds(r0, C), 0:W] = hq * _sigmoid(hq)
        z_s[pl.ds(r0, C), W:2 * W] = (1.0 - lb) * sig_neg
        z_s[pl.ds(r0, C), 3 * W:4 * W] = hg * _sigmoid(hg)
        b_s[pl.ds(r0, C), :] = b
        return jnp.maximum(spread, jnp.maximum(b[0:1] - b[mid:mid + 1], b[mid:mid + 1] - b[C - 1:C]))

    spread = lax.fori_loop(0, tm // C, prep, jnp.zeros((1, W), F32))
    single_ref_ok = jnp.max(spread) <= HG_SINGLE_REF_LIMIT

    def heads(x):
        return (jnp.where(head_lo, x, 0.0).astype(BF16), jnp.where(head_lo, 0.0, x).astype(BF16))

    def finish(cs, bs, qs, ks, vs, a_all):
        o_all = []
        for i, c in enumerate(cs):
            b, q, k, v = bs[i], qs[i], ks[i], vs[i]
            b_end = b[C - 1:C, :]
            qe = (q * jnp.exp(b)).astype(BF16)
            ke = (k * jnp.exp(b_end - b)).astype(BF16)
            d_end = jnp.exp(b_end)
            o_pairs = []
            for p in range(n_pairs):
                sl = slice(p * LANE, (p + 1) * LANE)
                v0, v1 = heads(v[:, sl])
                a0, a1 = a_all[i][p]
                st = st_ref[0, p]
                o_p = (_dot(a0.astype(BF16), v0) + _dot(a1.astype(BF16), v1)
                       + _dot_nt(qe[:, sl], st.astype(BF16)))
                upd = _dot(v[:, sl].T.astype(BF16), ke[:, sl])
                st_ref[0, p] = st * d_end[:, sl] + jnp.where(diag, upd, 0.0)
                o_pairs.append(o_p)
            o_all.append(jnp.concatenate(o_pairs, axis=1))
        for c, o in zip(cs, o_all):
            r0 = pl.multiple_of(c * C, C)
            ms = _dot((o * o).astype(BF16), gavg_ref[...])
            ob = o * lax.rsqrt(ms + EPS) * gh_ref[...] * z_s[pl.ds(r0, C), 3 * W:4 * W]
            ob_ref[0, pl.ds(r0, C), :] = ob.astype(BF16)

    def load(cs):
        out = []
        for c in cs:
            r0 = pl.multiple_of(c * C, C)
            out.append((b_s[pl.ds(r0, C), :], z_s[pl.ds(r0, C), 0:W], z_s[pl.ds(r0, C), W:2 * W],
                        z_s[pl.ds(r0, C), 2 * W:3 * W]))
        return tuple(zip(*out))

    group = min(HG_CHUNKS_PER_STEP, tm // C)

    def fast(j, carry):
        cs = [j * group + i for i in range(group)]
        bs, qs, ks, vs = load(cs)
        a_all = []
        for b, q, k in zip(bs, qs, ks):
            ref = b[mid:mid + 1, :]
            qt = q * jnp.exp(b - ref)
            kt = (k * jnp.exp(ref - b)).astype(BF16)
            a_c = []
            for p in range(n_pairs):
                sl = slice(p * LANE, (p + 1) * LANE)
                q0, q1 = heads(qt[:, sl])
                a_c.append((jnp.where(tril_b, _dot_nt(q0, kt[:, sl]), 0.0),
                            jnp.where(tril_b, _dot_nt(q1, kt[:, sl]), 0.0)))
            a_all.append(a_c)
        finish(cs, bs, qs, ks, vs, a_all)
        return carry

    def safe(c, carry):
        bs, qs, ks, vs = load([c])
        a_c = []
        for p in range(n_pairs):
            sl = slice(p * LANE, (p + 1) * LANE)
            bp, qp, kp = bs[0][:, sl], qs[0][:, sl], ks[0][:, sl]
            a_h = [jnp.zeros((C, C), F32), jnp.zeros((C, C), F32)]
            for s in sizes:
                ref = _level_ref_rows(bp, s)
                upper = (rsub & (s - 1)) >= (s // 2)
                qt = heads(qp * jnp.exp(jnp.where(upper, bp - ref, NEG)))
                kt = (kp * jnp.exp(jnp.where(upper, NEG, ref - bp))).astype(BF16)
                for hh in range(2):
                    pr = _dot_nt(qt[hh], kt)
                    a_h[hh] = a_h[hh] + (pr if s == C else jnp.where(blk_masks[s], pr, 0.0))
            ref = _base_ref_rows(bp)
            qt = heads(qp * jnp.exp(bp - ref))
            kt = (kp * jnp.exp(ref - bp)).astype(BF16)
            for hh in range(2):
                a_h[hh] = a_h[hh] + jnp.where(base_mask, _dot_nt(qt[hh], kt), 0.0)
            a_c.append(tuple(a_h))
        finish([c], bs, qs, ks, vs, [a_c])
        return carry

    @pl.when(single_ref_ok)
    def _():
        lax.fori_loop(0, tm // (C * group), fast, 0)

    @pl.when(jnp.logical_not(single_ref_ok))
    def _():
        lax.fori_loop(0, tm // C, safe, 0)


def _hgrn(x, gmix, wh, lbl, st0, gh, gavg, layer, tm):
    B, T, D = x.shape
    W = HG_HEADS * HG_DK
    st_spec = pl.BlockSpec((1, HG_HEADS // 2, LANE, LANE), lambda b, t: (b, 0, 0, 0))
    return pl.pallas_call(
        functools.partial(_hgrn_kernel, layer=layer, tm=tm),
        grid=(B, T // tm),
        in_specs=[pl.BlockSpec((1, tm, D), lambda b, t: (b, t, 0)), _const_spec(gmix.shape),
                  _const_spec(wh.shape), _const_spec(lbl.shape), st_spec, _const_spec(gh.shape),
                  _const_spec(gavg.shape)],
        out_specs=[pl.BlockSpec((1, tm, W), lambda b, t: (b, t, 0)), st_spec],
        out_shape=[jax.ShapeDtypeStruct((B, T, W), BF16),
                   jax.ShapeDtypeStruct((B, HG_HEADS // 2, LANE, LANE), F32)],
        scratch_shapes=[pltpu.VMEM((tm, 4 * W), F32), pltpu.VMEM((tm, W), F32)],
        compiler_params=_cparams(("parallel", "arbitrary")),
        name="hgrn",
    )(x, gmix, wh, lbl, st0, gh, gavg)


def _post_kernel(x_ref, oa_ref, ob_ref, oc_ref, gmix_ref, wg_ref, wb_ref, wo_ref, gffn_ref, wup_ref,
                 cw_ref, cb_ref, wdn_ref, conv0_ref, gfin_ref, y_ref, conv_ref, carry_s, *, tm, d_ff, final):
    ti = pl.program_id(1)
    x = x_ref[0]
    D = x.shape[1]
    h = _rms(x, gmix_ref[...]).astype(BF16)
    mixed = jnp.zeros((tm, D), F32)
    for n, o_ref in enumerate((oa_ref, ob_ref, oc_ref)):
        gate = _sigmoid(_dot(h, wg_ref[:, n * D:(n + 1) * D]))
        mixed = mixed + gate * _dot(o_ref[0], wb_ref[n])
    x1 = x + _dot(mixed.astype(BF16), wo_ref[...])
    xn = _rms(x1, gffn_ref[...]).astype(BF16)

    @pl.when(ti == 0)
    def _():
        carry_s[0:CONV_W - 1, :] = conv0_ref[0]

    acc = jnp.zeros((tm, D), F32)
    for f in range(0, d_ff, FF_BLOCK):
        w = min(FF_BLOCK, d_ff - f)
        cols = slice(f, f + w)
        a = _dot(xn, wup_ref[:, cols])
        vv = _dot(xn, wup_ref[:, d_ff + f:d_ff + f + w])
        p0 = carry_s[0:1, cols]
        p1 = carry_s[1:2, cols]
        r1 = pltpu.roll(a, 1, 0)
        r2 = pltpu.roll(a, 2, 0)
        last2 = r2[0:8, :]
        rows = lax.broadcasted_iota(jnp.int32, (8, w), 0)
        a1 = jnp.concatenate([jnp.where(rows < 1, p1, r1[0:8]), r1[8:]], axis=0) if tm > 8 else \
            jnp.where(rows < 1, p1, r1)
        h2 = jnp.where(rows < 1, p0, jnp.where(rows < 2, p1, last2))
        a2 = jnp.concatenate([h2, r2[8:]], axis=0) if tm > 8 else h2
        cc = (cb_ref[:, cols] + a2 * cw_ref[0:1, cols] + a1 * cw_ref[1:2, cols] + a * cw_ref[2:3, cols])
        act = (cc * _sigmoid(cc) * vv).astype(BF16)
        acc = acc + _dot(act, wdn_ref[cols, :])
        carry_s[:, cols] = last2
        conv_ref[0, :, cols] = last2[0:CONV_W - 1, :]
    x2 = x1 + acc
    y_ref[0] = _rms(x2, gfin_ref[...]) if final else x2


def _post(x, oa, ob, oc, gmix, wg, wb, wo, gffn, wup, cw, cb, wdn, conv0, gfin, tm, final):
    B, T, D = x.shape
    d_ff = wdn.shape[0]
    tok = lambda w: pl.BlockSpec((1, tm, w), lambda b, t: (b, t, 0))
    cspec = pl.BlockSpec((1, CONV_W - 1, d_ff), lambda b, t: (b, 0, 0))
    one = lambda a: pl.BlockSpec(a.shape, lambda *_: (0,) * a.ndim, pipeline_mode=pl.Buffered(1))
    return pl.pallas_call(
        functools.partial(_post_kernel, tm=tm, d_ff=d_ff, final=final),
        grid=(B, T // tm),
        in_specs=[tok(D), tok(BR_WIDTH), tok(BR_WIDTH), tok(BR_WIDTH), one(gmix), one(wg), one(wb), one(wo),
                  one(gffn), one(wup), one(cw), one(cb), one(wdn), cspec, one(gfin)],
        out_specs=[tok(D), cspec],
        out_shape=[jax.ShapeDtypeStruct((B, T, D), F32),
                   jax.ShapeDtypeStruct((B, CONV_W - 1, d_ff), F32)],
        scratch_shapes=[pltpu.VMEM((8, d_ff), F32)],
        compiler_params=_cparams(("parallel", "arbitrary")),
        name="post",
    )(x, oa, ob, oc, gmix, wg, wb, wo, gffn, wup, cw, cb, wdn, conv0, gfin)


def _rot_half_cols(w):
    half = w.shape[-1] // 2
    return jnp.concatenate([-w[..., half:], w[..., :half]], axis=-1)


def _pad_cols(w, width):
    return jnp.pad(w, [(0, 0)] * (w.ndim - 1) + [(0, width - w.shape[-1])])


def _layer_weights(w_in, w_uq, w_ukv):
    D = w_in.shape[0]
    o_q, o_kv, o_kr = 0, MLA_Q_LORA, MLA_Q_LORA + MLA_KV_LORA
    o_h = o_kr + MLA_ROPE
    o_d = o_h + 4 * HG_HEADS * HG_DK
    o_g = o_d + 3 * DF_HEADS * DF_DV
    w_kr = w_in[:, o_kr:o_h]
    wm = jnp.concatenate([w_in[:, o_q:o_kr], _pad_cols(w_kr, LANE), _pad_cols(_rot_half_cols(w_kr), LANE)],
                         axis=1)
    wh = w_in[:, o_h:o_d]
    wd = w_in[:, o_d:o_g]
    wg = w_in[:, o_g:]
    uq = w_uq.reshape(MLA_Q_LORA, MLA_HEADS, MLA_NOPE + MLA_ROPE)
    nope, rope = uq[..., :MLA_NOPE], uq[..., MLA_NOPE:]
    zr = jnp.zeros((MLA_Q_LORA, MLA_HEADS, LANE - MLA_NOPE - MLA_ROPE), w_uq.dtype)
    wq1 = jnp.concatenate([nope, rope, zr], axis=-1).reshape(MLA_Q_LORA, MLA_HEADS * LANE)
    wq2 = jnp.concatenate([jnp.zeros_like(nope), _rot_half_cols(rope), zr], axis=-1)
    wq = jnp.concatenate([wq1, wq2.reshape(MLA_Q_LORA, MLA_HEADS * LANE)], axis=1)
    ukv = w_ukv.reshape(MLA_KV_LORA, MLA_HEADS, MLA_NOPE + MLA_V)
    wk = _pad_cols(ukv[..., :MLA_NOPE], LANE).reshape(MLA_KV_LORA, MLA_HEADS * LANE)
    wvt = ukv[..., MLA_NOPE:].reshape(MLA_KV_LORA, MLA_HEADS * MLA_V).T
    c = lambda a: a.astype(BF16)
    return c(wm), c(wh), c(wd), c(wg), c(wq), c(wk), c(wvt)


def _rope_tables(pos):
    half = MLA_ROPE // 2
    inv = ROPE_THETA ** (-jnp.arange(half, dtype=F32) / half)
    ang = pos.astype(F32)[:, None] * inv[None, :]
    cos2 = jnp.concatenate([jnp.cos(ang)] * 2, axis=1)
    sin2 = jnp.concatenate([jnp.sin(ang)] * 2, axis=1)
    T = pos.shape[0]
    c = MLA_SCALE * LOG2E
    zq = jnp.zeros((T, LANE - MLA_NOPE - MLA_ROPE), F32)
    cq = jnp.concatenate([jnp.full((T, MLA_NOPE), c, F32), c * cos2, zq], axis=1)
    sq = jnp.concatenate([jnp.zeros((T, MLA_NOPE), F32), c * sin2, zq], axis=1)
    return cq, sq, _pad_cols(cos2, LANE), _pad_cols(sin2, LANE)


def _rope_place():
    r = jnp.arange(MLA_ROPE)
    e = jnp.zeros((MLA_ROPE, MLA_HEADS, LANE), F32)
    e = e.at[r, :, MLA_NOPE + r].set(1.0)
    return e.reshape(MLA_ROPE, MLA_HEADS * LANE).astype(BF16)


def _pair_state(s):
    B = s.shape[0]
    st = jnp.swapaxes(s, -1, -2).reshape(B, HG_HEADS // 2, 2, HG_DV, HG_DK)
    z = jnp.zeros_like(st[:, :, 0])
    top = jnp.concatenate([st[:, :, 0], z], axis=-1)
    bot = jnp.concatenate([z, st[:, :, 1]], axis=-1)
    return jnp.concatenate([top, bot], axis=-2)


def _unpair_state(sp):
    a = sp[:, :, :HG_DV, :HG_DK]
    b = sp[:, :, HG_DV:, HG_DK:]
    st = jnp.stack([a, b], axis=2).reshape(sp.shape[0], HG_HEADS, HG_DV, HG_DK)
    return jnp.swapaxes(st, -1, -2)


def _pad_rows(a, n):
    return jnp.pad(a, ((0, 0), (0, n - a.shape[1]), (0, 0)))


def _layer(l, depth, x, pos, past, prev, s0, conv0, lw, prm, final):
    (norm_mix_g, mla_q_norm_g, mla_kv_norm_g, hgrn_lb_logits, hgrn_norm_g, diff_lambda, diff_norm_g,
     w_branch, w_out, norm_ffn_g, ffn_w_up, ffn_conv_w, ffn_conv_b, ffn_w_down, norm_final_g) = prm
    wm, wh, wd, wg, wq, wk, wvt = lw
    B, T, D = x.shape
    row = lambda a: a.reshape(1, -1)
    gmix = row(norm_mix_g)
    tm = min(512, T)
    place = _rope_place()

    if past is None:
        q_off, kvalid = 0, T
        tq = min(KV_BLOCK, T)
        cq, sq, ck, sk = _rope_tables(pos)
        (q_in, k_mla, vt_mla, dq_in, k_d, vt_d), new_rows = _proj(
            x, gmix, wm, row(mla_q_norm_g), wq, row(mla_kv_norm_g), cq, sq, ck, sk, wk, place, wvt, wd,
            l, depth, prev, tm)
    else:
        n_tok = B * T
        xf = x.reshape(1, n_tok, D)
        cq, sq, ck, sk = _rope_tables(jnp.tile(pos, B))
        q, ckv, krot = _mla_proj(xf, gmix, wm, row(mla_q_norm_g), wq, row(mla_kv_norm_g), cq, sq, ck, sk, n_tok)
        dq, dk, dv = _diff_proj(xf, gmix, wd, n_tok)
        q, ckv, krot, dq, dk, dv = (a.reshape(B, T, -1) for a in (q, ckv, krot, dq, dk, dv))
        P = past[0].shape[1]
        q_off, kvalid = P, P + T
        kpad = -(-kvalid // KV_BLOCK) * KV_BLOCK
        cat = lambda a, b: _pad_rows(jnp.concatenate([a.reshape(B, P, -1), b], axis=1), kpad)
        tq = LANE
        q_in, dq_in = _pad_rows(q, tq), _pad_rows(dq, tq)
        k_mla, vt_mla = _mla_kv(cat(past[0], ckv), cat(past[1], krot), wk, place, wvt)
        k_d, vt_d = _diff_kv(cat(past[2], dk), cat(past[3], dv))
        new_rows = (ckv, krot, dk, dv)
    o_a = _attn_mla(q_in, k_mla, vt_mla, tq, q_off, kvalid)[:, :T]
    lam_init = 0.8 - 0.6 * math.exp(-0.3 * l)
    o_c = _attn_diff(dq_in, k_d, vt_d, diff_lambda, row(jnp.tile(diff_norm_g, 2)), tq, q_off, kvalid,
                     lam_init)[:, :T]

    gavg = jnp.kron(jnp.eye(HG_HEADS, dtype=F32), jnp.full((HG_DV, HG_DV), 1.0 / HG_DV, F32)).astype(BF16)
    o_b, st = _hgrn(x, gmix, wh, hgrn_lb_logits, _pair_state(s0), row(jnp.tile(hgrn_norm_g, HG_HEADS)),
                    gavg, l, tm)

    y, conv_new = _post(x, o_a, o_b, o_c, gmix, wg, w_branch.astype(BF16), w_out.astype(BF16),
                        row(norm_ffn_g), ffn_w_up.astype(BF16), ffn_conv_w, row(ffn_conv_b),
                        ffn_w_down.astype(BF16), conv0, row(norm_final_g), tm, final)
    return y, new_rows, (_unpair_state(st), conv_new)


def kernel(x_prompt, x_sample, cache_mla_ckv, cache_mla_krope, cache_diff_k, cache_diff_v, state_hgrn,
           state_ffn_conv, norm_mix_g, w_in, mla_q_norm_g, mla_w_uq, mla_kv_norm_g, mla_w_ukv, hgrn_lb_logits,
           hgrn_norm_g, diff_lambda, diff_norm_g, w_branch, w_out, norm_ffn_g, ffn_w_up, ffn_conv_w,
           ffn_conv_b, ffn_w_down, norm_final_g):
    depth = w_in.shape[0]
    B, T, _ = x_prompt.shape
    Bs, Ts, _ = x_sample.shape
    P = cache_mla_ckv.shape[2]
    d_ff = ffn_w_down.shape[1]
    pos_p = jnp.arange(T)
    pos_s = P + jnp.arange(Ts)
    yp, ys = x_prompt, x_sample
    p_rows = None
    s_rows, p_small, s_small = [], [], []
    for l in range(depth):
        lw = _layer_weights(w_in[l], mla_w_uq[l], mla_w_ukv[l])
        prm = (norm_mix_g[l], mla_q_norm_g[l], mla_kv_norm_g[l], hgrn_lb_logits, hgrn_norm_g[l], diff_lambda[l],
               diff_norm_g[l], w_branch[l], w_out[l], norm_ffn_g[l], ffn_w_up[l], ffn_conv_w[l], ffn_conv_b[l],
               ffn_w_down[l], norm_final_g)
        final = l == depth - 1
        yp, p_rows, small = _layer(l, depth, yp, pos_p, None, p_rows, jnp.zeros((B, HG_HEADS, HG_DK, HG_DV), F32),
                                   jnp.zeros((B, CONV_W - 1, d_ff), F32), lw, prm, final)
        p_small.append(small)
        ys, rows, small = _layer(l, depth, ys, pos_s,
                                 (cache_mla_ckv[l], cache_mla_krope[l], cache_diff_k[l], cache_diff_v[l]), None,
                                 state_hgrn[l], state_ffn_conv[l], lw, prm, final)
        s_rows.append(rows)
        s_small.append(small)
    stk = lambda states, i: jnp.stack([s[i] for s in states], axis=0)
    heads = lambda a, w: a.reshape(a.shape[:3] + (-1, w))
    s_rows = [stk(s_rows, i) for i in range(4)]
    out_rows = lambda r: (r[0], r[1], heads(r[2], 2 * DF_DH), heads(r[3], DF_DV))
    return ((yp, ys) + out_rows(p_rows) + (stk(p_small, 0), stk(p_small, 1))
            + out_rows(s_rows) + (stk(s_small, 0), stk(s_small, 1)))
```

```python
import functools
import math

import jax
import jax.numpy as jnp
from jax import lax
from jax.experimental import pallas as pl
from jax.experimental.pallas import tpu as pltpu

CHUNK = 64
EPS = 1e-6
NEG = -1e30
F_MIN = 1e-12
MLA_HEADS = 8
MLA_NOPE = 64
MLA_ROPE = 32
MLA_V = 64
MLA_Q_LORA = 384
MLA_KV_LORA = 256
ROPE_THETA = 10000.0
MLA_SCALE = (MLA_NOPE + MLA_ROPE) ** -0.5
HG_HEADS = 8
HG_DK = 64
HG_DV = 64
DF_HEADS = 8
DF_DH = 32
DF_DV = 2 * DF_DH
DF_SCALE = DF_DH ** -0.5
N_BRANCH = 3
BR_WIDTH = 512
CONV_W = 3
LOG2E = 1.4426950408889634

LANE = 128
VMEM_LIMIT = 56 * 1024 * 1024
KV_BLOCK = 256
HG_CHUNK = 64
HG_CHUNKS_PER_STEP = 2
HG_SINGLE_REF_LIMIT = 43.0
FF_BLOCK = 1024
MLA_HEADS_PER_STEP = 8
DF_PAIRS_PER_STEP = 2

BF16 = jnp.bfloat16
F32 = jnp.float32


def _cparams(sem):
    return pltpu.CompilerParams(dimension_semantics=sem, vmem_limit_bytes=VMEM_LIMIT)


def _dot(a, b):
    return jnp.dot(a, b, preferred_element_type=F32)


def _dot_nt(a, b):
    return lax.dot_general(a, b, (((1,), (1,)), ((), ())), preferred_element_type=F32)


def _rms(x, g):
    r = lax.rsqrt(jnp.mean(x * x, axis=-1, keepdims=True) + EPS)
    return x * r * g


def _sigmoid(x):
    return 0.5 * jnp.tanh(0.5 * x) + 0.5


def _sigmoid_pair(x):
    e = jnp.exp(-jnp.abs(x))
    r = 1.0 / (1.0 + e)
    er = e * r
    pos = x >= 0.0
    return jnp.where(pos, r, er), jnp.where(pos, er, r)


def _const_spec(shape):
    nd = len(shape)
    return pl.BlockSpec(shape, lambda *_: (0,) * nd)


def _mla_qkv(h, wm_ref, gq_ref, wq_ref, gkv_ref, cq_ref, sq_ref, ck_ref, sk_ref):
    z = _dot(h, wm_ref[...])
    qn = _rms(z[:, :MLA_Q_LORA], gq_ref[...]).astype(BF16)
    q2 = _dot(qn, wq_ref[...])
    nq = MLA_HEADS * LANE
    cq = jnp.concatenate([cq_ref[...]] * MLA_HEADS, axis=1)
    sq = jnp.concatenate([sq_ref[...]] * MLA_HEADS, axis=1)
    q = (q2[:, :nq] * cq + q2[:, nq:] * sq).astype(BF16)
    ckv = _rms(z[:, MLA_Q_LORA:MLA_Q_LORA + MLA_KV_LORA], gkv_ref[...])
    o = MLA_Q_LORA + MLA_KV_LORA
    kr = z[:, o:o + LANE] * ck_ref[...] + z[:, o + LANE:o + 2 * LANE] * sk_ref[...]
    return q, ckv, kr[:, :MLA_ROPE]


def _mla_expand(ckv, krot, wk_ref, e_ref, wvt_ref):
    c = ckv.astype(BF16)
    k = (_dot(c, wk_ref[...]) + _dot(krot.astype(BF16), e_ref[...])).astype(BF16)
    return k, _dot_nt(wvt_ref[...], c).astype(BF16)


def _diff_cols(h, wd_ref):
    z = _dot(h, wd_ref[...])
    n = DF_HEADS * 2 * DF_DH
    return (z[:, :n] * (DF_SCALE * LOG2E)).astype(BF16), z[:, n:2 * n], z[:, 2 * n:]


def _proj_kernel(*refs, tm, aliased):
    (x_ref, gmix_ref, wm_ref, gq_ref, wq_ref, gkv_ref, cq_ref, sq_ref, ck_ref, sk_ref, wk_ref, e_ref, wvt_ref,
     wd_ref) = refs[:14]
    (q_ref, kc_ref, vtm_ref, dq_ref, kd_ref, vtd_ref, ckv_ref, krot_ref, dk_ref, dv_ref) = refs[14 + aliased:]
    h = _rms(x_ref[0], gmix_ref[...]).astype(BF16)
    q, ckv, krot = _mla_qkv(h, wm_ref, gq_ref, wq_ref, gkv_ref, cq_ref, sq_ref, ck_ref, sk_ref)
    k, vt = _mla_expand(ckv, krot, wk_ref, e_ref, wvt_ref)
    dq, dk, dv = _diff_cols(h, wd_ref)
    dvt = dv.T.astype(BF16)
    q_ref[0] = q
    kc_ref[0] = k
    dq_ref[0] = dq
    kd_ref[0] = dk.astype(BF16)
    for i in range(tm // KV_BLOCK):
        vtm_ref[0, i] = vt[:, i * KV_BLOCK:(i + 1) * KV_BLOCK]
        vtd_ref[0, i] = dvt[:, i * KV_BLOCK:(i + 1) * KV_BLOCK]
    ckv_ref[0, 0] = ckv
    krot_ref[0, 0] = krot
    dk_ref[0, 0] = dk
    dv_ref[0, 0] = dv


def _proj(x, gmix, wm, gq, wq, gkv, cq, sq, ck, sk, wk, e, wvt, wd, layer, depth, prev, tm):
    B, T, D = x.shape
    nd = DF_HEADS * 2 * DF_DH
    nk = tm // KV_BLOCK
    tok = lambda w: pl.BlockSpec((1, tm, w), lambda b, t: (b, t, 0))
    tab = pl.BlockSpec((tm, LANE), lambda b, t: (t, 0))
    vts = lambda r: pl.BlockSpec((1, nk, r, KV_BLOCK), lambda b, t: (b, t, 0, 0))
    st = lambda w: pl.BlockSpec((1, 1, tm, w), lambda b, t: (layer, b, t, 0))
    consts = (gmix, wm, gq, wq, gkv)
    consts2 = (wk, e, wvt, wd)
    in_specs = ([tok(D)] + [_const_spec(a.shape) for a in consts] + [tab] * 4
                + [_const_spec(a.shape) for a in consts2])
    args = (x,) + consts + (cq, sq, ck, sk) + consts2
    aliases = {}
    if prev is not None:
        in_specs += [pl.BlockSpec(memory_space=pl.ANY)] * 4
        aliases = {len(args) + i: 6 + i for i in range(4)}
        args += tuple(prev)
    widths = (MLA_KV_LORA, MLA_ROPE, nd, nd)
    outs = pl.pallas_call(
        functools.partial(_proj_kernel, tm=tm, aliased=4 if prev is not None else 0),
        grid=(B, T // tm),
        in_specs=in_specs,
        out_specs=[tok(MLA_HEADS * LANE), tok(MLA_HEADS * LANE), vts(MLA_HEADS * MLA_V), tok(nd), tok(nd), vts(nd)]
        + [st(w) for w in widths],
        out_shape=[jax.ShapeDtypeStruct((B, T, MLA_HEADS * LANE), BF16),
                   jax.ShapeDtypeStruct((B, T, MLA_HEADS * LANE), BF16),
                   jax.ShapeDtypeStruct((B, T // KV_BLOCK, MLA_HEADS * MLA_V, KV_BLOCK), BF16),
                   jax.ShapeDtypeStruct((B, T, nd), BF16),
                   jax.ShapeDtypeStruct((B, T, nd), BF16),
                   jax.ShapeDtypeStruct((B, T // KV_BLOCK, nd, KV_BLOCK), BF16)]
        + [jax.ShapeDtypeStruct((depth, B, T, w), F32) for w in widths],
        input_output_aliases=aliases,
        compiler_params=_cparams(("parallel", "parallel")),
        name="proj",
    )(*args)
    return outs[:6], outs[6:]


def _mla_proj_kernel(x_ref, gmix_ref, wm_ref, gq_ref, wq_ref, gkv_ref, cq_ref, sq_ref, ck_ref, sk_ref,
                     q_ref, ckv_ref, krot_ref):
    h = _rms(x_ref[0], gmix_ref[...]).astype(BF16)
    q_ref[0], ckv_ref[0], krot_ref[0] = _mla_qkv(h, wm_ref, gq_ref, wq_ref, gkv_ref, cq_ref, sq_ref, ck_ref,
                                                  sk_ref)


def _mla_proj(x, gmix, wm, gq, wq, gkv, cq, sq, ck, sk, tm):
    B, T, D = x.shape
    grid = (B, T // tm)
    tok = lambda w: pl.BlockSpec((1, tm, w), lambda b, t: (b, t, 0))
    tab = pl.BlockSpec((tm, LANE), lambda b, t: (t, 0))
    return pl.pallas_call(
        _mla_proj_kernel,
        grid=grid,
        in_specs=[tok(D), _const_spec(gmix.shape), _const_spec(wm.shape), _const_spec(gq.shape),
                  _const_spec(wq.shape), _const_spec(gkv.shape), tab, tab, tab, tab],
        out_specs=[tok(MLA_HEADS * LANE), tok(MLA_KV_LORA), tok(MLA_ROPE)],
        out_shape=[jax.ShapeDtypeStruct((B, T, MLA_HEADS * LANE), BF16),
                   jax.ShapeDtypeStruct((B, T, MLA_KV_LORA), F32),
                   jax.ShapeDtypeStruct((B, T, MLA_ROPE), F32)],
        compiler_params=_cparams(("parallel", "parallel")),
        name="mla_proj",
    )(x, gmix, wm, gq, wq, gkv, cq, sq, ck, sk)


def _past_new_specs(layer, n_past, width):
    return (pl.BlockSpec((1, 1, KV_BLOCK, width), lambda b, t: (layer, b, jnp.minimum(t, n_past - 1), 0)),
            pl.BlockSpec((1, KV_BLOCK, width), lambda b, t: (b, 0, 0)))


def _mla_kv_kernel(pc_ref, nc_ref, pr_ref, nr_ref, wk_ref, e_ref, wvt_ref, k_ref, vt_ref, *, n_past):
    t = pl.program_id(1)

    @pl.when(t < n_past)
    def _():
        k_ref[0], vt_ref[0, 0] = _mla_expand(pc_ref[0, 0], pr_ref[0, 0], wk_ref, e_ref, wvt_ref)

    @pl.when(t >= n_past)
    def _():
        k_ref[0], vt_ref[0, 0] = _mla_expand(nc_ref[0], nr_ref[0], wk_ref, e_ref, wvt_ref)


def _mla_kv(layer, ckv_past, krot_past, ckv_new, krot_new, wk, e, wvt):
    _, B, P, _ = ckv_past.shape
    n_past = P // KV_BLOCK
    nkb = n_past + 1
    return pl.pallas_call(
        functools.partial(_mla_kv_kernel, n_past=n_past),
        grid=(B, nkb),
        in_specs=[*_past_new_specs(layer, n_past, MLA_KV_LORA), *_past_new_specs(layer, n_past, MLA_ROPE),
                  _const_spec(wk.shape), _const_spec(e.shape), _const_spec(wvt.shape)],
        out_specs=[pl.BlockSpec((1, KV_BLOCK, MLA_HEADS * LANE), lambda b, t: (b, t, 0)),
                   pl.BlockSpec((1, 1, MLA_HEADS * MLA_V, KV_BLOCK), lambda b, t: (b, t, 0, 0))],
        out_shape=[jax.ShapeDtypeStruct((B, nkb * KV_BLOCK, MLA_HEADS * LANE), BF16),
                   jax.ShapeDtypeStruct((B, nkb, MLA_HEADS * MLA_V, KV_BLOCK), BF16)],
        compiler_params=_cparams(("parallel", "parallel")),
        name="mla_kv",
    )(ckv_past, ckv_new, krot_past, krot_new, wk, e, wvt)


def _visible_blocks(q0, tq, kvalid, nkb):
    n_full = jnp.minimum((((q0 >> 6) + 1) * CHUNK) // KV_BLOCK, kvalid // KV_BLOCK)
    last = (((q0 + tq - 1) >> 6) + 1) * CHUNK
    n_vis = jnp.minimum((last + KV_BLOCK - 1) // KV_BLOCK, nkb)
    return n_full, n_vis


def _flash_t(streams, k_ref, vt_ref, m_s, acc_s, n_full, n_vis, q0, tq, kvalid):
    ones = jnp.ones((16, KV_BLOCK), BF16)
    n = len(streams)

    def step(kbs, masked):
        ss = []
        for qm, ksl, _ in streams:
            ss.append([_dot_nt(k_ref[0, pl.ds(pl.multiple_of(kb * KV_BLOCK, KV_BLOCK), KV_BLOCK), ksl], qm)
                       for kb in kbs])
        if any(masked):
            qchunk = (q0 + lax.broadcasted_iota(jnp.int32, (KV_BLOCK, tq), 1)) >> 6
            for j, kb in enumerate(kbs):
                if masked[j]:
                    kpos = kb * KV_BLOCK + lax.broadcasted_iota(jnp.int32, (KV_BLOCK, tq), 0)
                    vis = jnp.where(kpos < kvalid, kpos >> 6, jnp.int32(2 ** 30)) <= qchunk
                    for s in ss:
                        s[j] = jnp.where(vis, s[j], NEG)
        ms = []
        for i, s in enumerate(ss):
            m = m_s[i]
            for sj in s:
                m = jnp.maximum(m, jnp.max(sj, axis=0, keepdims=True))
            ms.append(m)
        ps = [[jnp.exp2(sj - m_new).astype(BF16) for sj in s] for s, m_new in zip(ss, ms)]
        for i, ((_, _, vsl), m_new, p) in enumerate(zip(streams, ms, ps)):
            acc = jnp.exp2(m_s[i] - m_new) * acc_s[i]
            for kb, pj in zip(kbs, p):
                vt = jnp.concatenate([vt_ref[0, kb, vsl, :], ones], axis=0)
                acc = acc + _dot(vt, pj)
            acc_s[i] = acc
            m_s[i] = m_new

    for i in range(n):
        m_s[i] = jnp.full((1, tq), -jnp.inf, F32)
        acc_s[i] = jnp.zeros((MLA_V + 16, tq), F32)
    n_pairs = n_full // 2

    def pair_step(j, c):
        step([2 * j, 2 * j + 1], (False, False))
        return c

    def masked_step(kb, c):
        step([kb], (True,))
        return c

    lax.fori_loop(0, n_pairs, pair_step, 0)
    mixed = jnp.logical_and(n_full - 2 * n_pairs == 1, n_vis > n_full)

    @pl.when(mixed)
    def _():
        step([n_full - 1, n_full], (False, True))

    lax.fori_loop(jnp.where(mixed, n_full + 1, 2 * n_pairs), n_vis, masked_step, 0)
    return [acc_s[i] for i in range(n)]


def _attn_mla_kernel(q_ref, k_ref, vt_ref, o_ref, m_s, acc_s, *, tq, q_off, kvalid, nkb):
    q0 = q_off + pl.program_id(2) * tq
    n_full, n_vis = _visible_blocks(q0, tq, kvalid, nkb)
    streams = []
    for hh in range(MLA_HEADS_PER_STEP):
        lanes = slice(hh * LANE, (hh + 1) * LANE)
        streams.append((q_ref[0, :, lanes], lanes, slice(hh * MLA_V, (hh + 1) * MLA_V)))
    accs = _flash_t(streams, k_ref, vt_ref, m_s, acc_s, n_full, n_vis, q0, tq, kvalid)
    outs = [acc[:MLA_V] * (1.0 / acc[MLA_V:MLA_V + 1]) for acc in accs]
    for i in range(0, MLA_HEADS_PER_STEP, 2):
        o_ref[0, :, i * MLA_V:(i + 2) * MLA_V] = jnp.concatenate(outs[i:i + 2], axis=0).T.astype(BF16)


def _attn_mla(q, k, vt, tq, q_off, kvalid):
    B, T, _ = q.shape
    Tk = k.shape[1]
    nkb = Tk // KV_BLOCK
    hps = MLA_HEADS_PER_STEP
    kern = functools.partial(_attn_mla_kernel, tq=tq, q_off=q_off, kvalid=kvalid, nkb=nkb)
    return pl.pallas_call(
        kern,
        grid=(B, MLA_HEADS // hps, T // tq),
        in_specs=[pl.BlockSpec((1, tq, hps * LANE), lambda b, p, i: (b, i, p)),
                  pl.BlockSpec((1, Tk, hps * LANE), lambda b, p, i: (b, 0, p)),
                  pl.BlockSpec((1, nkb, hps * MLA_V, KV_BLOCK), lambda b, p, i: (b, 0, p, 0))],
        out_specs=pl.BlockSpec((1, tq, hps * MLA_V), lambda b, p, i: (b, i, p)),
        out_shape=jax.ShapeDtypeStruct((B, T, MLA_HEADS * MLA_V), BF16),
        scratch_shapes=[pltpu.VMEM((hps, 1, tq), F32), pltpu.VMEM((hps, MLA_V + 16, tq), F32)],
        compiler_params=_cparams(("parallel", "parallel", "arbitrary")),
        name="attn_mla",
    )(q, k, vt)


def _attn_diff_kernel(q_ref, k_ref, vt_ref, lam_ref, g_ref, o_ref, m_s, acc_s, *, tq, q_off, kvalid, nkb,
                      lam_init):
    q0 = q_off + pl.program_id(2) * tq
    n_full, n_vis = _visible_blocks(q0, tq, kvalid, nkb)
    lamv = lam_ref[...]
    lam = (jnp.exp(jnp.sum(lamv[0:1] * lamv[1:2], axis=-1, keepdims=True))
           - jnp.exp(jnp.sum(lamv[2:3] * lamv[3:4], axis=-1, keepdims=True)) + lam_init)
    lane = lax.broadcasted_iota(jnp.int32, (1, LANE), 1)
    streams = []
    for pr in range(DF_PAIRS_PER_STEP):
        lanes = slice(pr * LANE, (pr + 1) * LANE)
        q = q_ref[0, :, lanes]
        for hh in range(2):
            for j in range(2):
                lo = hh * 2 * DF_DH + j * DF_DH
                qm = q * jnp.where((lane >= lo) & (lane < lo + DF_DH), 1.0, 0.0).astype(BF16)
                streams.append((qm, lanes, slice((2 * pr + hh) * DF_DV, (2 * pr + hh + 1) * DF_DV)))
    accs = _flash_t(streams, k_ref, vt_ref, m_s, acc_s, n_full, n_vis, q0, tq, kvalid)
    maps = [acc[:DF_DV] * (1.0 / acc[DF_DV:DF_DV + 1]) for acc in accs]
    for pr in range(DF_PAIRS_PER_STEP):
        outs = []
        for hh in range(2):
            i = 4 * pr + 2 * hh
            o = maps[i] - lam * maps[i + 1]
            outs.append(o * lax.rsqrt(jnp.mean(o * o, axis=0, keepdims=True) + EPS))
        ot = jnp.concatenate(outs, axis=0).T
        o_ref[0, :, pr * LANE:(pr + 1) * LANE] = (ot * g_ref[...] * (1.0 - lam_init)).astype(BF16)


def _attn_diff(q, k, vt, lam_rows, g2, tq, q_off, kvalid, lam_init):
    B, T, _ = q.shape
    Tk = k.shape[1]
    nkb = Tk // KV_BLOCK
    pps = DF_PAIRS_PER_STEP
    kern = functools.partial(_attn_diff_kernel, tq=tq, q_off=q_off, kvalid=kvalid, nkb=nkb,
                             lam_init=lam_init)
    return pl.pallas_call(
        kern,
        grid=(B, DF_HEADS // (2 * pps), T // tq),
        in_specs=[pl.BlockSpec((1, tq, pps * LANE), lambda b, p, i: (b, i, p)),
                  pl.BlockSpec((1, Tk, pps * LANE), lambda b, p, i: (b, 0, p)),
                  pl.BlockSpec((1, nkb, pps * 2 * DF_DV, KV_BLOCK), lambda b, p, i: (b, 0, p, 0)),
                  _const_spec(lam_rows.shape), _const_spec(g2.shape)],
        out_specs=pl.BlockSpec((1, tq, pps * 2 * DF_DV), lambda b, p, i: (b, i, p)),
        out_shape=jax.ShapeDtypeStruct((B, T, DF_HEADS * DF_DV), BF16),
        scratch_shapes=[pltpu.VMEM((4 * pps, 1, tq), F32), pltpu.VMEM((4 * pps, DF_DV + 16, tq), F32)],
        compiler_params=_cparams(("parallel", "parallel", "arbitrary")),
        name="attn_diff",
    )(q, k, vt, lam_rows, g2)


def _diff_proj_kernel(x_ref, gmix_ref, wd_ref, q_ref, dk_ref, dv_ref):
    h = _rms(x_ref[0], gmix_ref[...]).astype(BF16)
    q_ref[0], dk_ref[0], dv_ref[0] = _diff_cols(h, wd_ref)


def _diff_proj(x, gmix, wd, tm):
    B, T, D = x.shape
    n = DF_HEADS * 2 * DF_DH
    tok = lambda w: pl.BlockSpec((1, tm, w), lambda b, t: (b, t, 0))
    return pl.pallas_call(
        _diff_proj_kernel,
        grid=(B, T // tm),
        in_specs=[tok(D), _const_spec(gmix.shape), _const_spec(wd.shape)],
        out_specs=[tok(n), tok(n), tok(n)],
        out_shape=[jax.ShapeDtypeStruct((B, T, n), BF16),
                   jax.ShapeDtypeStruct((B, T, n), F32),
                   jax.ShapeDtypeStruct((B, T, n), F32)],
        compiler_params=_cparams(("parallel", "parallel")),
        name="diff_proj",
    )(x, gmix, wd)


def _diff_kv_kernel(pk_ref, nk_ref, pv_ref, nv_ref, k_ref, vt_ref, *, n_past):
    t = pl.program_id(1)

    @pl.when(t < n_past)
    def _():
        k_ref[0] = pk_ref[0, 0].astype(BF16)
        vt_ref[0, 0] = pv_ref[0, 0].T.astype(BF16)

    @pl.when(t >= n_past)
    def _():
        k_ref[0] = nk_ref[0].astype(BF16)
        vt_ref[0, 0] = nv_ref[0].T.astype(BF16)


def _diff_kv(layer, dk_past, dv_past, dk_new, dv_new):
    _, B, P, n = dk_past.shape
    n_past = P // KV_BLOCK
    nkb = n_past + 1
    return pl.pallas_call(
        functools.partial(_diff_kv_kernel, n_past=n_past),
        grid=(B, nkb),
        in_specs=[*_past_new_specs(layer, n_past, n), *_past_new_specs(layer, n_past, n)],
        out_specs=[pl.BlockSpec((1, KV_BLOCK, n), lambda b, t: (b, t, 0)),
                   pl.BlockSpec((1, 1, n, KV_BLOCK), lambda b, t: (b, t, 0, 0))],
        out_shape=[jax.ShapeDtypeStruct((B, nkb * KV_BLOCK, n), BF16),
                   jax.ShapeDtypeStruct((B, nkb, n, KV_BLOCK), BF16)],
        compiler_params=_cparams(("parallel", "parallel")),
        name="diff_kv",
    )(dk_past, dk_new, dv_past, dv_new)


def _split2(x):
    a = x.astype(BF16)
    return a, (x - a.astype(F32)).astype(BF16)


def _level_ref_rows(b, size):
    half = size // 2
    C = b.shape[0]
    pieces = [jnp.broadcast_to(b[i * size + half - 1:i * size + half, :], (size, b.shape[1]))
              for i in range(C // size)]
    return pieces[0] if len(pieces) == 1 else jnp.concatenate(pieces, axis=0)


def _base_ref_rows(b):
    C, W = b.shape
    sub = lax.broadcasted_iota(jnp.int32, (8, W), 0)
    pieces = [jnp.where(sub < 4, jnp.broadcast_to(b[8 * i:8 * i + 1, :], (8, W)),
                        jnp.broadcast_to(b[8 * i + 4:8 * i + 5, :], (8, W))) for i in range(C // 8)]
    return jnp.concatenate(pieces, axis=0)


def _hgrn_kernel(x_ref, gmix_ref, wh_ref, lbl_ref, st0_ref, gh_ref, gavg_ref, ob_ref, st_ref, z_s, b_s,
                 *, layer, tm):
    C = min(HG_CHUNK, tm)
    W = HG_HEADS * HG_DK
    ti = pl.program_id(1)

    @pl.when(ti == 0)
    def _():
        st_ref[0] = st0_ref[0]

    h = _rms(x_ref[0], gmix_ref[...]).astype(BF16)
    z_s[...] = _dot(h, wh_ref[...])

    lg = lbl_ref[...]
    e = jnp.exp(lg - jnp.max(lg, axis=0, keepdims=True))
    sm = e / jnp.sum(e, axis=0, keepdims=True)
    lb = jnp.zeros((1, W), F32)
    for i in range(1, layer + 1):
        lb = lb + sm[i:i + 1]

    row = lax.broadcasted_iota(jnp.int32, (C, C), 0)
    col = lax.broadcasted_iota(jnp.int32, (C, C), 1)
    tril_b = col <= row
    tril = tril_b.astype(BF16)
    rsub = lax.broadcasted_iota(jnp.int32, (C, LANE), 0)
    lane = lax.broadcasted_iota(jnp.int32, (C, LANE), 1)
    head_lo = lane < HG_DK
    sizes = [s for s in (64, 32, 16, 8) if s <= C]
    blk_masks = {s: (row // s) == (col // s) for s in sizes}
    base_mask = ((row // 4) == (col // 4)) & tril_b
    diag = ((lax.broadcasted_iota(jnp.int32, (LANE, LANE), 0) // HG_DK)
            == (lax.broadcasted_iota(jnp.int32, (LANE, LANE), 1) // HG_DK))
    mid = C // 2 - 1
    n_pairs = HG_HEADS // 2

    def prep(c, spread):
        r0 = pl.multiple_of(c * C, C)
        hq = z_s[pl.ds(r0, C), 0:W]
        hf = z_s[pl.ds(r0, C), W:2 * W]
        hg = z_s[pl.ds(r0, C), 3 * W:4 * W]
        sig, sig_neg = _sigmoid_pair(hf)
        f = sig + lb * sig_neg
        l1, l2 = _split2(jnp.log(jnp.maximum(f, F_MIN)))
        b = _dot(tril, l1) + _dot(tril, l2)
        z_s[pl.ds(r0, C), 0:W] = hq * _sigmoid(hq)
        z_s[pl.ds(r0, C), W:2 * W] = (1.0 - lb) * sig_neg
        z_s[pl.ds(r0, C), 3 * W:4 * W] = hg * _sigmoid(hg)
        b_s[pl.ds(r0, C), :] = b
        return jnp.maximum(spread, jnp.maximum(b[0:1] - b[mid:mid + 1], b[mid:mid + 1] - b[C - 1:C]))

    spread = lax.fori_loop(0, tm // C, prep, jnp.zeros((1, W), F32))
    single_ref_ok = jnp.max(spread) <= HG_SINGLE_REF_LIMIT

    def heads(x):
        return (jnp.where(head_lo, x, 0.0).astype(BF16), jnp.where(head_lo, 0.0, x).astype(BF16))

    def finish(cs, bs, qs, ks, vs, a_all):
        o_all = []
        for i, c in enumerate(cs):
            b, q, k, v = bs[i], qs[i], ks[i], vs[i]
            b_end = b[C - 1:C, :]
            qe = (q * jnp.exp(b)).astype(BF16)
            ke = (k * jnp.exp(b_end - b)).astype(BF16)
            d_end = jnp.exp(b_end)
            o_pairs = []
            for p in range(n_pairs):
                sl = slice(p * LANE, (p + 1) * LANE)
                v0, v1 = heads(v[:, sl])
                a0, a1 = a_all[i][p]
                st = st_ref[0, p]
                o_p = (_dot(a0.astype(BF16), v0) + _dot(a1.astype(BF16), v1)
                       + _dot_nt(qe[:, sl], st.astype(BF16)))
                upd = _dot(v[:, sl].T.astype(BF16), ke[:, sl])
                st_ref[0, p] = st * d_end[:, sl] + jnp.where(diag, upd, 0.0)
                o_pairs.append(o_p)
            o_all.append(jnp.concatenate(o_pairs, axis=1))
        for c, o in zip(cs, o_all):
            r0 = pl.multiple_of(c * C, C)
            ms = _dot((o * o).astype(BF16), gavg_ref[...])
            ob = o * lax.rsqrt(ms + EPS) * gh_ref[...] * z_s[pl.ds(r0, C), 3 * W:4 * W]
            ob_ref[0, pl.ds(r0, C), :] = ob.astype(BF16)

    def load(cs):
        out = []
        for c in cs:
            r0 = pl.multiple_of(c * C, C)
            out.append((b_s[pl.ds(r0, C), :], z_s[pl.ds(r0, C), 0:W], z_s[pl.ds(r0, C), W:2 * W],
                        z_s[pl.ds(r0, C), 2 * W:3 * W]))
        return tuple(zip(*out))

    group = min(HG_CHUNKS_PER_STEP, tm // C)

    def fast(j, carry):
        cs = [j * group + i for i in range(group)]
        bs, qs, ks, vs = load(cs)
        a_all = []
        for b, q, k in zip(bs, qs, ks):
            ref = b[mid:mid + 1, :]
            qt = q * jnp.exp(b - ref)
            kt = (k * jnp.exp(ref - b)).astype(BF16)
            a_c = []
            for p in range(n_pairs):
                sl = slice(p * LANE, (p + 1) * LANE)
                q0, q1 = heads(qt[:, sl])
                a_c.append((jnp.where(tril_b, _dot_nt(q0, kt[:, sl]), 0.0),
                            jnp.where(tril_b, _dot_nt(q1, kt[:, sl]), 0.0)))
            a_all.append(a_c)
        finish(cs, bs, qs, ks, vs, a_all)
        return carry

    def safe(c, carry):
        bs, qs, ks, vs = load([c])
        a_c = []
        for p in range(n_pairs):
            sl = slice(p * LANE, (p + 1) * LANE)
            bp, qp, kp = bs[0][:, sl], qs[0][:, sl], ks[0][:, sl]
            a_h = [jnp.zeros((C, C), F32), jnp.zeros((C, C), F32)]
            for s in sizes:
                ref = _level_ref_rows(bp, s)
                upper = (rsub & (s - 1)) >= (s // 2)
                qt = heads(qp * jnp.exp(jnp.where(upper, bp - ref, NEG)))
                kt = (kp * jnp.exp(jnp.where(upper, NEG, ref - bp))).astype(BF16)
                for hh in range(2):
                    pr = _dot_nt(qt[hh], kt)
                    a_h[hh] = a_h[hh] + (pr if s == C else jnp.where(blk_masks[s], pr, 0.0))
            ref = _base_ref_rows(bp)
            qt = heads(qp * jnp.exp(bp - ref))
            kt = (kp * jnp.exp(ref - bp)).astype(BF16)
            for hh in range(2):
                a_h[hh] = a_h[hh] + jnp.where(base_mask, _dot_nt(qt[hh], kt), 0.0)
            a_c.append(tuple(a_h))
        finish([c], bs, qs, ks, vs, [a_c])
        return carry

    @pl.when(single_ref_ok)
    def _():
        lax.fori_loop(0, tm // (C * group), fast, 0)

    @pl.when(jnp.logical_not(single_ref_ok))
    def _():
        lax.fori_loop(0, tm // C, safe, 0)


def _hgrn(x, gmix, wh, lbl, st0, gh, gavg, layer, tm):
    B, T, D = x.shape
    W = HG_HEADS * HG_DK
    st_spec = pl.BlockSpec((1, HG_HEADS // 2, LANE, LANE), lambda b, t: (b, 0, 0, 0))
    return pl.pallas_call(
        functools.partial(_hgrn_kernel, layer=layer, tm=tm),
        grid=(B, T // tm),
        in_specs=[pl.BlockSpec((1, tm, D), lambda b, t: (b, t, 0)), _const_spec(gmix.shape),
                  _const_spec(wh.shape), _const_spec(lbl.shape), st_spec, _const_spec(gh.shape),
                  _const_spec(gavg.shape)],
        out_specs=[pl.BlockSpec((1, tm, W), lambda b, t: (b, t, 0)), st_spec],
        out_shape=[jax.ShapeDtypeStruct((B, T, W), BF16),
                   jax.ShapeDtypeStruct((B, HG_HEADS // 2, LANE, LANE), F32)],
        scratch_shapes=[pltpu.VMEM((tm, 4 * W), F32), pltpu.VMEM((tm, W), F32)],
        compiler_params=_cparams(("parallel", "arbitrary")),
        name="hgrn",
    )(x, gmix, wh, lbl, st0, gh, gavg)


def _post_kernel(x_ref, oa_ref, ob_ref, oc_ref, gmix_ref, wg_ref, wb_ref, wo_ref, gffn_ref, wup_ref,
                 cw_ref, cb_ref, wdn_ref, conv0_ref, gfin_ref, y_ref, conv_ref, carry_s, *, tm, d_ff, final):
    ti = pl.program_id(1)
    x = x_ref[0]
    D = x.shape[1]
    h = _rms(x, gmix_ref[...]).astype(BF16)
    mixed = jnp.zeros((tm, D), F32)
    for n, o_ref in enumerate((oa_ref, ob_ref, oc_ref)):
        gate = _sigmoid(_dot(h, wg_ref[:, n * D:(n + 1) * D]))
        mixed = mixed + gate * _dot(o_ref[0], wb_ref[n])
    x1 = x + _dot(mixed.astype(BF16), wo_ref[...])
    xn = _rms(x1, gffn_ref[...]).astype(BF16)

    @pl.when(ti == 0)
    def _():
        carry_s[0:CONV_W - 1, :] = conv0_ref[0]

    acc = jnp.zeros((tm, D), F32)
    for f in range(0, d_ff, FF_BLOCK):
        w = min(FF_BLOCK, d_ff - f)
        cols = slice(f, f + w)
        a = _dot(xn, wup_ref[:, cols])
        vv = _dot(xn, wup_ref[:, d_ff + f:d_ff + f + w])
        p0 = carry_s[0:1, cols]
        p1 = carry_s[1:2, cols]
        r1 = pltpu.roll(a, 1, 0)
        r2 = pltpu.roll(a, 2, 0)
        last2 = r2[0:8, :]
        rows = lax.broadcasted_iota(jnp.int32, (8, w), 0)
        a1 = jnp.concatenate([jnp.where(rows < 1, p1, r1[0:8]), r1[8:]], axis=0) if tm > 8 else \
            jnp.where(rows < 1, p1, r1)
        h2 = jnp.where(rows < 1, p0, jnp.where(rows < 2, p1, last2))
        a2 = jnp.concatenate([h2, r2[8:]], axis=0) if tm > 8 else h2
        cc = (cb_ref[:, cols] + a2 * cw_ref[0:1, cols] + a1 * cw_ref[1:2, cols] + a * cw_ref[2:3, cols])
        act = (cc * _sigmoid(cc) * vv).astype(BF16)
        acc = acc + _dot(act, wdn_ref[cols, :])
        carry_s[:, cols] = last2
        conv_ref[0, :, cols] = last2[0:CONV_W - 1, :]
    x2 = x1 + acc
    y_ref[0] = _rms(x2, gfin_ref[...]) if final else x2


def _post(x, oa, ob, oc, gmix, wg, wb, wo, gffn, wup, cw, cb, wdn, conv0, gfin, tm, final):
    B, T, D = x.shape
    d_ff = wdn.shape[0]
    tok = lambda w: pl.BlockSpec((1, tm, w), lambda b, t: (b, t, 0))
    cspec = pl.BlockSpec((1, CONV_W - 1, d_ff), lambda b, t: (b, 0, 0))
    one = lambda a: pl.BlockSpec(a.shape, lambda *_: (0,) * a.ndim, pipeline_mode=pl.Buffered(1))
    return pl.pallas_call(
        functools.partial(_post_kernel, tm=tm, d_ff=d_ff, final=final),
        grid=(B, T // tm),
        in_specs=[tok(D), tok(BR_WIDTH), tok(BR_WIDTH), tok(BR_WIDTH), one(gmix), one(wg), one(wb), one(wo),
                  one(gffn), one(wup), one(cw), one(cb), one(wdn), cspec, one(gfin)],
        out_specs=[tok(D), cspec],
        out_shape=[jax.ShapeDtypeStruct((B, T, D), F32),
                   jax.ShapeDtypeStruct((B, CONV_W - 1, d_ff), F32)],
        scratch_shapes=[pltpu.VMEM((8, d_ff), F32)],
        compiler_params=_cparams(("parallel", "arbitrary")),
        name="post",
    )(x, oa, ob, oc, gmix, wg, wb, wo, gffn, wup, cw, cb, wdn, conv0, gfin)


def _rot_half_cols(w):
    half = w.shape[-1] // 2
    return jnp.concatenate([-w[..., half:], w[..., :half]], axis=-1)


def _pad_cols(w, width):
    return jnp.pad(w, [(0, 0)] * (w.ndim - 1) + [(0, width - w.shape[-1])])


def _layer_weights(w_in, w_uq, w_ukv):
    D = w_in.shape[0]
    o_q, o_kv, o_kr = 0, MLA_Q_LORA, MLA_Q_LORA + MLA_KV_LORA
    o_h = o_kr + MLA_ROPE
    o_d = o_h + 4 * HG_HEADS * HG_DK
    o_g = o_d + 3 * DF_HEADS * DF_DV
    w_kr = w_in[:, o_kr:o_h]
    wm = jnp.concatenate([w_in[:, o_q:o_kr], _pad_cols(w_kr, LANE), _pad_cols(_rot_half_cols(w_kr), LANE)],
                         axis=1)
    wh = w_in[:, o_h:o_d]
    wd = w_in[:, o_d:o_g]
    wg = w_in[:, o_g:]
    uq = w_uq.reshape(MLA_Q_LORA, MLA_HEADS, MLA_NOPE + MLA_ROPE)
    nope, rope = uq[..., :MLA_NOPE], uq[..., MLA_NOPE:]
    zr = jnp.zeros((MLA_Q_LORA, MLA_HEADS, LANE - MLA_NOPE - MLA_ROPE), w_uq.dtype)
    wq1 = jnp.concatenate([nope, rope, zr], axis=-1).reshape(MLA_Q_LORA, MLA_HEADS * LANE)
    wq2 = jnp.concatenate([jnp.zeros_like(nope), _rot_half_cols(rope), zr], axis=-1)
    wq = jnp.concatenate([wq1, wq2.reshape(MLA_Q_LORA, MLA_HEADS * LANE)], axis=1)
    ukv = w_ukv.reshape(MLA_KV_LORA, MLA_HEADS, MLA_NOPE + MLA_V)
    wk = _pad_cols(ukv[..., :MLA_NOPE], LANE).reshape(MLA_KV_LORA, MLA_HEADS * LANE)
    wvt = ukv[..., MLA_NOPE:].reshape(MLA_KV_LORA, MLA_HEADS * MLA_V).T
    c = lambda a: a.astype(BF16)
    return c(wm), c(wh), c(wd), c(wg), c(wq), c(wk), c(wvt)


def _rope_tables(pos):
    half = MLA_ROPE // 2
    inv = ROPE_THETA ** (-jnp.arange(half, dtype=F32) / half)
    ang = pos.astype(F32)[:, None] * inv[None, :]
    cos2 = jnp.concatenate([jnp.cos(ang)] * 2, axis=1)
    sin2 = jnp.concatenate([jnp.sin(ang)] * 2, axis=1)
    T = pos.shape[0]
    c = MLA_SCALE * LOG2E
    zq = jnp.zeros((T, LANE - MLA_NOPE - MLA_ROPE), F32)
    cq = jnp.concatenate([jnp.full((T, MLA_NOPE), c, F32), c * cos2, zq], axis=1)
    sq = jnp.concatenate([jnp.zeros((T, MLA_NOPE), F32), c * sin2, zq], axis=1)
    return cq, sq, _pad_cols(cos2, LANE), _pad_cols(sin2, LANE)


def _rope_place():
    r = jnp.arange(MLA_ROPE)
    e = jnp.zeros((MLA_ROPE, MLA_HEADS, LANE), F32)
    e = e.at[r, :, MLA_NOPE + r].set(1.0)
    return e.reshape(MLA_ROPE, MLA_HEADS * LANE).astype(BF16)


def _pair_state(s):
    B = s.shape[0]
    st = jnp.swapaxes(s, -1, -2).reshape(B, HG_HEADS // 2, 2, HG_DV, HG_DK)
    z = jnp.zeros_like(st[:, :, 0])
    top = jnp.concatenate([st[:, :, 0], z], axis=-1)
    bot = jnp.concatenate([z, st[:, :, 1]], axis=-1)
    return jnp.concatenate([top, bot], axis=-2)


def _unpair_state(sp):
    a = sp[:, :, :HG_DV, :HG_DK]
    b = sp[:, :, HG_DV:, HG_DK:]
    st = jnp.stack([a, b], axis=2).reshape(sp.shape[0], HG_HEADS, HG_DV, HG_DK)
    return jnp.swapaxes(st, -1, -2)


def _pad_rows(a, n):
    return jnp.pad(a, ((0, 0), (0, n - a.shape[1]), (0, 0)))


def _layer(l, depth, x, pos, past, prev, s0, conv0, lw, prm, final):
    (norm_mix_g, mla_q_norm_g, mla_kv_norm_g, hgrn_lb_logits, hgrn_norm_g, diff_lambda, diff_norm_g,
     w_branch, w_out, norm_ffn_g, ffn_w_up, ffn_conv_w, ffn_conv_b, ffn_w_down, norm_final_g) = prm
    wm, wh, wd, wg, wq, wk, wvt = lw
    B, T, D = x.shape
    row = lambda a: a.reshape(1, -1)
    gmix = row(norm_mix_g)
    tm = min(512, T)
    place = _rope_place()

    if past is None:
        q_off, kvalid = 0, T
        tq = min(KV_BLOCK, T)
        cq, sq, ck, sk = _rope_tables(pos)
        (q_in, k_mla, vt_mla, dq_in, k_d, vt_d), new_rows = _proj(
            x, gmix, wm, row(mla_q_norm_g), wq, row(mla_kv_norm_g), cq, sq, ck, sk, wk, place, wvt, wd,
            l, depth, prev, tm)
    else:
        n_tok = B * T
        xf = x.reshape(1, n_tok, D)
        cq, sq, ck, sk = _rope_tables(jnp.tile(pos, B))
        q, ckv, krot = _mla_proj(xf, gmix, wm, row(mla_q_norm_g), wq, row(mla_kv_norm_g), cq, sq, ck, sk, n_tok)
        dq, dk, dv = _diff_proj(xf, gmix, wd, n_tok)
        q, ckv, krot, dq, dk, dv = (a.reshape(B, T, -1) for a in (q, ckv, krot, dq, dk, dv))
        P = past[0].shape[2]
        assert P % KV_BLOCK == 0 and T <= KV_BLOCK, (P, T)
        q_off, kvalid = P, P + T
        tq = LANE
        q_in, dq_in = _pad_rows(q, tq), _pad_rows(dq, tq)
        flat = lambda a: a.reshape(a.shape[:3] + (-1,))
        blk = lambda a: _pad_rows(a, KV_BLOCK)
        k_mla, vt_mla = _mla_kv(l, past[0], past[1], blk(ckv), blk(krot), wk, place, wvt)
        k_d, vt_d = _diff_kv(l, flat(past[2]), flat(past[3]), blk(dk), blk(dv))
        new_rows = (ckv, krot, dk, dv)
    o_a = _attn_mla(q_in, k_mla, vt_mla, tq, q_off, kvalid)[:, :T]
    lam_init = 0.8 - 0.6 * math.exp(-0.3 * l)
    o_c = _attn_diff(dq_in, k_d, vt_d, diff_lambda, row(jnp.tile(diff_norm_g, 2)), tq, q_off, kvalid,
                     lam_init)[:, :T]

    gavg = jnp.kron(jnp.eye(HG_HEADS, dtype=F32), jnp.full((HG_DV, HG_DV), 1.0 / HG_DV, F32)).astype(BF16)
    o_b, st = _hgrn(x, gmix, wh, hgrn_lb_logits, _pair_state(s0), row(jnp.tile(hgrn_norm_g, HG_HEADS)),
                    gavg, l, tm)

    y, conv_new = _post(x, o_a, o_b, o_c, gmix, wg, w_branch.astype(BF16), w_out.astype(BF16),
                        row(norm_ffn_g), ffn_w_up.astype(BF16), ffn_conv_w, row(ffn_conv_b),
                        ffn_w_down.astype(BF16), conv0, row(norm_final_g), tm, final)
    return y, new_rows, (_unpair_state(st), conv_new)


def kernel(x_prompt, x_sample, cache_mla_ckv, cache_mla_krope, cache_diff_k, cache_diff_v, state_hgrn,
           state_ffn_conv, norm_mix_g, w_in, mla_q_norm_g, mla_w_uq, mla_kv_norm_g, mla_w_ukv, hgrn_lb_logits,
           hgrn_norm_g, diff_lambda, diff_norm_g, w_branch, w_out, norm_ffn_g, ffn_w_up, ffn_conv_w,
           ffn_conv_b, ffn_w_down, norm_final_g):
    depth = w_in.shape[0]
    B, T, _ = x_prompt.shape
    Bs, Ts, _ = x_sample.shape
    P = cache_mla_ckv.shape[2]
    d_ff = ffn_w_down.shape[1]
    pos_p = jnp.arange(T)
    pos_s = P + jnp.arange(Ts)
    yp, ys = x_prompt, x_sample
    p_rows = None
    s_rows, p_small, s_small = [], [], []
    for l in range(depth):
        lw = _layer_weights(w_in[l], mla_w_uq[l], mla_w_ukv[l])
        prm = (norm_mix_g[l], mla_q_norm_g[l], mla_kv_norm_g[l], hgrn_lb_logits, hgrn_norm_g[l], diff_lambda[l],
               diff_norm_g[l], w_branch[l], w_out[l], norm_ffn_g[l], ffn_w_up[l], ffn_conv_w[l], ffn_conv_b[l],
               ffn_w_down[l], norm_final_g)
        final = l == depth - 1
        yp, p_rows, small = _layer(l, depth, yp, pos_p, None, p_rows, jnp.zeros((B, HG_HEADS, HG_DK, HG_DV), F32),
                                   jnp.zeros((B, CONV_W - 1, d_ff), F32), lw, prm, final)
        p_small.append(small)
        ys, rows, small = _layer(l, depth, ys, pos_s,
                                 (cache_mla_ckv, cache_mla_krope, cache_diff_k, cache_diff_v), None,
                                 state_hgrn[l], state_ffn_conv[l], lw, prm, final)
        s_rows.append(rows)
        s_small.append(small)
    stk = lambda states, i: jnp.stack([s[i] for s in states], axis=0)
    heads = lambda a, w: a.reshape(a.shape[:3] + (-1, w))
    s_rows = [stk(s_rows, i) for i in range(4)]
    out_rows = lambda r: (r[0], r[1], heads(r[2], 2 * DF_DH), heads(r[3], DF_DV))
    return ((yp, ys) + out_rows(p_rows) + (stk(p_small, 0), stk(p_small, 1))
            + out_rows(s_rows) + (stk(s_small, 0), stk(s_small, 1)))
```

```python
import functools
import math

import jax
import jax.numpy as jnp
from jax import lax
from jax.experimental import pallas as pl
from jax.experimental.pallas import tpu as pltpu

CHUNK = 64
EPS = 1e-6
NEG = -1e30
F_MIN = 1e-12
MLA_HEADS = 8
MLA_NOPE = 64
MLA_ROPE = 32
MLA_V = 64
MLA_Q_LORA = 384
MLA_KV_LORA = 256
ROPE_THETA = 10000.0
MLA_SCALE = (MLA_NOPE + MLA_ROPE) ** -0.5
HG_HEADS = 8
HG_DK = 64
HG_DV = 64
DF_HEADS = 8
DF_DH = 32
DF_DV = 2 * DF_DH
DF_SCALE = DF_DH ** -0.5
N_BRANCH = 3
BR_WIDTH = 512
CONV_W = 3
LOG2E = 1.4426950408889634

LANE = 128
VMEM_LIMIT = 56 * 1024 * 1024
KV_BLOCK = 256
HG_CHUNK = 64
HG_CHUNKS_PER_STEP = 4
HG_SINGLE_REF_LIMIT = 43.0
FF_BLOCK = 1024
MLA_HEADS_PER_STEP = 8
DF_PAIRS_PER_STEP = 2

BF16 = jnp.bfloat16
F32 = jnp.float32


def _cparams(sem):
    return pltpu.CompilerParams(dimension_semantics=sem, vmem_limit_bytes=VMEM_LIMIT)


def _dot(a, b):
    return jnp.dot(a, b, preferred_element_type=F32)


def _dot_nt(a, b):
    return lax.dot_general(a, b, (((1,), (1,)), ((), ())), preferred_element_type=F32)


def _rms(x, g):
    r = lax.rsqrt(jnp.mean(x * x, axis=-1, keepdims=True) + EPS)
    return x * r * g


def _sigmoid(x):
    return 0.5 * jnp.tanh(0.5 * x) + 0.5


def _sigmoid_pair(x):
    e = jnp.exp(-jnp.abs(x))
    r = 1.0 / (1.0 + e)
    er = e * r
    pos = x >= 0.0
    return jnp.where(pos, r, er), jnp.where(pos, er, r)


def _const_spec(shape):
    nd = len(shape)
    return pl.BlockSpec(shape, lambda *_: (0,) * nd)


def _mla_qkv(h, wm_ref, gq_ref, wq_ref, gkv_ref, cq_ref, sq_ref, ck_ref, sk_ref):
    z = _dot(h, wm_ref[...])
    qn = _rms(z[:, :MLA_Q_LORA], gq_ref[...]).astype(BF16)
    q2 = _dot(qn, wq_ref[...])
    nq = MLA_HEADS * LANE
    cq = jnp.concatenate([cq_ref[...]] * MLA_HEADS, axis=1)
    sq = jnp.concatenate([sq_ref[...]] * MLA_HEADS, axis=1)
    q = (q2[:, :nq] * cq + q2[:, nq:] * sq).astype(BF16)
    ckv = _rms(z[:, MLA_Q_LORA:MLA_Q_LORA + MLA_KV_LORA], gkv_ref[...])
    o = MLA_Q_LORA + MLA_KV_LORA
    kr = z[:, o:o + LANE] * ck_ref[...] + z[:, o + LANE:o + 2 * LANE] * sk_ref[...]
    return q, ckv, kr[:, :MLA_ROPE]


def _mla_expand(ckv, krot, wk_ref, e_ref, wvt_ref):
    c = ckv.astype(BF16)
    k = (_dot(c, wk_ref[...]) + _dot(krot.astype(BF16), e_ref[...])).astype(BF16)
    return k, _dot_nt(wvt_ref[...], c).astype(BF16)


def _diff_cols(h, wd_ref):
    z = _dot(h, wd_ref[...])
    n = DF_HEADS * 2 * DF_DH
    return (z[:, :n] * (DF_SCALE * LOG2E)).astype(BF16), z[:, n:2 * n], z[:, 2 * n:]


def _proj_kernel(*refs, tm, aliased):
    (x_ref, gmix_ref, wm_ref, gq_ref, wq_ref, gkv_ref, cq_ref, sq_ref, ck_ref, sk_ref, wk_ref, e_ref, wvt_ref,
     wd_ref) = refs[:14]
    (q_ref, kc_ref, vtm_ref, dq_ref, kd_ref, vtd_ref, ckv_ref, krot_ref, dk_ref, dv_ref) = refs[14 + aliased:]
    h = _rms(x_ref[0], gmix_ref[...]).astype(BF16)
    q, ckv, krot = _mla_qkv(h, wm_ref, gq_ref, wq_ref, gkv_ref, cq_ref, sq_ref, ck_ref, sk_ref)
    k, vt = _mla_expand(ckv, krot, wk_ref, e_ref, wvt_ref)
    dq, dk, dv = _diff_cols(h, wd_ref)
    dvt = dv.T.astype(BF16)
    q_ref[0] = q
    kc_ref[0] = k
    dq_ref[0] = dq
    kd_ref[0] = dk.astype(BF16)
    for i in range(tm // KV_BLOCK):
        vtm_ref[0, i] = vt[:, i * KV_BLOCK:(i + 1) * KV_BLOCK]
        vtd_ref[0, i] = dvt[:, i * KV_BLOCK:(i + 1) * KV_BLOCK]
    ckv_ref[0, 0] = ckv
    krot_ref[0, 0] = krot
    dk_ref[0, 0] = dk
    dv_ref[0, 0] = dv


def _proj(x, gmix, wm, gq, wq, gkv, cq, sq, ck, sk, wk, e, wvt, wd, layer, depth, prev, tm):
    B, T, D = x.shape
    nd = DF_HEADS * 2 * DF_DH
    nk = tm // KV_BLOCK
    tok = lambda w: pl.BlockSpec((1, tm, w), lambda b, t: (b, t, 0))
    tab = pl.BlockSpec((tm, LANE), lambda b, t: (t, 0))
    vts = lambda r: pl.BlockSpec((1, nk, r, KV_BLOCK), lambda b, t: (b, t, 0, 0))
    st = lambda w: pl.BlockSpec((1, 1, tm, w), lambda b, t: (layer, b, t, 0))
    consts = (gmix, wm, gq, wq, gkv)
    consts2 = (wk, e, wvt, wd)
    in_specs = ([tok(D)] + [_const_spec(a.shape) for a in consts] + [tab] * 4
                + [_const_spec(a.shape) for a in consts2])
    args = (x,) + consts + (cq, sq, ck, sk) + consts2
    aliases = {}
    if prev is not None:
        in_specs += [pl.BlockSpec(memory_space=pl.ANY)] * 4
        aliases = {len(args) + i: 6 + i for i in range(4)}
        args += tuple(prev)
    widths = (MLA_KV_LORA, MLA_ROPE, nd, nd)
    outs = pl.pallas_call(
        functools.partial(_proj_kernel, tm=tm, aliased=4 if prev is not None else 0),
        grid=(B, T // tm),
        in_specs=in_specs,
        out_specs=[tok(MLA_HEADS * LANE), tok(MLA_HEADS * LANE), vts(MLA_HEADS * MLA_V), tok(nd), tok(nd), vts(nd)]
        + [st(w) for w in widths],
        out_shape=[jax.ShapeDtypeStruct((B, T, MLA_HEADS * LANE), BF16),
                   jax.ShapeDtypeStruct((B, T, MLA_HEADS * LANE), BF16),
                   jax.ShapeDtypeStruct((B, T // KV_BLOCK, MLA_HEADS * MLA_V, KV_BLOCK), BF16),
                   jax.ShapeDtypeStruct((B, T, nd), BF16),
                   jax.ShapeDtypeStruct((B, T, nd), BF16),
                   jax.ShapeDtypeStruct((B, T // KV_BLOCK, nd, KV_BLOCK), BF16)]
        + [jax.ShapeDtypeStruct((depth, B, T, w), F32) for w in widths],
        input_output_aliases=aliases,
        compiler_params=_cparams(("parallel", "parallel")),
        name="proj",
    )(*args)
    return outs[:6], outs[6:]


def _mla_proj_kernel(x_ref, gmix_ref, wm_ref, gq_ref, wq_ref, gkv_ref, cq_ref, sq_ref, ck_ref, sk_ref,
                     q_ref, ckv_ref, krot_ref):
    h = _rms(x_ref[0], gmix_ref[...]).astype(BF16)
    q_ref[0], ckv_ref[0], krot_ref[0] = _mla_qkv(h, wm_ref, gq_ref, wq_ref, gkv_ref, cq_ref, sq_ref, ck_ref,
                                                  sk_ref)


def _mla_proj(x, gmix, wm, gq, wq, gkv, cq, sq, ck, sk, tm):
    B, T, D = x.shape
    grid = (B, T // tm)
    tok = lambda w: pl.BlockSpec((1, tm, w), lambda b, t: (b, t, 0))
    tab = pl.BlockSpec((tm, LANE), lambda b, t: (t, 0))
    return pl.pallas_call(
        _mla_proj_kernel,
        grid=grid,
        in_specs=[tok(D), _const_spec(gmix.shape), _const_spec(wm.shape), _const_spec(gq.shape),
                  _const_spec(wq.shape), _const_spec(gkv.shape), tab, tab, tab, tab],
        out_specs=[tok(MLA_HEADS * LANE), tok(MLA_KV_LORA), tok(MLA_ROPE)],
        out_shape=[jax.ShapeDtypeStruct((B, T, MLA_HEADS * LANE), BF16),
                   jax.ShapeDtypeStruct((B, T, MLA_KV_LORA), F32),
                   jax.ShapeDtypeStruct((B, T, MLA_ROPE), F32)],
        compiler_params=_cparams(("parallel", "parallel")),
        name="mla_proj",
    )(x, gmix, wm, gq, wq, gkv, cq, sq, ck, sk)


def _past_new_specs(layer, n_past, width):
    return (pl.BlockSpec((1, 1, KV_BLOCK, width), lambda b, t: (layer, b, jnp.minimum(t, n_past - 1), 0)),
            pl.BlockSpec((1, KV_BLOCK, width), lambda b, t: (b, 0, 0)))


def _mla_kv_kernel(pc_ref, nc_ref, pr_ref, nr_ref, wk_ref, e_ref, wvt_ref, k_ref, vt_ref, *, n_past):
    t = pl.program_id(1)

    @pl.when(t < n_past)
    def _():
        k_ref[0], vt_ref[0, 0] = _mla_expand(pc_ref[0, 0], pr_ref[0, 0], wk_ref, e_ref, wvt_ref)

    @pl.when(t >= n_past)
    def _():
        k_ref[0], vt_ref[0, 0] = _mla_expand(nc_ref[0], nr_ref[0], wk_ref, e_ref, wvt_ref)


def _mla_kv(layer, ckv_past, krot_past, ckv_new, krot_new, wk, e, wvt):
    _, B, P, _ = ckv_past.shape
    n_past = P // KV_BLOCK
    nkb = n_past + 1
    return pl.pallas_call(
        functools.partial(_mla_kv_kernel, n_past=n_past),
        grid=(B, nkb),
        in_specs=[*_past_new_specs(layer, n_past, MLA_KV_LORA), *_past_new_specs(layer, n_past, MLA_ROPE),
                  _const_spec(wk.shape), _const_spec(e.shape), _const_spec(wvt.shape)],
        out_specs=[pl.BlockSpec((1, KV_BLOCK, MLA_HEADS * LANE), lambda b, t: (b, t, 0)),
                   pl.BlockSpec((1, 1, MLA_HEADS * MLA_V, KV_BLOCK), lambda b, t: (b, t, 0, 0))],
        out_shape=[jax.ShapeDtypeStruct((B, nkb * KV_BLOCK, MLA_HEADS * LANE), BF16),
                   jax.ShapeDtypeStruct((B, nkb, MLA_HEADS * MLA_V, KV_BLOCK), BF16)],
        compiler_params=_cparams(("parallel", "parallel")),
        name="mla_kv",
    )(ckv_past, ckv_new, krot_past, krot_new, wk, e, wvt)


def _visible_blocks(q0, tq, kvalid, nkb):
    n_full = jnp.minimum((((q0 >> 6) + 1) * CHUNK) // KV_BLOCK, kvalid // KV_BLOCK)
    last = (((q0 + tq - 1) >> 6) + 1) * CHUNK
    n_vis = jnp.minimum((last + KV_BLOCK - 1) // KV_BLOCK, nkb)
    return n_full, n_vis


def _flash_t(streams, k_ref, vt_ref, m_s, acc_s, n_full, n_vis, q0, tq, kvalid):
    ones = jnp.ones((16, KV_BLOCK), BF16)
    n = len(streams)

    def step(kbs, masked):
        ss = []
        for qm, ksl, _ in streams:
            ss.append([_dot_nt(k_ref[0, pl.ds(pl.multiple_of(kb * KV_BLOCK, KV_BLOCK), KV_BLOCK), ksl], qm)
                       for kb in kbs])
        if any(masked):
            qchunk = (q0 + lax.broadcasted_iota(jnp.int32, (KV_BLOCK, tq), 1)) >> 6
            for j, kb in enumerate(kbs):
                if masked[j]:
                    kpos = kb * KV_BLOCK + lax.broadcasted_iota(jnp.int32, (KV_BLOCK, tq), 0)
                    vis = jnp.where(kpos < kvalid, kpos >> 6, jnp.int32(2 ** 30)) <= qchunk
                    for s in ss:
                        s[j] = jnp.where(vis, s[j], NEG)
        ms = []
        for i, s in enumerate(ss):
            m = m_s[i]
            for sj in s:
                m = jnp.maximum(m, jnp.max(sj, axis=0, keepdims=True))
            ms.append(m)
        ps = [[jnp.exp2(sj - m_new).astype(BF16) for sj in s] for s, m_new in zip(ss, ms)]
        for i, ((_, _, vsl), m_new, p) in enumerate(zip(streams, ms, ps)):
            acc = jnp.exp2(m_s[i] - m_new) * acc_s[i]
            for kb, pj in zip(kbs, p):
                vt = jnp.concatenate([vt_ref[0, kb, vsl, :], ones], axis=0)
                acc = acc + _dot(vt, pj)
            acc_s[i] = acc
            m_s[i] = m_new

    for i in range(n):
        m_s[i] = jnp.full((1, tq), -jnp.inf, F32)
        acc_s[i] = jnp.zeros((MLA_V + 16, tq), F32)
    n_pairs = n_full // 2

    def pair_step(j, c):
        step([2 * j, 2 * j + 1], (False, False))
        return c

    def masked_step(kb, c):
        step([kb], (True,))
        return c

    lax.fori_loop(0, n_pairs, pair_step, 0)
    mixed = jnp.logical_and(n_full - 2 * n_pairs == 1, n_vis > n_full)

    @pl.when(mixed)
    def _():
        step([n_full - 1, n_full], (False, True))

    lax.fori_loop(jnp.where(mixed, n_full + 1, 2 * n_pairs), n_vis, masked_step, 0)
    return [acc_s[i] for i in range(n)]


def _attn_mla_kernel(q_ref, k_ref, vt_ref, o_ref, m_s, acc_s, *, tq, q_off, kvalid, nkb):
    q0 = q_off + pl.program_id(2) * tq
    n_full, n_vis = _visible_blocks(q0, tq, kvalid, nkb)
    streams = []
    for hh in range(MLA_HEADS_PER_STEP):
        lanes = slice(hh * LANE, (hh + 1) * LANE)
        streams.append((q_ref[0, :, lanes], lanes, slice(hh * MLA_V, (hh + 1) * MLA_V)))
    accs = _flash_t(streams, k_ref, vt_ref, m_s, acc_s, n_full, n_vis, q0, tq, kvalid)
    outs = [acc[:MLA_V] * (1.0 / acc[MLA_V:MLA_V + 1]) for acc in accs]
    for i in range(0, MLA_HEADS_PER_STEP, 2):
        o_ref[0, :, i * MLA_V:(i + 2) * MLA_V] = jnp.concatenate(outs[i:i + 2], axis=0).T.astype(BF16)


def _attn_mla(q, k, vt, tq, q_off, kvalid):
    B, T, _ = q.shape
    Tk = k.shape[1]
    nkb = Tk // KV_BLOCK
    hps = MLA_HEADS_PER_STEP
    kern = functools.partial(_attn_mla_kernel, tq=tq, q_off=q_off, kvalid=kvalid, nkb=nkb)
    return pl.pallas_call(
        kern,
        grid=(B, MLA_HEADS // hps, T // tq),
        in_specs=[pl.BlockSpec((1, tq, hps * LANE), lambda b, p, i: (b, i, p)),
                  pl.BlockSpec((1, Tk, hps * LANE), lambda b, p, i: (b, 0, p)),
                  pl.BlockSpec((1, nkb, hps * MLA_V, KV_BLOCK), lambda b, p, i: (b, 0, p, 0))],
        out_specs=pl.BlockSpec((1, tq, hps * MLA_V), lambda b, p, i: (b, i, p)),
        out_shape=jax.ShapeDtypeStruct((B, T, MLA_HEADS * MLA_V), BF16),
        scratch_shapes=[pltpu.VMEM((hps, 1, tq), F32), pltpu.VMEM((hps, MLA_V + 16, tq), F32)],
        compiler_params=_cparams(("parallel", "parallel", "arbitrary")),
        name="attn_mla",
    )(q, k, vt)


def _attn_diff_kernel(q_ref, k_ref, vt_ref, lam_ref, g_ref, o_ref, m_s, acc_s, *, tq, q_off, kvalid, nkb,
                      lam_init):
    q0 = q_off + pl.program_id(2) * tq
    n_full, n_vis = _visible_blocks(q0, tq, kvalid, nkb)
    lamv = lam_ref[...]
    lam = (jnp.exp(jnp.sum(lamv[0:1] * lamv[1:2], axis=-1, keepdims=True))
           - jnp.exp(jnp.sum(lamv[2:3] * lamv[3:4], axis=-1, keepdims=True)) + lam_init)
    lane = lax.broadcasted_iota(jnp.int32, (1, LANE), 1)
    streams = []
    for pr in range(DF_PAIRS_PER_STEP):
        lanes = slice(pr * LANE, (pr + 1) * LANE)
        q = q_ref[0, :, lanes]
        for hh in range(2):
            for j in range(2):
                lo = hh * 2 * DF_DH + j * DF_DH
                qm = q * jnp.where((lane >= lo) & (lane < lo + DF_DH), 1.0, 0.0).astype(BF16)
                streams.append((qm, lanes, slice((2 * pr + hh) * DF_DV, (2 * pr + hh + 1) * DF_DV)))
    accs = _flash_t(streams, k_ref, vt_ref, m_s, acc_s, n_full, n_vis, q0, tq, kvalid)
    maps = [acc[:DF_DV] * (1.0 / acc[DF_DV:DF_DV + 1]) for acc in accs]
    for pr in range(DF_PAIRS_PER_STEP):
        outs = []
        for hh in range(2):
            i = 4 * pr + 2 * hh
            o = maps[i] - lam * maps[i + 1]
            outs.append(o * lax.rsqrt(jnp.mean(o * o, axis=0, keepdims=True) + EPS))
        ot = jnp.concatenate(outs, axis=0).T
        o_ref[0, :, pr * LANE:(pr + 1) * LANE] = (ot * g_ref[...] * (1.0 - lam_init)).astype(BF16)


def _attn_diff(q, k, vt, lam_rows, g2, tq, q_off, kvalid, lam_init):
    B, T, _ = q.shape
    Tk = k.shape[1]
    nkb = Tk // KV_BLOCK
    pps = DF_PAIRS_PER_STEP
    kern = functools.partial(_attn_diff_kernel, tq=tq, q_off=q_off, kvalid=kvalid, nkb=nkb,
                             lam_init=lam_init)
    return pl.pallas_call(
        kern,
        grid=(B, DF_HEADS // (2 * pps), T // tq),
        in_specs=[pl.BlockSpec((1, tq, pps * LANE), lambda b, p, i: (b, i, p)),
                  pl.BlockSpec((1, Tk, pps * LANE), lambda b, p, i: (b, 0, p)),
                  pl.BlockSpec((1, nkb, pps * 2 * DF_DV, KV_BLOCK), lambda b, p, i: (b, 0, p, 0)),
                  _const_spec(lam_rows.shape), _const_spec(g2.shape)],
        out_specs=pl.BlockSpec((1, tq, pps * 2 * DF_DV), lambda b, p, i: (b, i, p)),
        out_shape=jax.ShapeDtypeStruct((B, T, DF_HEADS * DF_DV), BF16),
        scratch_shapes=[pltpu.VMEM((4 * pps, 1, tq), F32), pltpu.VMEM((4 * pps, DF_DV + 16, tq), F32)],
        compiler_params=_cparams(("parallel", "parallel", "arbitrary")),
        name="attn_diff",
    )(q, k, vt, lam_rows, g2)


def _diff_proj_kernel(x_ref, gmix_ref, wd_ref, q_ref, dk_ref, dv_ref):
    h = _rms(x_ref[0], gmix_ref[...]).astype(BF16)
    q_ref[0], dk_ref[0], dv_ref[0] = _diff_cols(h, wd_ref)


def _diff_proj(x, gmix, wd, tm):
    B, T, D = x.shape
    n = DF_HEADS * 2 * DF_DH
    tok = lambda w: pl.BlockSpec((1, tm, w), lambda b, t: (b, t, 0))
    return pl.pallas_call(
        _diff_proj_kernel,
        grid=(B, T // tm),
        in_specs=[tok(D), _const_spec(gmix.shape), _const_spec(wd.shape)],
        out_specs=[tok(n), tok(n), tok(n)],
        out_shape=[jax.ShapeDtypeStruct((B, T, n), BF16),
                   jax.ShapeDtypeStruct((B, T, n), F32),
                   jax.ShapeDtypeStruct((B, T, n), F32)],
        compiler_params=_cparams(("parallel", "parallel")),
        name="diff_proj",
    )(x, gmix, wd)


def _diff_kv_kernel(pk_ref, nk_ref, pv_ref, nv_ref, k_ref, vt_ref, *, n_past):
    t = pl.program_id(1)

    def merged(ref):
        return jnp.concatenate([ref[0, 0, :, h, :] for h in range(DF_HEADS)], axis=-1)

    @pl.when(t < n_past)
    def _():
        k_ref[0] = merged(pk_ref).astype(BF16)
        vt_ref[0, 0] = merged(pv_ref).T.astype(BF16)

    @pl.when(t >= n_past)
    def _():
        k_ref[0] = nk_ref[0].astype(BF16)
        vt_ref[0, 0] = nv_ref[0].T.astype(BF16)


def _diff_kv(layer, dk_past, dv_past, dk_new, dv_new):
    _, B, P, nh, w = dk_past.shape
    n = nh * w
    n_past = P // KV_BLOCK
    nkb = n_past + 1
    past = pl.BlockSpec((1, 1, KV_BLOCK, nh, w), lambda b, t: (layer, b, jnp.minimum(t, n_past - 1), 0, 0))
    new = pl.BlockSpec((1, KV_BLOCK, n), lambda b, t: (b, 0, 0))
    return pl.pallas_call(
        functools.partial(_diff_kv_kernel, n_past=n_past),
        grid=(B, nkb),
        in_specs=[past, new, past, new],
        out_specs=[pl.BlockSpec((1, KV_BLOCK, n), lambda b, t: (b, t, 0)),
                   pl.BlockSpec((1, 1, n, KV_BLOCK), lambda b, t: (b, t, 0, 0))],
        out_shape=[jax.ShapeDtypeStruct((B, nkb * KV_BLOCK, n), BF16),
                   jax.ShapeDtypeStruct((B, nkb, n, KV_BLOCK), BF16)],
        compiler_params=_cparams(("parallel", "parallel")),
        name="diff_kv",
    )(dk_past, dk_new, dv_past, dv_new)


def _split2(x):
    a = x.astype(BF16)
    return a, (x - a.astype(F32)).astype(BF16)


def _level_ref_rows(b, size):
    half = size // 2
    C = b.shape[0]
    pieces = [jnp.broadcast_to(b[i * size + half - 1:i * size + half, :], (size, b.shape[1]))
              for i in range(C // size)]
    return pieces[0] if len(pieces) == 1 else jnp.concatenate(pieces, axis=0)


def _base_ref_rows(b):
    C, W = b.shape
    sub = lax.broadcasted_iota(jnp.int32, (8, W), 0)
    pieces = [jnp.where(sub < 4, jnp.broadcast_to(b[8 * i:8 * i + 1, :], (8, W)),
                        jnp.broadcast_to(b[8 * i + 4:8 * i + 5, :], (8, W))) for i in range(C // 8)]
    return jnp.concatenate(pieces, axis=0)


def _hgrn_kernel(x_ref, gmix_ref, wh_ref, lbl_ref, st0_ref, gh_ref, gavg_ref, ob_ref, st_ref, z_s, b_s,
                 *, layer, tm):
    C = min(HG_CHUNK, tm)
    W = HG_HEADS * HG_DK
    ti = pl.program_id(1)

    @pl.when(ti == 0)
    def _():
        st_ref[0] = st0_ref[0]

    h = _rms(x_ref[0], gmix_ref[...]).astype(BF16)
    z_s[...] = _dot(h, wh_ref[...])

    lg = lbl_ref[...]
    e = jnp.exp(lg - jnp.max(lg, axis=0, keepdims=True))
    sm = e / jnp.sum(e, axis=0, keepdims=True)
    lb = jnp.zeros((1, W), F32)
    for i in range(1, layer + 1):
        lb = lb + sm[i:i + 1]

    row = lax.broadcasted_iota(jnp.int32, (C, C), 0)
    col = lax.broadcasted_iota(jnp.int32, (C, C), 1)
    tril_b = col <= row
    tril = tril_b.astype(BF16)
    rsub = lax.broadcasted_iota(jnp.int32, (C, LANE), 0)
    lane = lax.broadcasted_iota(jnp.int32, (C, LANE), 1)
    head_lo = lane < HG_DK
    sizes = [s for s in (64, 32, 16, 8) if s <= C]
    blk_masks = {s: (row // s) == (col // s) for s in sizes}
    base_mask = ((row // 4) == (col // 4)) & tril_b
    diag = ((lax.broadcasted_iota(jnp.int32, (LANE, LANE), 0) // HG_DK)
            == (lax.broadcasted_iota(jnp.int32, (LANE, LANE), 1) // HG_DK))
    mid = C // 2 - 1
    n_pairs = HG_HEADS // 2

    group = min(HG_CHUNKS_PER_STEP, tm // C)

    def prep(j, spread):
        for i in range(group):
            r0 = pl.multiple_of((j * group + i) * C, C)
            hq = z_s[pl.ds(r0, C), 0:W]
            hf = z_s[pl.ds(r0, C), W:2 * W]
            hg = z_s[pl.ds(r0, C), 3 * W:4 * W]
            sig, sig_neg = _sigmoid_pair(hf)
            f = sig + lb * sig_neg
            l1, l2 = _split2(jnp.log(jnp.maximum(f, F_MIN)))
            b = _dot(tril, l1) + _dot(tril, l2)
            z_s[pl.ds(r0, C), 0:W] = hq * _sigmoid(hq)
            z_s[pl.ds(r0, C), W:2 * W] = (1.0 - lb) * sig_neg
            z_s[pl.ds(r0, C), 3 * W:4 * W] = hg * _sigmoid(hg)
            b_s[pl.ds(r0, C), :] = b
            spread = jnp.maximum(spread, jnp.maximum(b[0:1] - b[mid:mid + 1], b[mid:mid + 1] - b[C - 1:C]))
        return spread

    spread = lax.fori_loop(0, tm // (C * group), prep, jnp.zeros((1, W), F32))
    single_ref_ok = jnp.max(spread) <= HG_SINGLE_REF_LIMIT

    def heads(x):
        return (jnp.where(head_lo, x, 0.0).astype(BF16), jnp.where(head_lo, 0.0, x).astype(BF16))

    def finish(cs, bs, qs, ks, vs, a_all):
        o_all = []
        for i, c in enumerate(cs):
            b, q, k, v = bs[i], qs[i], ks[i], vs[i]
            b_end = b[C - 1:C, :]
            qe = (q * jnp.exp(b)).astype(BF16)
            ke = (k * jnp.exp(b_end - b)).astype(BF16)
            d_end = jnp.exp(b_end)
            o_pairs = []
            for p in range(n_pairs):
                sl = slice(p * LANE, (p + 1) * LANE)
                v0, v1 = heads(v[:, sl])
                a0, a1 = a_all[i][p]
                st = st_ref[0, p]
                o_p = (_dot(a0.astype(BF16), v0) + _dot(a1.astype(BF16), v1)
                       + _dot_nt(qe[:, sl], st.astype(BF16)))
                upd = _dot(v[:, sl].T.astype(BF16), ke[:, sl])
                st_ref[0, p] = st * d_end[:, sl] + jnp.where(diag, upd, 0.0)
                o_pairs.append(o_p)
            o_all.append(jnp.concatenate(o_pairs, axis=1))
        for c, o in zip(cs, o_all):
            r0 = pl.multiple_of(c * C, C)
            ms = _dot((o * o).astype(BF16), gavg_ref[...])
            ob = o * lax.rsqrt(ms + EPS) * gh_ref[...] * z_s[pl.ds(r0, C), 3 * W:4 * W]
            ob_ref[0, pl.ds(r0, C), :] = ob.astype(BF16)

    def load(cs):
        out = []
        for c in cs:
            r0 = pl.multiple_of(c * C, C)
            out.append((b_s[pl.ds(r0, C), :], z_s[pl.ds(r0, C), 0:W], z_s[pl.ds(r0, C), W:2 * W],
                        z_s[pl.ds(r0, C), 2 * W:3 * W]))
        return tuple(zip(*out))

    def fast(j, carry):
        cs = [j * group + i for i in range(group)]
        bs, qs, ks, vs = load(cs)
        a_all = []
        for b, q, k in zip(bs, qs, ks):
            ref = b[mid:mid + 1, :]
            qt = q * jnp.exp(b - ref)
            kt = (k * jnp.exp(ref - b)).astype(BF16)
            a_c = []
            for p in range(n_pairs):
                sl = slice(p * LANE, (p + 1) * LANE)
                q0, q1 = heads(qt[:, sl])
                a_c.append((jnp.where(tril_b, _dot_nt(q0, kt[:, sl]), 0.0),
                            jnp.where(tril_b, _dot_nt(q1, kt[:, sl]), 0.0)))
            a_all.append(a_c)
        finish(cs, bs, qs, ks, vs, a_all)
        return carry

    def safe(c, carry):
        bs, qs, ks, vs = load([c])
        a_c = []
        for p in range(n_pairs):
            sl = slice(p * LANE, (p + 1) * LANE)
            bp, qp, kp = bs[0][:, sl], qs[0][:, sl], ks[0][:, sl]
            a_h = [jnp.zeros((C, C), F32), jnp.zeros((C, C), F32)]
            for s in sizes:
                ref = _level_ref_rows(bp, s)
                upper = (rsub & (s - 1)) >= (s // 2)
                qt = heads(qp * jnp.exp(jnp.where(upper, bp - ref, NEG)))
                kt = (kp * jnp.exp(jnp.where(upper, NEG, ref - bp))).astype(BF16)
                for hh in range(2):
                    pr = _dot_nt(qt[hh], kt)
                    a_h[hh] = a_h[hh] + (pr if s == C else jnp.where(blk_masks[s], pr, 0.0))
            ref = _base_ref_rows(bp)
            qt = heads(qp * jnp.exp(bp - ref))
            kt = (kp * jnp.exp(ref - bp)).astype(BF16)
            for hh in range(2):
                a_h[hh] = a_h[hh] + jnp.where(base_mask, _dot_nt(qt[hh], kt), 0.0)
            a_c.append(tuple(a_h))
        finish([c], bs, qs, ks, vs, [a_c])
        return carry

    @pl.when(single_ref_ok)
    def _():
        lax.fori_loop(0, tm // (C * group), fast, 0)

    @pl.when(jnp.logical_not(single_ref_ok))
    def _():
        lax.fori_loop(0, tm // C, safe, 0)


def _hgrn(x, gmix, wh, lbl, st0, gh, gavg, layer, tm):
    B, T, D = x.shape
    W = HG_HEADS * HG_DK
    st_spec = pl.BlockSpec((1, HG_HEADS // 2, LANE, LANE), lambda b, t: (b, 0, 0, 0))
    return pl.pallas_call(
        functools.partial(_hgrn_kernel, layer=layer, tm=tm),
        grid=(B, T // tm),
        in_specs=[pl.BlockSpec((1, tm, D), lambda b, t: (b, t, 0)), _const_spec(gmix.shape),
                  _const_spec(wh.shape), _const_spec(lbl.shape), st_spec, _const_spec(gh.shape),
                  _const_spec(gavg.shape)],
        out_specs=[pl.BlockSpec((1, tm, W), lambda b, t: (b, t, 0)), st_spec],
        out_shape=[jax.ShapeDtypeStruct((B, T, W), BF16),
                   jax.ShapeDtypeStruct((B, HG_HEADS // 2, LANE, LANE), F32)],
        scratch_shapes=[pltpu.VMEM((tm, 4 * W), F32), pltpu.VMEM((tm, W), F32)],
        compiler_params=_cparams(("parallel", "arbitrary")),
        name="hgrn",
    )(x, gmix, wh, lbl, st0, gh, gavg)


def _post_kernel(x_ref, oa_ref, ob_ref, oc_ref, gmix_ref, wg_ref, wb_ref, wo_ref, gffn_ref, wup_ref,
                 cw_ref, cb_ref, wdn_ref, conv0_ref, gfin_ref, y_ref, conv_ref, carry_s, *, tm, d_ff, final):
    ti = pl.program_id(1)
    x = x_ref[0]
    D = x.shape[1]
    h = _rms(x, gmix_ref[...]).astype(BF16)
    mixed = jnp.zeros((tm, D), F32)
    for n, o_ref in enumerate((oa_ref, ob_ref, oc_ref)):
        gate = _sigmoid(_dot(h, wg_ref[:, n * D:(n + 1) * D]))
        mixed = mixed + gate * _dot(o_ref[0], wb_ref[n])
    x1 = x + _dot(mixed.astype(BF16), wo_ref[...])
    xn = _rms(x1, gffn_ref[...]).astype(BF16)

    @pl.when(ti == 0)
    def _():
        carry_s[0:CONV_W - 1, :] = conv0_ref[0]

    acc = jnp.zeros((tm, D), F32)
    for f in range(0, d_ff, FF_BLOCK):
        w = min(FF_BLOCK, d_ff - f)
        cols = slice(f, f + w)
        a = _dot(xn, wup_ref[:, cols])
        vv = _dot(xn, wup_ref[:, d_ff + f:d_ff + f + w])
        p0 = carry_s[0:1, cols]
        p1 = carry_s[1:2, cols]
        r1 = pltpu.roll(a, 1, 0)
        r2 = pltpu.roll(a, 2, 0)
        last2 = r2[0:8, :]
        rows = lax.broadcasted_iota(jnp.int32, (8, w), 0)
        a1 = jnp.concatenate([jnp.where(rows < 1, p1, r1[0:8]), r1[8:]], axis=0) if tm > 8 else \
            jnp.where(rows < 1, p1, r1)
        h2 = jnp.where(rows < 1, p0, jnp.where(rows < 2, p1, last2))
        a2 = jnp.concatenate([h2, r2[8:]], axis=0) if tm > 8 else h2
        cc = (cb_ref[:, cols] + a2 * cw_ref[0:1, cols] + a1 * cw_ref[1:2, cols] + a * cw_ref[2:3, cols])
        act = (cc * _sigmoid(cc) * vv).astype(BF16)
        acc = acc + _dot(act, wdn_ref[cols, :])
        carry_s[:, cols] = last2
        conv_ref[0, :, cols] = last2[0:CONV_W - 1, :]
    x2 = x1 + acc
    y_ref[0] = _rms(x2, gfin_ref[...]) if final else x2


def _post(x, oa, ob, oc, gmix, wg, wb, wo, gffn, wup, cw, cb, wdn, conv0, gfin, tm, final):
    B, T, D = x.shape
    d_ff = wdn.shape[0]
    tok = lambda w: pl.BlockSpec((1, tm, w), lambda b, t: (b, t, 0))
    cspec = pl.BlockSpec((1, CONV_W - 1, d_ff), lambda b, t: (b, 0, 0))
    one = lambda a: pl.BlockSpec(a.shape, lambda *_: (0,) * a.ndim, pipeline_mode=pl.Buffered(1))
    return pl.pallas_call(
        functools.partial(_post_kernel, tm=tm, d_ff=d_ff, final=final),
        grid=(B, T // tm),
        in_specs=[tok(D), tok(BR_WIDTH), tok(BR_WIDTH), tok(BR_WIDTH), one(gmix), one(wg), one(wb), one(wo),
                  one(gffn), one(wup), one(cw), one(cb), one(wdn), cspec, one(gfin)],
        out_specs=[tok(D), cspec],
        out_shape=[jax.ShapeDtypeStruct((B, T, D), F32),
                   jax.ShapeDtypeStruct((B, CONV_W - 1, d_ff), F32)],
        scratch_shapes=[pltpu.VMEM((8, d_ff), F32)],
        compiler_params=_cparams(("parallel", "arbitrary")),
        name="post",
    )(x, oa, ob, oc, gmix, wg, wb, wo, gffn, wup, cw, cb, wdn, conv0, gfin)


def _rot_half_cols(w):
    half = w.shape[-1] // 2
    return jnp.concatenate([-w[..., half:], w[..., :half]], axis=-1)


def _pad_cols(w, width):
    return jnp.pad(w, [(0, 0)] * (w.ndim - 1) + [(0, width - w.shape[-1])])


def _layer_weights(w_in, w_uq, w_ukv):
    D = w_in.shape[0]
    o_q, o_kv, o_kr = 0, MLA_Q_LORA, MLA_Q_LORA + MLA_KV_LORA
    o_h = o_kr + MLA_ROPE
    o_d = o_h + 4 * HG_HEADS * HG_DK
    o_g = o_d + 3 * DF_HEADS * DF_DV
    w_kr = w_in[:, o_kr:o_h]
    wm = jnp.concatenate([w_in[:, o_q:o_kr], _pad_cols(w_kr, LANE), _pad_cols(_rot_half_cols(w_kr), LANE)],
                         axis=1)
    wh = w_in[:, o_h:o_d]
    wd = w_in[:, o_d:o_g]
    wg = w_in[:, o_g:]
    uq = w_uq.reshape(MLA_Q_LORA, MLA_HEADS, MLA_NOPE + MLA_ROPE)
    nope, rope = uq[..., :MLA_NOPE], uq[..., MLA_NOPE:]
    zr = jnp.zeros((MLA_Q_LORA, MLA_HEADS, LANE - MLA_NOPE - MLA_ROPE), w_uq.dtype)
    wq1 = jnp.concatenate([nope, rope, zr], axis=-1).reshape(MLA_Q_LORA, MLA_HEADS * LANE)
    wq2 = jnp.concatenate([jnp.zeros_like(nope), _rot_half_cols(rope), zr], axis=-1)
    wq = jnp.concatenate([wq1, wq2.reshape(MLA_Q_LORA, MLA_HEADS * LANE)], axis=1)
    ukv = w_ukv.reshape(MLA_KV_LORA, MLA_HEADS, MLA_NOPE + MLA_V)
    wk = _pad_cols(ukv[..., :MLA_NOPE], LANE).reshape(MLA_KV_LORA, MLA_HEADS * LANE)
    wvt = ukv[..., MLA_NOPE:].reshape(MLA_KV_LORA, MLA_HEADS * MLA_V).T
    c = lambda a: a.astype(BF16)
    return c(wm), c(wh), c(wd), c(wg), c(wq), c(wk), c(wvt)


def _rope_tables(pos):
    half = MLA_ROPE // 2
    inv = ROPE_THETA ** (-jnp.arange(half, dtype=F32) / half)
    ang = pos.astype(F32)[:, None] * inv[None, :]
    cos2 = jnp.concatenate([jnp.cos(ang)] * 2, axis=1)
    sin2 = jnp.concatenate([jnp.sin(ang)] * 2, axis=1)
    T = pos.shape[0]
    c = MLA_SCALE * LOG2E
    zq = jnp.zeros((T, LANE - MLA_NOPE - MLA_ROPE), F32)
    cq = jnp.concatenate([jnp.full((T, MLA_NOPE), c, F32), c * cos2, zq], axis=1)
    sq = jnp.concatenate([jnp.zeros((T, MLA_NOPE), F32), c * sin2, zq], axis=1)
    return cq, sq, _pad_cols(cos2, LANE), _pad_cols(sin2, LANE)


def _rope_place():
    r = jnp.arange(MLA_ROPE)
    e = jnp.zeros((MLA_ROPE, MLA_HEADS, LANE), F32)
    e = e.at[r, :, MLA_NOPE + r].set(1.0)
    return e.reshape(MLA_ROPE, MLA_HEADS * LANE).astype(BF16)


def _pair_state(s):
    B = s.shape[0]
    st = jnp.swapaxes(s, -1, -2).reshape(B, HG_HEADS // 2, 2, HG_DV, HG_DK)
    z = jnp.zeros_like(st[:, :, 0])
    top = jnp.concatenate([st[:, :, 0], z], axis=-1)
    bot = jnp.concatenate([z, st[:, :, 1]], axis=-1)
    return jnp.concatenate([top, bot], axis=-2)


def _unpair_state(sp):
    a = sp[:, :, :HG_DV, :HG_DK]
    b = sp[:, :, HG_DV:, HG_DK:]
    st = jnp.stack([a, b], axis=2).reshape(sp.shape[0], HG_HEADS, HG_DV, HG_DK)
    return jnp.swapaxes(st, -1, -2)


def _pad_rows(a, n):
    return jnp.pad(a, ((0, 0), (0, n - a.shape[1]), (0, 0)))


def _layer(l, depth, x, pos, past, prev, s0, conv0, lw, prm, final):
    (norm_mix_g, mla_q_norm_g, mla_kv_norm_g, hgrn_lb_logits, hgrn_norm_g, diff_lambda, diff_norm_g,
     w_branch, w_out, norm_ffn_g, ffn_w_up, ffn_conv_w, ffn_conv_b, ffn_w_down, norm_final_g) = prm
    wm, wh, wd, wg, wq, wk, wvt = lw
    B, T, D = x.shape
    row = lambda a: a.reshape(1, -1)
    gmix = row(norm_mix_g)
    tm = min(512, T)
    place = _rope_place()

    if past is None:
        q_off, kvalid = 0, T
        tq = min(KV_BLOCK, T)
        cq, sq, ck, sk = _rope_tables(pos)
        (q_in, k_mla, vt_mla, dq_in, k_d, vt_d), new_rows = _proj(
            x, gmix, wm, row(mla_q_norm_g), wq, row(mla_kv_norm_g), cq, sq, ck, sk, wk, place, wvt, wd,
            l, depth, prev, tm)
    else:
        n_tok = B * T
        xf = x.reshape(1, n_tok, D)
        cq, sq, ck, sk = _rope_tables(jnp.tile(pos, B))
        q, ckv, krot = _mla_proj(xf, gmix, wm, row(mla_q_norm_g), wq, row(mla_kv_norm_g), cq, sq, ck, sk, n_tok)
        dq, dk, dv = _diff_proj(xf, gmix, wd, n_tok)
        q, ckv, krot, dq, dk, dv = (a.reshape(B, T, -1) for a in (q, ckv, krot, dq, dk, dv))
        P = past[0].shape[2]
        assert P % KV_BLOCK == 0 and T <= KV_BLOCK, (P, T)
        q_off, kvalid = P, P + T
        tq = LANE
        q_in, dq_in = _pad_rows(q, tq), _pad_rows(dq, tq)
        blk = lambda a: _pad_rows(a, KV_BLOCK)
        k_mla, vt_mla = _mla_kv(l, past[0], past[1], blk(ckv), blk(krot), wk, place, wvt)
        k_d, vt_d = _diff_kv(l, past[2], past[3], blk(dk), blk(dv))
        new_rows = (ckv, krot, dk, dv)
    o_a = _attn_mla(q_in, k_mla, vt_mla, tq, q_off, kvalid)[:, :T]
    lam_init = 0.8 - 0.6 * math.exp(-0.3 * l)
    o_c = _attn_diff(dq_in, k_d, vt_d, diff_lambda, row(jnp.tile(diff_norm_g, 2)), tq, q_off, kvalid,
                     lam_init)[:, :T]

    gavg = jnp.kron(jnp.eye(HG_HEADS, dtype=F32), jnp.full((HG_DV, HG_DV), 1.0 / HG_DV, F32)).astype(BF16)
    o_b, st = _hgrn(x, gmix, wh, hgrn_lb_logits, _pair_state(s0), row(jnp.tile(hgrn_norm_g, HG_HEADS)),
                    gavg, l, tm)

    y, conv_new = _post(x, o_a, o_b, o_c, gmix, wg, w_branch.astype(BF16), w_out.astype(BF16),
                        row(norm_ffn_g), ffn_w_up.astype(BF16), ffn_conv_w, row(ffn_conv_b),
                        ffn_w_down.astype(BF16), conv0, row(norm_final_g), tm, final)
    return y, new_rows, (_unpair_state(st), conv_new)


def kernel(x_prompt, x_sample, cache_mla_ckv, cache_mla_krope, cache_diff_k, cache_diff_v, state_hgrn,
           state_ffn_conv, norm_mix_g, w_in, mla_q_norm_g, mla_w_uq, mla_kv_norm_g, mla_w_ukv, hgrn_lb_logits,
           hgrn_norm_g, diff_lambda, diff_norm_g, w_branch, w_out, norm_ffn_g, ffn_w_up, ffn_conv_w,
           ffn_conv_b, ffn_w_down, norm_final_g):
    depth = w_in.shape[0]
    B, T, _ = x_prompt.shape
    Bs, Ts, _ = x_sample.shape
    P = cache_mla_ckv.shape[2]
    d_ff = ffn_w_down.shape[1]
    pos_p = jnp.arange(T)
    pos_s = P + jnp.arange(Ts)
    yp, ys = x_prompt, x_sample
    p_rows = None
    s_rows, p_small, s_small = [], [], []
    for l in range(depth):
        lw = _layer_weights(w_in[l], mla_w_uq[l], mla_w_ukv[l])
        prm = (norm_mix_g[l], mla_q_norm_g[l], mla_kv_norm_g[l], hgrn_lb_logits, hgrn_norm_g[l], diff_lambda[l],
               diff_norm_g[l], w_branch[l], w_out[l], norm_ffn_g[l], ffn_w_up[l], ffn_conv_w[l], ffn_conv_b[l],
               ffn_w_down[l], norm_final_g)
        final = l == depth - 1
        yp, p_rows, small = _layer(l, depth, yp, pos_p, None, p_rows, jnp.zeros((B, HG_HEADS, HG_DK, HG_DV), F32),
                                   jnp.zeros((B, CONV_W - 1, d_ff), F32), lw, prm, final)
        p_small.append(small)
        ys, rows, small = _layer(l, depth, ys, pos_s,
                                 (cache_mla_ckv, cache_mla_krope, cache_diff_k, cache_diff_v), None,
                                 state_hgrn[l], state_ffn_conv[l], lw, prm, final)
        s_rows.append(rows)
        s_small.append(small)
    stk = lambda states, i: jnp.stack([s[i] for s in states], axis=0)
    heads = lambda a, w: a.reshape(a.shape[:3] + (-1, w))
    s_rows = [stk(s_rows, i) for i in range(4)]
    out_rows = lambda r: (r[0], r[1], heads(r[2], 2 * DF_DH), heads(r[3], DF_DV))
    return ((yp, ys) + out_rows(p_rows) + (stk(p_small, 0), stk(p_small, 1))
            + out_rows(s_rows) + (stk(s_small, 0), stk(s_small, 1)))
```

```python
import functools
import math

import jax
import jax.numpy as jnp
from jax import lax
from jax.experimental import pallas as pl
from jax.experimental.pallas import tpu as pltpu

CHUNK = 64
EPS = 1e-6
NEG = -1e30
F_MIN = 1e-12
MLA_HEADS = 8
MLA_NOPE = 64
MLA_ROPE = 32
MLA_V = 64
MLA_Q_LORA = 384
MLA_KV_LORA = 256
ROPE_THETA = 10000.0
MLA_SCALE = (MLA_NOPE + MLA_ROPE) ** -0.5
HG_HEADS = 8
HG_DK = 64
HG_DV = 64
DF_HEADS = 8
DF_DH = 32
DF_DV = 2 * DF_DH
DF_SCALE = DF_DH ** -0.5
N_BRANCH = 3
BR_WIDTH = 512
CONV_W = 3
LOG2E = 1.4426950408889634

LANE = 128
VMEM_LIMIT = 56 * 1024 * 1024
KV_BLOCK = 256
HG_CHUNK = 64
HG_CHUNKS_PER_STEP = 4
HG_SINGLE_REF_LIMIT = 43.0
FF_BLOCK = 1024
MLA_HEADS_PER_STEP = 8
DF_PAIRS_PER_STEP = 2

BF16 = jnp.bfloat16
F32 = jnp.float32


def _cparams(sem):
    return pltpu.CompilerParams(dimension_semantics=sem, vmem_limit_bytes=VMEM_LIMIT)


def _dot(a, b):
    return jnp.dot(a, b, preferred_element_type=F32)


def _dot_nt(a, b):
    return lax.dot_general(a, b, (((1,), (1,)), ((), ())), preferred_element_type=F32)


def _rms(x, g):
    r = lax.rsqrt(jnp.mean(x * x, axis=-1, keepdims=True) + EPS)
    return x * r * g


def _sigmoid(x):
    return 0.5 * jnp.tanh(0.5 * x) + 0.5


def _sigmoid_pair(x):
    e = jnp.exp(-jnp.abs(x))
    r = 1.0 / (1.0 + e)
    er = e * r
    pos = x >= 0.0
    return jnp.where(pos, r, er), jnp.where(pos, er, r)


def _const_spec(shape):
    nd = len(shape)
    return pl.BlockSpec(shape, lambda *_: (0,) * nd)


def _mla_qkv(h, wm_ref, gq_ref, wq_ref, gkv_ref, cq_ref, sq_ref, ck_ref, sk_ref):
    z = _dot(h, wm_ref[...])
    qn = _rms(z[:, :MLA_Q_LORA], gq_ref[...]).astype(BF16)
    q2 = _dot(qn, wq_ref[...])
    nq = MLA_HEADS * LANE
    cq = jnp.concatenate([cq_ref[...]] * MLA_HEADS, axis=1)
    sq = jnp.concatenate([sq_ref[...]] * MLA_HEADS, axis=1)
    q = (q2[:, :nq] * cq + q2[:, nq:] * sq).astype(BF16)
    ckv = _rms(z[:, MLA_Q_LORA:MLA_Q_LORA + MLA_KV_LORA], gkv_ref[...])
    o = MLA_Q_LORA + MLA_KV_LORA
    kr = z[:, o:o + LANE] * ck_ref[...] + z[:, o + LANE:o + 2 * LANE] * sk_ref[...]
    return q, ckv, kr[:, :MLA_ROPE]


def _mla_expand(ckv, krot, wk_ref, e_ref, wvt_ref):
    c = ckv.astype(BF16)
    k = (_dot(c, wk_ref[...]) + _dot(krot.astype(BF16), e_ref[...])).astype(BF16)
    return k, _dot_nt(wvt_ref[...], c).astype(BF16)


def _diff_cols(h, wd_ref):
    z = _dot(h, wd_ref[...])
    n = DF_HEADS * 2 * DF_DH
    return (z[:, :n] * (DF_SCALE * LOG2E)).astype(BF16), z[:, n:2 * n], z[:, 2 * n:]


def _proj_kernel(*refs, tm, aliased):
    (x_ref, gmix_ref, wm_ref, gq_ref, wq_ref, gkv_ref, cq_ref, sq_ref, ck_ref, sk_ref, wk_ref, e_ref, wvt_ref,
     wd_ref) = refs[:14]
    (q_ref, kc_ref, vtm_ref, dq_ref, kd_ref, vtd_ref, ckv_ref, krot_ref, dk_ref, dv_ref) = refs[14 + aliased:]
    h = _rms(x_ref[0], gmix_ref[...]).astype(BF16)
    q, ckv, krot = _mla_qkv(h, wm_ref, gq_ref, wq_ref, gkv_ref, cq_ref, sq_ref, ck_ref, sk_ref)
    k, vt = _mla_expand(ckv, krot, wk_ref, e_ref, wvt_ref)
    dq, dk, dv = _diff_cols(h, wd_ref)
    dvt = dv.T.astype(BF16)
    q_ref[0] = q
    kc_ref[0] = k
    dq_ref[0] = dq
    kd_ref[0] = dk.astype(BF16)
    for i in range(tm // KV_BLOCK):
        vtm_ref[0, i] = vt[:, i * KV_BLOCK:(i + 1) * KV_BLOCK]
        vtd_ref[0, i] = dvt[:, i * KV_BLOCK:(i + 1) * KV_BLOCK]
    ckv_ref[0, 0] = ckv
    krot_ref[0, 0] = krot
    dk_ref[0, 0] = dk
    dv_ref[0, 0] = dv


def _proj(x, gmix, wm, gq, wq, gkv, cq, sq, ck, sk, wk, e, wvt, wd, layer, depth, prev, tm):
    B, T, D = x.shape
    nd = DF_HEADS * 2 * DF_DH
    nk = tm // KV_BLOCK
    tok = lambda w: pl.BlockSpec((1, tm, w), lambda b, t: (b, t, 0))
    tab = pl.BlockSpec((tm, LANE), lambda b, t: (t, 0))
    vts = lambda r: pl.BlockSpec((1, nk, r, KV_BLOCK), lambda b, t: (b, t, 0, 0))
    st = lambda w: pl.BlockSpec((1, 1, tm, w), lambda b, t: (layer, b, t, 0))
    consts = (gmix, wm, gq, wq, gkv)
    consts2 = (wk, e, wvt, wd)
    in_specs = ([tok(D)] + [_const_spec(a.shape) for a in consts] + [tab] * 4
                + [_const_spec(a.shape) for a in consts2])
    args = (x,) + consts + (cq, sq, ck, sk) + consts2
    aliases = {}
    if prev is not None:
        in_specs += [pl.BlockSpec(memory_space=pl.ANY)] * 4
        aliases = {len(args) + i: 6 + i for i in range(4)}
        args += tuple(prev)
    widths = (MLA_KV_LORA, MLA_ROPE, nd, nd)
    outs = pl.pallas_call(
        functools.partial(_proj_kernel, tm=tm, aliased=4 if prev is not None else 0),
        grid=(B, T // tm),
        in_specs=in_specs,
        out_specs=[tok(MLA_HEADS * LANE), tok(MLA_HEADS * LANE), vts(MLA_HEADS * MLA_V), tok(nd), tok(nd), vts(nd)]
        + [st(w) for w in widths],
        out_shape=[jax.ShapeDtypeStruct((B, T, MLA_HEADS * LANE), BF16),
                   jax.ShapeDtypeStruct((B, T, MLA_HEADS * LANE), BF16),
                   jax.ShapeDtypeStruct((B, T // KV_BLOCK, MLA_HEADS * MLA_V, KV_BLOCK), BF16),
                   jax.ShapeDtypeStruct((B, T, nd), BF16),
                   jax.ShapeDtypeStruct((B, T, nd), BF16),
                   jax.ShapeDtypeStruct((B, T // KV_BLOCK, nd, KV_BLOCK), BF16)]
        + [jax.ShapeDtypeStruct((depth, B, T, w), F32) for w in widths],
        input_output_aliases=aliases,
        compiler_params=_cparams(("parallel", "parallel")),
        name="proj",
    )(*args)
    return outs[:6], outs[6:]


def _mla_proj_kernel(x_ref, gmix_ref, wm_ref, gq_ref, wq_ref, gkv_ref, cq_ref, sq_ref, ck_ref, sk_ref,
                     q_ref, ckv_ref, krot_ref):
    h = _rms(x_ref[0], gmix_ref[...]).astype(BF16)
    q_ref[0], ckv_ref[0], krot_ref[0] = _mla_qkv(h, wm_ref, gq_ref, wq_ref, gkv_ref, cq_ref, sq_ref, ck_ref,
                                                  sk_ref)


def _mla_proj(x, gmix, wm, gq, wq, gkv, cq, sq, ck, sk, tm):
    B, T, D = x.shape
    grid = (B, T // tm)
    tok = lambda w: pl.BlockSpec((1, tm, w), lambda b, t: (b, t, 0))
    tab = pl.BlockSpec((tm, LANE), lambda b, t: (t, 0))
    return pl.pallas_call(
        _mla_proj_kernel,
        grid=grid,
        in_specs=[tok(D), _const_spec(gmix.shape), _const_spec(wm.shape), _const_spec(gq.shape),
                  _const_spec(wq.shape), _const_spec(gkv.shape), tab, tab, tab, tab],
        out_specs=[tok(MLA_HEADS * LANE), tok(MLA_KV_LORA), tok(MLA_ROPE)],
        out_shape=[jax.ShapeDtypeStruct((B, T, MLA_HEADS * LANE), BF16),
                   jax.ShapeDtypeStruct((B, T, MLA_KV_LORA), F32),
                   jax.ShapeDtypeStruct((B, T, MLA_ROPE), F32)],
        compiler_params=_cparams(("parallel", "parallel")),
        name="mla_proj",
    )(x, gmix, wm, gq, wq, gkv, cq, sq, ck, sk)


def _past_new_specs(layer, n_past, width):
    return (pl.BlockSpec((1, 1, KV_BLOCK, width), lambda b, t: (layer, b, jnp.minimum(t, n_past - 1), 0)),
            pl.BlockSpec((1, KV_BLOCK, width), lambda b, t: (b, 0, 0)))


def _mla_kv_kernel(pc_ref, nc_ref, pr_ref, nr_ref, wk_ref, e_ref, wvt_ref, k_ref, vt_ref, *, n_past):
    t = pl.program_id(1)

    @pl.when(t < n_past)
    def _():
        k_ref[0], vt_ref[0, 0] = _mla_expand(pc_ref[0, 0], pr_ref[0, 0], wk_ref, e_ref, wvt_ref)

    @pl.when(t >= n_past)
    def _():
        k_ref[0], vt_ref[0, 0] = _mla_expand(nc_ref[0], nr_ref[0], wk_ref, e_ref, wvt_ref)


def _mla_kv(layer, ckv_past, krot_past, ckv_new, krot_new, wk, e, wvt):
    _, B, P, _ = ckv_past.shape
    n_past = P // KV_BLOCK
    nkb = n_past + 1
    return pl.pallas_call(
        functools.partial(_mla_kv_kernel, n_past=n_past),
        grid=(B, nkb),
        in_specs=[*_past_new_specs(layer, n_past, MLA_KV_LORA), *_past_new_specs(layer, n_past, MLA_ROPE),
                  _const_spec(wk.shape), _const_spec(e.shape), _const_spec(wvt.shape)],
        out_specs=[pl.BlockSpec((1, KV_BLOCK, MLA_HEADS * LANE), lambda b, t: (b, t, 0)),
                   pl.BlockSpec((1, 1, MLA_HEADS * MLA_V, KV_BLOCK), lambda b, t: (b, t, 0, 0))],
        out_shape=[jax.ShapeDtypeStruct((B, nkb * KV_BLOCK, MLA_HEADS * LANE), BF16),
                   jax.ShapeDtypeStruct((B, nkb, MLA_HEADS * MLA_V, KV_BLOCK), BF16)],
        compiler_params=_cparams(("parallel", "parallel")),
        name="mla_kv",
    )(ckv_past, ckv_new, krot_past, krot_new, wk, e, wvt)


def _visible_blocks(q0, tq, kvalid, nkb):
    n_full = jnp.minimum((((q0 >> 6) + 1) * CHUNK) // KV_BLOCK, kvalid // KV_BLOCK)
    last = (((q0 + tq - 1) >> 6) + 1) * CHUNK
    n_vis = jnp.minimum((last + KV_BLOCK - 1) // KV_BLOCK, nkb)
    return n_full, n_vis


def _flash_t(streams, k_ref, vt_ref, m_s, acc_s, n_full, n_vis, q0, tq, kvalid):
    ones = jnp.ones((16, KV_BLOCK), BF16)
    n = len(streams)

    def step(kbs, masked):
        ss = []
        for qm, ksl, _ in streams:
            ss.append([_dot_nt(k_ref[0, pl.ds(pl.multiple_of(kb * KV_BLOCK, KV_BLOCK), KV_BLOCK), ksl], qm)
                       for kb in kbs])
        if any(masked):
            qchunk = (q0 + lax.broadcasted_iota(jnp.int32, (KV_BLOCK, tq), 1)) >> 6
            for j, kb in enumerate(kbs):
                if masked[j]:
                    kpos = kb * KV_BLOCK + lax.broadcasted_iota(jnp.int32, (KV_BLOCK, tq), 0)
                    vis = jnp.where(kpos < kvalid, kpos >> 6, jnp.int32(2 ** 30)) <= qchunk
                    for s in ss:
                        s[j] = jnp.where(vis, s[j], NEG)
        ms = []
        for i, s in enumerate(ss):
            m = m_s[i]
            for sj in s:
                m = jnp.maximum(m, jnp.max(sj, axis=0, keepdims=True))
            ms.append(m)
        ps = [[jnp.exp2(sj - m_new).astype(BF16) for sj in s] for s, m_new in zip(ss, ms)]
        for i, ((_, _, vsl), m_new, p) in enumerate(zip(streams, ms, ps)):
            acc = jnp.exp2(m_s[i] - m_new) * acc_s[i]
            for kb, pj in zip(kbs, p):
                vt = jnp.concatenate([vt_ref[0, kb, vsl, :], ones], axis=0)
                acc = acc + _dot(vt, pj)
            acc_s[i] = acc
            m_s[i] = m_new

    for i in range(n):
        m_s[i] = jnp.full((1, tq), -jnp.inf, F32)
        acc_s[i] = jnp.zeros((MLA_V + 16, tq), F32)
    n_pairs = n_full // 2

    def pair_step(j, c):
        step([2 * j, 2 * j + 1], (False, False))
        return c

    def masked_step(kb, c):
        step([kb], (True,))
        return c

    lax.fori_loop(0, n_pairs, pair_step, 0)
    mixed = jnp.logical_and(n_full - 2 * n_pairs == 1, n_vis > n_full)

    @pl.when(mixed)
    def _():
        step([n_full - 1, n_full], (False, True))

    lax.fori_loop(jnp.where(mixed, n_full + 1, 2 * n_pairs), n_vis, masked_step, 0)
    return [acc_s[i] for i in range(n)]


def _attn_mla_kernel(q_ref, k_ref, vt_ref, o_ref, m_s, acc_s, *, tq, q_off, kvalid, nkb):
    q0 = q_off + pl.program_id(2) * tq
    n_full, n_vis = _visible_blocks(q0, tq, kvalid, nkb)
    streams = []
    for hh in range(MLA_HEADS_PER_STEP):
        lanes = slice(hh * LANE, (hh + 1) * LANE)
        streams.append((q_ref[0, :, lanes], lanes, slice(hh * MLA_V, (hh + 1) * MLA_V)))
    accs = _flash_t(streams, k_ref, vt_ref, m_s, acc_s, n_full, n_vis, q0, tq, kvalid)
    outs = [acc[:MLA_V] * (1.0 / acc[MLA_V:MLA_V + 1]) for acc in accs]
    for i in range(0, MLA_HEADS_PER_STEP, 2):
        o_ref[0, :, i * MLA_V:(i + 2) * MLA_V] = jnp.concatenate(outs[i:i + 2], axis=0).T.astype(BF16)


def _attn_mla(q, k, vt, tq, q_off, kvalid):
    B, T, _ = q.shape
    Tk = k.shape[1]
    nkb = Tk // KV_BLOCK
    hps = MLA_HEADS_PER_STEP
    kern = functools.partial(_attn_mla_kernel, tq=tq, q_off=q_off, kvalid=kvalid, nkb=nkb)
    return pl.pallas_call(
        kern,
        grid=(B, MLA_HEADS // hps, T // tq),
        in_specs=[pl.BlockSpec((1, tq, hps * LANE), lambda b, p, i: (b, i, p)),
                  pl.BlockSpec((1, Tk, hps * LANE), lambda b, p, i: (b, 0, p)),
                  pl.BlockSpec((1, nkb, hps * MLA_V, KV_BLOCK), lambda b, p, i: (b, 0, p, 0))],
        out_specs=pl.BlockSpec((1, tq, hps * MLA_V), lambda b, p, i: (b, i, p)),
        out_shape=jax.ShapeDtypeStruct((B, T, MLA_HEADS * MLA_V), BF16),
        scratch_shapes=[pltpu.VMEM((hps, 1, tq), F32), pltpu.VMEM((hps, MLA_V + 16, tq), F32)],
        compiler_params=_cparams(("parallel", "parallel", "arbitrary")),
        name="attn_mla",
    )(q, k, vt)


def _attn_diff_kernel(q_ref, k_ref, vt_ref, lam_ref, g_ref, o_ref, m_s, acc_s, *, tq, q_off, kvalid, nkb,
                      lam_init):
    q0 = q_off + pl.program_id(2) * tq
    n_full, n_vis = _visible_blocks(q0, tq, kvalid, nkb)
    lamv = lam_ref[...]
    lam = (jnp.exp(jnp.sum(lamv[0:1] * lamv[1:2], axis=-1, keepdims=True))
           - jnp.exp(jnp.sum(lamv[2:3] * lamv[3:4], axis=-1, keepdims=True)) + lam_init)
    lane = lax.broadcasted_iota(jnp.int32, (1, LANE), 1)
    streams = []
    for pr in range(DF_PAIRS_PER_STEP):
        lanes = slice(pr * LANE, (pr + 1) * LANE)
        q = q_ref[0, :, lanes]
        for hh in range(2):
            for j in range(2):
                lo = hh * 2 * DF_DH + j * DF_DH
                qm = q * jnp.where((lane >= lo) & (lane < lo + DF_DH), 1.0, 0.0).astype(BF16)
                streams.append((qm, lanes, slice((2 * pr + hh) * DF_DV, (2 * pr + hh + 1) * DF_DV)))
    accs = _flash_t(streams, k_ref, vt_ref, m_s, acc_s, n_full, n_vis, q0, tq, kvalid)
    maps = [acc[:DF_DV] * (1.0 / acc[DF_DV:DF_DV + 1]) for acc in accs]
    for pr in range(DF_PAIRS_PER_STEP):
        outs = []
        for hh in range(2):
            i = 4 * pr + 2 * hh
            o = maps[i] - lam * maps[i + 1]
            outs.append(o * lax.rsqrt(jnp.mean(o * o, axis=0, keepdims=True) + EPS))
        ot = jnp.concatenate(outs, axis=0).T
        o_ref[0, :, pr * LANE:(pr + 1) * LANE] = (ot * g_ref[...] * (1.0 - lam_init)).astype(BF16)


def _attn_diff(q, k, vt, lam_rows, g2, tq, q_off, kvalid, lam_init):
    B, T, _ = q.shape
    Tk = k.shape[1]
    nkb = Tk // KV_BLOCK
    pps = DF_PAIRS_PER_STEP
    kern = functools.partial(_attn_diff_kernel, tq=tq, q_off=q_off, kvalid=kvalid, nkb=nkb,
                             lam_init=lam_init)
    return pl.pallas_call(
        kern,
        grid=(B, DF_HEADS // (2 * pps), T // tq),
        in_specs=[pl.BlockSpec((1, tq, pps * LANE), lambda b, p, i: (b, i, p)),
                  pl.BlockSpec((1, Tk, pps * LANE), lambda b, p, i: (b, 0, p)),
                  pl.BlockSpec((1, nkb, pps * 2 * DF_DV, KV_BLOCK), lambda b, p, i: (b, 0, p, 0)),
                  _const_spec(lam_rows.shape), _const_spec(g2.shape)],
        out_specs=pl.BlockSpec((1, tq, pps * 2 * DF_DV), lambda b, p, i: (b, i, p)),
        out_shape=jax.ShapeDtypeStruct((B, T, DF_HEADS * DF_DV), BF16),
        scratch_shapes=[pltpu.VMEM((4 * pps, 1, tq), F32), pltpu.VMEM((4 * pps, DF_DV + 16, tq), F32)],
        compiler_params=_cparams(("parallel", "parallel", "arbitrary")),
        name="attn_diff",
    )(q, k, vt, lam_rows, g2)


def _diff_proj_kernel(x_ref, gmix_ref, wd_ref, q_ref, dk_ref, dv_ref):
    h = _rms(x_ref[0], gmix_ref[...]).astype(BF16)
    q_ref[0], dk_ref[0], dv_ref[0] = _diff_cols(h, wd_ref)


def _diff_proj(x, gmix, wd, tm):
    B, T, D = x.shape
    n = DF_HEADS * 2 * DF_DH
    tok = lambda w: pl.BlockSpec((1, tm, w), lambda b, t: (b, t, 0))
    return pl.pallas_call(
        _diff_proj_kernel,
        grid=(B, T // tm),
        in_specs=[tok(D), _const_spec(gmix.shape), _const_spec(wd.shape)],
        out_specs=[tok(n), tok(n), tok(n)],
        out_shape=[jax.ShapeDtypeStruct((B, T, n), BF16),
                   jax.ShapeDtypeStruct((B, T, n), F32),
                   jax.ShapeDtypeStruct((B, T, n), F32)],
        compiler_params=_cparams(("parallel", "parallel")),
        name="diff_proj",
    )(x, gmix, wd)


def _diff_kv_kernel(pk_ref, nk_ref, pv_ref, nv_ref, k_ref, vt_ref, *, n_past):
    t = pl.program_id(1)

    @pl.when(t < n_past)
    def _():
        k_ref[0] = pk_ref[0, 0].astype(BF16)
        vt_ref[0, 0] = pv_ref[0, 0].T.astype(BF16)

    @pl.when(t >= n_past)
    def _():
        k_ref[0] = nk_ref[0].astype(BF16)
        vt_ref[0, 0] = nv_ref[0].T.astype(BF16)


def _diff_kv(layer, dk_past, dv_past, dk_new, dv_new):
    _, B, P, n = dk_past.shape
    n_past = P // KV_BLOCK
    nkb = n_past + 1
    return pl.pallas_call(
        functools.partial(_diff_kv_kernel, n_past=n_past),
        grid=(B, nkb),
        in_specs=[*_past_new_specs(layer, n_past, n), *_past_new_specs(layer, n_past, n)],
        out_specs=[pl.BlockSpec((1, KV_BLOCK, n), lambda b, t: (b, t, 0)),
                   pl.BlockSpec((1, 1, n, KV_BLOCK), lambda b, t: (b, t, 0, 0))],
        out_shape=[jax.ShapeDtypeStruct((B, nkb * KV_BLOCK, n), BF16),
                   jax.ShapeDtypeStruct((B, nkb, n, KV_BLOCK), BF16)],
        compiler_params=_cparams(("parallel", "parallel")),
        name="diff_kv",
    )(dk_past, dk_new, dv_past, dv_new)


def _split2(x):
    a = x.astype(BF16)
    return a, (x - a.astype(F32)).astype(BF16)


def _level_ref_rows(b, size):
    half = size // 2
    C = b.shape[0]
    pieces = [jnp.broadcast_to(b[i * size + half - 1:i * size + half, :], (size, b.shape[1]))
              for i in range(C // size)]
    return pieces[0] if len(pieces) == 1 else jnp.concatenate(pieces, axis=0)


def _base_ref_rows(b):
    C, W = b.shape
    sub = lax.broadcasted_iota(jnp.int32, (8, W), 0)
    pieces = [jnp.where(sub < 4, jnp.broadcast_to(b[8 * i:8 * i + 1, :], (8, W)),
                        jnp.broadcast_to(b[8 * i + 4:8 * i + 5, :], (8, W))) for i in range(C // 8)]
    return jnp.concatenate(pieces, axis=0)


def _hgrn_kernel(x_ref, gmix_ref, wh_ref, lbl_ref, st0_ref, gh_ref, gavg_ref, ob_ref, st_ref, z_s, b_s,
                 *, layer, tm):
    C = min(HG_CHUNK, tm)
    W = HG_HEADS * HG_DK
    ti = pl.program_id(1)

    @pl.when(ti == 0)
    def _():
        st_ref[0] = st0_ref[0]

    h = _rms(x_ref[0], gmix_ref[...]).astype(BF16)
    z_s[...] = _dot(h, wh_ref[...])

    lg = lbl_ref[...]
    e = jnp.exp(lg - jnp.max(lg, axis=0, keepdims=True))
    sm = e / jnp.sum(e, axis=0, keepdims=True)
    lb = jnp.zeros((1, W), F32)
    for i in range(1, layer + 1):
        lb = lb + sm[i:i + 1]

    row = lax.broadcasted_iota(jnp.int32, (C, C), 0)
    col = lax.broadcasted_iota(jnp.int32, (C, C), 1)
    tril_b = col <= row
    tril = tril_b.astype(BF16)
    rsub = lax.broadcasted_iota(jnp.int32, (C, LANE), 0)
    lane = lax.broadcasted_iota(jnp.int32, (C, LANE), 1)
    head_lo = lane < HG_DK
    sizes = [s for s in (64, 32, 16, 8) if s <= C]
    blk_masks = {s: (row // s) == (col // s) for s in sizes}
    base_mask = ((row // 4) == (col // 4)) & tril_b
    diag = ((lax.broadcasted_iota(jnp.int32, (LANE, LANE), 0) // HG_DK)
            == (lax.broadcasted_iota(jnp.int32, (LANE, LANE), 1) // HG_DK))
    mid = C // 2 - 1
    n_pairs = HG_HEADS // 2

    group = min(HG_CHUNKS_PER_STEP, tm // C)

    def prep(j, spread):
        for i in range(group):
            r0 = pl.multiple_of((j * group + i) * C, C)
            hq = z_s[pl.ds(r0, C), 0:W]
            hf = z_s[pl.ds(r0, C), W:2 * W]
            hg = z_s[pl.ds(r0, C), 3 * W:4 * W]
            sig, sig_neg = _sigmoid_pair(hf)
            f = sig + lb * sig_neg
            l1, l2 = _split2(jnp.log(jnp.maximum(f, F_MIN)))
            b = _dot(tril, l1) + _dot(tril, l2)
            z_s[pl.ds(r0, C), 0:W] = hq * _sigmoid(hq)
            z_s[pl.ds(r0, C), W:2 * W] = (1.0 - lb) * sig_neg
            z_s[pl.ds(r0, C), 3 * W:4 * W] = hg * _sigmoid(hg)
            b_s[pl.ds(r0, C), :] = b
            spread = jnp.maximum(spread, jnp.maximum(b[0:1] - b[mid:mid + 1], b[mid:mid + 1] - b[C - 1:C]))
        return spread

    spread = lax.fori_loop(0, tm // (C * group), prep, jnp.zeros((1, W), F32))
    single_ref_ok = jnp.max(spread) <= HG_SINGLE_REF_LIMIT

    def heads(x):
        return (jnp.where(head_lo, x, 0.0).astype(BF16), jnp.where(head_lo, 0.0, x).astype(BF16))

    def finish(cs, bs, qs, ks, vs, a_all):
        o_all = []
        for i, c in enumerate(cs):
            b, q, k, v = bs[i], qs[i], ks[i], vs[i]
            b_end = b[C - 1:C, :]
            qe = (q * jnp.exp(b)).astype(BF16)
            ke = (k * jnp.exp(b_end - b)).astype(BF16)
            d_end = jnp.exp(b_end)
            o_pairs = []
            for p in range(n_pairs):
                sl = slice(p * LANE, (p + 1) * LANE)
                v0, v1 = heads(v[:, sl])
                a0, a1 = a_all[i][p]
                st = st_ref[0, p]
                o_p = (_dot(a0.astype(BF16), v0) + _dot(a1.astype(BF16), v1)
                       + _dot_nt(qe[:, sl], st.astype(BF16)))
                upd = _dot(v[:, sl].T.astype(BF16), ke[:, sl])
                st_ref[0, p] = st * d_end[:, sl] + jnp.where(diag, upd, 0.0)
                o_pairs.append(o_p)
            o_all.append(jnp.concatenate(o_pairs, axis=1))
        for c, o in zip(cs, o_all):
            r0 = pl.multiple_of(c * C, C)
            ms = _dot((o * o).astype(BF16), gavg_ref[...])
            ob = o * lax.rsqrt(ms + EPS) * gh_ref[...] * z_s[pl.ds(r0, C), 3 * W:4 * W]
            ob_ref[0, pl.ds(r0, C), :] = ob.astype(BF16)

    def load(cs):
        out = []
        for c in cs:
            r0 = pl.multiple_of(c * C, C)
            out.append((b_s[pl.ds(r0, C), :], z_s[pl.ds(r0, C), 0:W], z_s[pl.ds(r0, C), W:2 * W],
                        z_s[pl.ds(r0, C), 2 * W:3 * W]))
        return tuple(zip(*out))

    def fast(j, carry):
        cs = [j * group + i for i in range(group)]
        bs, qs, ks, vs = load(cs)
        a_all = []
        for b, q, k in zip(bs, qs, ks):
            ref = b[mid:mid + 1, :]
            qt = q * jnp.exp(b - ref)
            kt = (k * jnp.exp(ref - b)).astype(BF16)
            a_c = []
            for p in range(n_pairs):
                sl = slice(p * LANE, (p + 1) * LANE)
                q0, q1 = heads(qt[:, sl])
                a_c.append((jnp.where(tril_b, _dot_nt(q0, kt[:, sl]), 0.0),
                            jnp.where(tril_b, _dot_nt(q1, kt[:, sl]), 0.0)))
            a_all.append(a_c)
        finish(cs, bs, qs, ks, vs, a_all)
        return carry

    def safe(c, carry):
        bs, qs, ks, vs = load([c])
        a_c = []
        for p in range(n_pairs):
            sl = slice(p * LANE, (p + 1) * LANE)
            bp, qp, kp = bs[0][:, sl], qs[0][:, sl], ks[0][:, sl]
            a_h = [jnp.zeros((C, C), F32), jnp.zeros((C, C), F32)]
            for s in sizes:
                ref = _level_ref_rows(bp, s)
                upper = (rsub & (s - 1)) >= (s // 2)
                qt = heads(qp * jnp.exp(jnp.where(upper, bp - ref, NEG)))
                kt = (kp * jnp.exp(jnp.where(upper, NEG, ref - bp))).astype(BF16)
                for hh in range(2):
                    pr = _dot_nt(qt[hh], kt)
                    a_h[hh] = a_h[hh] + (pr if s == C else jnp.where(blk_masks[s], pr, 0.0))
            ref = _base_ref_rows(bp)
            qt = heads(qp * jnp.exp(bp - ref))
            kt = (kp * jnp.exp(ref - bp)).astype(BF16)
            for hh in range(2):
                a_h[hh] = a_h[hh] + jnp.where(base_mask, _dot_nt(qt[hh], kt), 0.0)
            a_c.append(tuple(a_h))
        finish([c], bs, qs, ks, vs, [a_c])
        return carry

    @pl.when(single_ref_ok)
    def _():
        lax.fori_loop(0, tm // (C * group), fast, 0)

    @pl.when(jnp.logical_not(single_ref_ok))
    def _():
        lax.fori_loop(0, tm // C, safe, 0)


def _hgrn(x, gmix, wh, lbl, st0, gh, gavg, layer, tm):
    B, T, D = x.shape
    W = HG_HEADS * HG_DK
    st_spec = pl.BlockSpec((1, HG_HEADS // 2, LANE, LANE), lambda b, t: (b, 0, 0, 0))
    return pl.pallas_call(
        functools.partial(_hgrn_kernel, layer=layer, tm=tm),
        grid=(B, T // tm),
        in_specs=[pl.BlockSpec((1, tm, D), lambda b, t: (b, t, 0)), _const_spec(gmix.shape),
                  _const_spec(wh.shape), _const_spec(lbl.shape), st_spec, _const_spec(gh.shape),
                  _const_spec(gavg.shape)],
        out_specs=[pl.BlockSpec((1, tm, W), lambda b, t: (b, t, 0)), st_spec],
        out_shape=[jax.ShapeDtypeStruct((B, T, W), BF16),
                   jax.ShapeDtypeStruct((B, HG_HEADS // 2, LANE, LANE), F32)],
        scratch_shapes=[pltpu.VMEM((tm, 4 * W), F32), pltpu.VMEM((tm, W), F32)],
        compiler_params=_cparams(("parallel", "arbitrary")),
        name="hgrn",
    )(x, gmix, wh, lbl, st0, gh, gavg)


def _post_kernel(x_ref, oa_ref, ob_ref, oc_ref, gmix_ref, wg_ref, wb_ref, wo_ref, gffn_ref, wup_ref,
                 cw_ref, cb_ref, wdn_ref, conv0_ref, gfin_ref, y_ref, conv_ref, carry_s, *, tm, d_ff, final):
    ti = pl.program_id(1)
    x = x_ref[0]
    D = x.shape[1]
    h = _rms(x, gmix_ref[...]).astype(BF16)
    mixed = jnp.zeros((tm, D), F32)
    for n, o_ref in enumerate((oa_ref, ob_ref, oc_ref)):
        gate = _sigmoid(_dot(h, wg_ref[:, n * D:(n + 1) * D]))
        mixed = mixed + gate * _dot(o_ref[0], wb_ref[n])
    x1 = x + _dot(mixed.astype(BF16), wo_ref[...])
    xn = _rms(x1, gffn_ref[...]).astype(BF16)

    @pl.when(ti == 0)
    def _():
        carry_s[0:CONV_W - 1, :] = conv0_ref[0]

    acc = jnp.zeros((tm, D), F32)
    for f in range(0, d_ff, FF_BLOCK):
        w = min(FF_BLOCK, d_ff - f)
        cols = slice(f, f + w)
        a = _dot(xn, wup_ref[:, cols])
        vv = _dot(xn, wup_ref[:, d_ff + f:d_ff + f + w])
        p0 = carry_s[0:1, cols]
        p1 = carry_s[1:2, cols]
        r1 = pltpu.roll(a, 1, 0)
        r2 = pltpu.roll(a, 2, 0)
        last2 = r2[0:8, :]
        rows = lax.broadcasted_iota(jnp.int32, (8, w), 0)
        a1 = jnp.concatenate([jnp.where(rows < 1, p1, r1[0:8]), r1[8:]], axis=0) if tm > 8 else \
            jnp.where(rows < 1, p1, r1)
        h2 = jnp.where(rows < 1, p0, jnp.where(rows < 2, p1, last2))
        a2 = jnp.concatenate([h2, r2[8:]], axis=0) if tm > 8 else h2
        cc = (cb_ref[:, cols] + a2 * cw_ref[0:1, cols] + a1 * cw_ref[1:2, cols] + a * cw_ref[2:3, cols])
        act = (cc * _sigmoid(cc) * vv).astype(BF16)
        acc = acc + _dot(act, wdn_ref[cols, :])
        carry_s[:, cols] = last2
        conv_ref[0, :, cols] = last2[0:CONV_W - 1, :]
    x2 = x1 + acc
    y_ref[0] = _rms(x2, gfin_ref[...]) if final else x2


def _post(x, oa, ob, oc, gmix, wg, wb, wo, gffn, wup, cw, cb, wdn, conv0, gfin, tm, final):
    B, T, D = x.shape
    d_ff = wdn.shape[0]
    tok = lambda w: pl.BlockSpec((1, tm, w), lambda b, t: (b, t, 0))
    cspec = pl.BlockSpec((1, CONV_W - 1, d_ff), lambda b, t: (b, 0, 0))
    one = lambda a: pl.BlockSpec(a.shape, lambda *_: (0,) * a.ndim, pipeline_mode=pl.Buffered(1))
    return pl.pallas_call(
        functools.partial(_post_kernel, tm=tm, d_ff=d_ff, final=final),
        grid=(B, T // tm),
        in_specs=[tok(D), tok(BR_WIDTH), tok(BR_WIDTH), tok(BR_WIDTH), one(gmix), one(wg), one(wb), one(wo),
                  one(gffn), one(wup), one(cw), one(cb), one(wdn), cspec, one(gfin)],
        out_specs=[tok(D), cspec],
        out_shape=[jax.ShapeDtypeStruct((B, T, D), F32),
                   jax.ShapeDtypeStruct((B, CONV_W - 1, d_ff), F32)],
        scratch_shapes=[pltpu.VMEM((8, d_ff), F32)],
        compiler_params=_cparams(("parallel", "arbitrary")),
        name="post",
    )(x, oa, ob, oc, gmix, wg, wb, wo, gffn, wup, cw, cb, wdn, conv0, gfin)


def _rot_half_cols(w):
    half = w.shape[-1] // 2
    return jnp.concatenate([-w[..., half:], w[..., :half]], axis=-1)


def _pad_cols(w, width):
    return jnp.pad(w, [(0, 0)] * (w.ndim - 1) + [(0, width - w.shape[-1])])


def _layer_weights(w_in, w_uq, w_ukv):
    D = w_in.shape[0]
    o_q, o_kv, o_kr = 0, MLA_Q_LORA, MLA_Q_LORA + MLA_KV_LORA
    o_h = o_kr + MLA_ROPE
    o_d = o_h + 4 * HG_HEADS * HG_DK
    o_g = o_d + 3 * DF_HEADS * DF_DV
    w_kr = w_in[:, o_kr:o_h]
    wm = jnp.concatenate([w_in[:, o_q:o_kr], _pad_cols(w_kr, LANE), _pad_cols(_rot_half_cols(w_kr), LANE)],
                         axis=1)
    wh = w_in[:, o_h:o_d]
    wd = w_in[:, o_d:o_g]
    wg = w_in[:, o_g:]
    uq = w_uq.reshape(MLA_Q_LORA, MLA_HEADS, MLA_NOPE + MLA_ROPE)
    nope, rope = uq[..., :MLA_NOPE], uq[..., MLA_NOPE:]
    zr = jnp.zeros((MLA_Q_LORA, MLA_HEADS, LANE - MLA_NOPE - MLA_ROPE), w_uq.dtype)
    wq1 = jnp.concatenate([nope, rope, zr], axis=-1).reshape(MLA_Q_LORA, MLA_HEADS * LANE)
    wq2 = jnp.concatenate([jnp.zeros_like(nope), _rot_half_cols(rope), zr], axis=-1)
    wq = jnp.concatenate([wq1, wq2.reshape(MLA_Q_LORA, MLA_HEADS * LANE)], axis=1)
    ukv = w_ukv.reshape(MLA_KV_LORA, MLA_HEADS, MLA_NOPE + MLA_V)
    wk = _pad_cols(ukv[..., :MLA_NOPE], LANE).reshape(MLA_KV_LORA, MLA_HEADS * LANE)
    wvt = ukv[..., MLA_NOPE:].reshape(MLA_KV_LORA, MLA_HEADS * MLA_V).T
    c = lambda a: a.astype(BF16)
    return c(wm), c(wh), c(wd), c(wg), c(wq), c(wk), c(wvt)


def _rope_tables(pos):
    half = MLA_ROPE // 2
    inv = ROPE_THETA ** (-jnp.arange(half, dtype=F32) / half)
    ang = pos.astype(F32)[:, None] * inv[None, :]
    cos2 = jnp.concatenate([jnp.cos(ang)] * 2, axis=1)
    sin2 = jnp.concatenate([jnp.sin(ang)] * 2, axis=1)
    T = pos.shape[0]
    c = MLA_SCALE * LOG2E
    zq = jnp.zeros((T, LANE - MLA_NOPE - MLA_ROPE), F32)
    cq = jnp.concatenate([jnp.full((T, MLA_NOPE), c, F32), c * cos2, zq], axis=1)
    sq = jnp.concatenate([jnp.zeros((T, MLA_NOPE), F32), c * sin2, zq], axis=1)
    return cq, sq, _pad_cols(cos2, LANE), _pad_cols(sin2, LANE)


def _rope_place():
    r = jnp.arange(MLA_ROPE)
    e = jnp.zeros((MLA_ROPE, MLA_HEADS, LANE), F32)
    e = e.at[r, :, MLA_NOPE + r].set(1.0)
    return e.reshape(MLA_ROPE, MLA_HEADS * LANE).astype(BF16)


def _pair_state(s):
    B = s.shape[0]
    st = jnp.swapaxes(s, -1, -2).reshape(B, HG_HEADS // 2, 2, HG_DV, HG_DK)
    z = jnp.zeros_like(st[:, :, 0])
    top = jnp.concatenate([st[:, :, 0], z], axis=-1)
    bot = jnp.concatenate([z, st[:, :, 1]], axis=-1)
    return jnp.concatenate([top, bot], axis=-2)


def _unpair_state(sp):
    a = sp[:, :, :HG_DV, :HG_DK]
    b = sp[:, :, HG_DV:, HG_DK:]
    st = jnp.stack([a, b], axis=2).reshape(sp.shape[0], HG_HEADS, HG_DV, HG_DK)
    return jnp.swapaxes(st, -1, -2)


def _pad_rows(a, n):
    return jnp.pad(a, ((0, 0), (0, n - a.shape[1]), (0, 0)))


def _layer(l, depth, x, pos, past, prev, s0, conv0, lw, prm, final):
    (norm_mix_g, mla_q_norm_g, mla_kv_norm_g, hgrn_lb_logits, hgrn_norm_g, diff_lambda, diff_norm_g,
     w_branch, w_out, norm_ffn_g, ffn_w_up, ffn_conv_w, ffn_conv_b, ffn_w_down, norm_final_g) = prm
    wm, wh, wd, wg, wq, wk, wvt = lw
    B, T, D = x.shape
    row = lambda a: a.reshape(1, -1)
    gmix = row(norm_mix_g)
    tm = min(512, T)
    place = _rope_place()

    if past is None:
        q_off, kvalid = 0, T
        tq = min(KV_BLOCK, T)
        cq, sq, ck, sk = _rope_tables(pos)
        (q_in, k_mla, vt_mla, dq_in, k_d, vt_d), new_rows = _proj(
            x, gmix, wm, row(mla_q_norm_g), wq, row(mla_kv_norm_g), cq, sq, ck, sk, wk, place, wvt, wd,
            l, depth, prev, tm)
    else:
        n_tok = B * T
        xf = x.reshape(1, n_tok, D)
        cq, sq, ck, sk = _rope_tables(jnp.tile(pos, B))
        q, ckv, krot = _mla_proj(xf, gmix, wm, row(mla_q_norm_g), wq, row(mla_kv_norm_g), cq, sq, ck, sk, n_tok)
        dq, dk, dv = _diff_proj(xf, gmix, wd, n_tok)
        q, ckv, krot, dq, dk, dv = (a.reshape(B, T, -1) for a in (q, ckv, krot, dq, dk, dv))
        P = past[0].shape[2]
        assert P % KV_BLOCK == 0 and T <= KV_BLOCK, (P, T)
        q_off, kvalid = P, P + T
        tq = LANE
        q_in, dq_in = _pad_rows(q, tq), _pad_rows(dq, tq)
        flat = lambda a: a.reshape(a.shape[:3] + (-1,))
        blk = lambda a: _pad_rows(a, KV_BLOCK)
        k_mla, vt_mla = _mla_kv(l, past[0], past[1], blk(ckv), blk(krot), wk, place, wvt)
        k_d, vt_d = _diff_kv(l, flat(past[2]), flat(past[3]), blk(dk), blk(dv))
        new_rows = (ckv, krot, dk, dv)
    o_a = _attn_mla(q_in, k_mla, vt_mla, tq, q_off, kvalid)[:, :T]
    lam_init = 0.8 - 0.6 * math.exp(-0.3 * l)
    o_c = _attn_diff(dq_in, k_d, vt_d, diff_lambda, row(jnp.tile(diff_norm_g, 2)), tq, q_off, kvalid,
                     lam_init)[:, :T]

    gavg = jnp.kron(jnp.eye(HG_HEADS, dtype=F32), jnp.full((HG_DV, HG_DV), 1.0 / HG_DV, F32)).astype(BF16)
    o_b, st = _hgrn(x, gmix, wh, hgrn_lb_logits, _pair_state(s0), row(jnp.tile(hgrn_norm_g, HG_HEADS)),
                    gavg, l, tm)

    y, conv_new = _post(x, o_a, o_b, o_c, gmix, wg, w_branch.astype(BF16), w_out.astype(BF16),
                        row(norm_ffn_g), ffn_w_up.astype(BF16), ffn_conv_w, row(ffn_conv_b),
                        ffn_w_down.astype(BF16), conv0, row(norm_final_g), tm, final)
    return y, new_rows, (_unpair_state(st), conv_new)


def kernel(x_prompt, x_sample, cache_mla_ckv, cache_mla_krope, cache_diff_k, cache_diff_v, state_hgrn,
           state_ffn_conv, norm_mix_g, w_in, mla_q_norm_g, mla_w_uq, mla_kv_norm_g, mla_w_ukv, hgrn_lb_logits,
           hgrn_norm_g, diff_lambda, diff_norm_g, w_branch, w_out, norm_ffn_g, ffn_w_up, ffn_conv_w,
           ffn_conv_b, ffn_w_down, norm_final_g):
    depth = w_in.shape[0]
    B, T, _ = x_prompt.shape
    Bs, Ts, _ = x_sample.shape
    P = cache_mla_ckv.shape[2]
    d_ff = ffn_w_down.shape[1]
    pos_p = jnp.arange(T)
    pos_s = P + jnp.arange(Ts)
    yp, ys = x_prompt, x_sample
    p_rows = None
    s_rows, p_small, s_small = [], [], []
    for l in range(depth):
        lw = _layer_weights(w_in[l], mla_w_uq[l], mla_w_ukv[l])
        prm = (norm_mix_g[l], mla_q_norm_g[l], mla_kv_norm_g[l], hgrn_lb_logits, hgrn_norm_g[l], diff_lambda[l],
               diff_norm_g[l], w_branch[l], w_out[l], norm_ffn_g[l], ffn_w_up[l], ffn_conv_w[l], ffn_conv_b[l],
               ffn_w_down[l], norm_final_g)
        final = l == depth - 1
        yp, p_rows, small = _layer(l, depth, yp, pos_p, None, p_rows, jnp.zeros((B, HG_HEADS, HG_DK, HG_DV), F32),
                                   jnp.zeros((B, CONV_W - 1, d_ff), F32), lw, prm, final)
        p_small.append(small)
        ys, rows, small = _layer(l, depth, ys, pos_s,
                                 (cache_mla_ckv, cache_mla_krope, cache_diff_k, cache_diff_v), None,
                                 state_hgrn[l], state_ffn_conv[l], lw, prm, final)
        s_rows.append(rows)
        s_small.append(small)
    stk = lambda states, i: jnp.stack([s[i] for s in states], axis=0)
    heads = lambda a, w: a.reshape(a.shape[:3] + (-1, w))
    s_rows = [stk(s_rows, i) for i in range(4)]
    out_rows = lambda r: (r[0], r[1], heads(r[2], 2 * DF_DH), heads(r[3], DF_DV))
    return ((yp, ys) + out_rows(p_rows) + (stk(p_small, 0), stk(p_small, 1))
            + out_rows(s_rows) + (stk(s_small, 0), stk(s_small, 1)))
```

```python
import functools
import math

import jax
import jax.numpy as jnp
from jax import lax
from jax.experimental import pallas as pl
from jax.experimental.pallas import tpu as pltpu

CHUNK = 64
EPS = 1e-6
NEG = -1e30
F_MIN = 1e-12
MLA_HEADS = 8
MLA_NOPE = 64
MLA_ROPE = 32
MLA_V = 64
MLA_Q_LORA = 384
MLA_KV_LORA = 256
ROPE_THETA = 10000.0
MLA_SCALE = (MLA_NOPE + MLA_ROPE) ** -0.5
HG_HEADS = 8
HG_DK = 64
HG_DV = 64
DF_HEADS = 8
DF_DH = 32
DF_DV = 2 * DF_DH
DF_SCALE = DF_DH ** -0.5
N_BRANCH = 3
BR_WIDTH = 512
CONV_W = 3
LOG2E = 1.4426950408889634

LANE = 128
VMEM_LIMIT = 56 * 1024 * 1024
KV_BLOCK = 256
HG_CHUNK = 64
HG_CHUNKS_PER_STEP = 8
HG_SINGLE_REF_LIMIT = 43.0
FF_BLOCK = 1024
MLA_HEADS_PER_STEP = 8
DF_PAIRS_PER_STEP = 2

BF16 = jnp.bfloat16
F32 = jnp.float32


def _cparams(sem):
    return pltpu.CompilerParams(dimension_semantics=sem, vmem_limit_bytes=VMEM_LIMIT)


def _dot(a, b):
    return jnp.dot(a, b, preferred_element_type=F32)


def _dot_nt(a, b):
    return lax.dot_general(a, b, (((1,), (1,)), ((), ())), preferred_element_type=F32)


def _rms(x, g):
    r = lax.rsqrt(jnp.mean(x * x, axis=-1, keepdims=True) + EPS)
    return x * r * g


def _sigmoid(x):
    return 0.5 * jnp.tanh(0.5 * x) + 0.5


def _sigmoid_pair(x):
    e = jnp.exp(-jnp.abs(x))
    r = 1.0 / (1.0 + e)
    er = e * r
    pos = x >= 0.0
    return jnp.where(pos, r, er), jnp.where(pos, er, r)


def _const_spec(shape):
    nd = len(shape)
    return pl.BlockSpec(shape, lambda *_: (0,) * nd)


def _mla_qkv(h, wm_ref, gq_ref, wq_ref, gkv_ref, cq_ref, sq_ref, ck_ref, sk_ref):
    z = _dot(h, wm_ref[...])
    qn = _rms(z[:, :MLA_Q_LORA], gq_ref[...]).astype(BF16)
    q2 = _dot(qn, wq_ref[...])
    nq = MLA_HEADS * LANE
    cq = jnp.concatenate([cq_ref[...]] * MLA_HEADS, axis=1)
    sq = jnp.concatenate([sq_ref[...]] * MLA_HEADS, axis=1)
    q = (q2[:, :nq] * cq + q2[:, nq:] * sq).astype(BF16)
    ckv = _rms(z[:, MLA_Q_LORA:MLA_Q_LORA + MLA_KV_LORA], gkv_ref[...])
    o = MLA_Q_LORA + MLA_KV_LORA
    kr = z[:, o:o + LANE] * ck_ref[...] + z[:, o + LANE:o + 2 * LANE] * sk_ref[...]
    return q, ckv, kr[:, :MLA_ROPE]


def _mla_expand(ckv, krot, wk_ref, e_ref, wvt_ref):
    c = ckv.astype(BF16)
    k = (_dot(c, wk_ref[...]) + _dot(krot.astype(BF16), e_ref[...])).astype(BF16)
    return k, _dot_nt(wvt_ref[...], c).astype(BF16)


def _diff_cols(h, wd_ref):
    z = _dot(h, wd_ref[...])
    n = DF_HEADS * 2 * DF_DH
    return (z[:, :n] * (DF_SCALE * LOG2E)).astype(BF16), z[:, n:2 * n], z[:, 2 * n:]


def _proj_kernel(*refs, tm, aliased):
    (x_ref, gmix_ref, wm_ref, gq_ref, wq_ref, gkv_ref, cq_ref, sq_ref, ck_ref, sk_ref, wk_ref, e_ref, wvt_ref,
     wd_ref) = refs[:14]
    (q_ref, kc_ref, vtm_ref, dq_ref, kd_ref, vtd_ref, ckv_ref, krot_ref, dk_ref, dv_ref) = refs[14 + aliased:]
    h = _rms(x_ref[0], gmix_ref[...]).astype(BF16)
    q, ckv, krot = _mla_qkv(h, wm_ref, gq_ref, wq_ref, gkv_ref, cq_ref, sq_ref, ck_ref, sk_ref)
    k, vt = _mla_expand(ckv, krot, wk_ref, e_ref, wvt_ref)
    dq, dk, dv = _diff_cols(h, wd_ref)
    dvt = dv.T.astype(BF16)
    q_ref[0] = q
    kc_ref[0] = k
    dq_ref[0] = dq
    kd_ref[0] = dk.astype(BF16)
    for i in range(tm // KV_BLOCK):
        vtm_ref[0, i] = vt[:, i * KV_BLOCK:(i + 1) * KV_BLOCK]
        vtd_ref[0, i] = dvt[:, i * KV_BLOCK:(i + 1) * KV_BLOCK]
    ckv_ref[0, 0] = ckv
    krot_ref[0, 0] = krot
    dk_ref[0, 0] = dk
    dv_ref[0, 0] = dv


def _proj(x, gmix, wm, gq, wq, gkv, cq, sq, ck, sk, wk, e, wvt, wd, layer, depth, prev, tm):
    B, T, D = x.shape
    nd = DF_HEADS * 2 * DF_DH
    nk = tm // KV_BLOCK
    tok = lambda w: pl.BlockSpec((1, tm, w), lambda b, t: (b, t, 0))
    tab = pl.BlockSpec((tm, LANE), lambda b, t: (t, 0))
    vts = lambda r: pl.BlockSpec((1, nk, r, KV_BLOCK), lambda b, t: (b, t, 0, 0))
    st = lambda w: pl.BlockSpec((1, 1, tm, w), lambda b, t: (layer, b, t, 0))
    consts = (gmix, wm, gq, wq, gkv)
    consts2 = (wk, e, wvt, wd)
    in_specs = ([tok(D)] + [_const_spec(a.shape) for a in consts] + [tab] * 4
                + [_const_spec(a.shape) for a in consts2])
    args = (x,) + consts + (cq, sq, ck, sk) + consts2
    aliases = {}
    if prev is not None:
        in_specs += [pl.BlockSpec(memory_space=pl.ANY)] * 4
        aliases = {len(args) + i: 6 + i for i in range(4)}
        args += tuple(prev)
    widths = (MLA_KV_LORA, MLA_ROPE, nd, nd)
    outs = pl.pallas_call(
        functools.partial(_proj_kernel, tm=tm, aliased=4 if prev is not None else 0),
        grid=(B, T // tm),
        in_specs=in_specs,
        out_specs=[tok(MLA_HEADS * LANE), tok(MLA_HEADS * LANE), vts(MLA_HEADS * MLA_V), tok(nd), tok(nd), vts(nd)]
        + [st(w) for w in widths],
        out_shape=[jax.ShapeDtypeStruct((B, T, MLA_HEADS * LANE), BF16),
                   jax.ShapeDtypeStruct((B, T, MLA_HEADS * LANE), BF16),
                   jax.ShapeDtypeStruct((B, T // KV_BLOCK, MLA_HEADS * MLA_V, KV_BLOCK), BF16),
                   jax.ShapeDtypeStruct((B, T, nd), BF16),
                   jax.ShapeDtypeStruct((B, T, nd), BF16),
                   jax.ShapeDtypeStruct((B, T // KV_BLOCK, nd, KV_BLOCK), BF16)]
        + [jax.ShapeDtypeStruct((depth, B, T, w), F32) for w in widths],
        input_output_aliases=aliases,
        compiler_params=_cparams(("parallel", "parallel")),
        name="proj",
    )(*args)
    return outs[:6], outs[6:]


def _mla_proj_kernel(x_ref, gmix_ref, wm_ref, gq_ref, wq_ref, gkv_ref, cq_ref, sq_ref, ck_ref, sk_ref,
                     q_ref, ckv_ref, krot_ref):
    h = _rms(x_ref[0], gmix_ref[...]).astype(BF16)
    q_ref[0], ckv_ref[0], krot_ref[0] = _mla_qkv(h, wm_ref, gq_ref, wq_ref, gkv_ref, cq_ref, sq_ref, ck_ref,
                                                  sk_ref)


def _mla_proj(x, gmix, wm, gq, wq, gkv, cq, sq, ck, sk, tm):
    B, T, D = x.shape
    grid = (B, T // tm)
    tok = lambda w: pl.BlockSpec((1, tm, w), lambda b, t: (b, t, 0))
    tab = pl.BlockSpec((tm, LANE), lambda b, t: (t, 0))
    return pl.pallas_call(
        _mla_proj_kernel,
        grid=grid,
        in_specs=[tok(D), _const_spec(gmix.shape), _const_spec(wm.shape), _const_spec(gq.shape),
                  _const_spec(wq.shape), _const_spec(gkv.shape), tab, tab, tab, tab],
        out_specs=[tok(MLA_HEADS * LANE), tok(MLA_KV_LORA), tok(MLA_ROPE)],
        out_shape=[jax.ShapeDtypeStruct((B, T, MLA_HEADS * LANE), BF16),
                   jax.ShapeDtypeStruct((B, T, MLA_KV_LORA), F32),
                   jax.ShapeDtypeStruct((B, T, MLA_ROPE), F32)],
        compiler_params=_cparams(("parallel", "parallel")),
        name="mla_proj",
    )(x, gmix, wm, gq, wq, gkv, cq, sq, ck, sk)


def _past_new_specs(layer, n_past, width):
    return (pl.BlockSpec((1, 1, KV_BLOCK, width), lambda b, t: (layer, b, jnp.minimum(t, n_past - 1), 0)),
            pl.BlockSpec((1, KV_BLOCK, width), lambda b, t: (b, 0, 0)))


def _mla_kv_kernel(pc_ref, nc_ref, pr_ref, nr_ref, wk_ref, e_ref, wvt_ref, k_ref, vt_ref, *, n_past):
    t = pl.program_id(1)

    @pl.when(t < n_past)
    def _():
        k_ref[0], vt_ref[0, 0] = _mla_expand(pc_ref[0, 0], pr_ref[0, 0], wk_ref, e_ref, wvt_ref)

    @pl.when(t >= n_past)
    def _():
        k_ref[0], vt_ref[0, 0] = _mla_expand(nc_ref[0], nr_ref[0], wk_ref, e_ref, wvt_ref)


def _mla_kv(layer, ckv_past, krot_past, ckv_new, krot_new, wk, e, wvt):
    _, B, P, _ = ckv_past.shape
    n_past = P // KV_BLOCK
    nkb = n_past + 1
    return pl.pallas_call(
        functools.partial(_mla_kv_kernel, n_past=n_past),
        grid=(B, nkb),
        in_specs=[*_past_new_specs(layer, n_past, MLA_KV_LORA), *_past_new_specs(layer, n_past, MLA_ROPE),
                  _const_spec(wk.shape), _const_spec(e.shape), _const_spec(wvt.shape)],
        out_specs=[pl.BlockSpec((1, KV_BLOCK, MLA_HEADS * LANE), lambda b, t: (b, t, 0)),
                   pl.BlockSpec((1, 1, MLA_HEADS * MLA_V, KV_BLOCK), lambda b, t: (b, t, 0, 0))],
        out_shape=[jax.ShapeDtypeStruct((B, nkb * KV_BLOCK, MLA_HEADS * LANE), BF16),
                   jax.ShapeDtypeStruct((B, nkb, MLA_HEADS * MLA_V, KV_BLOCK), BF16)],
        compiler_params=_cparams(("parallel", "parallel")),
        name="mla_kv",
    )(ckv_past, ckv_new, krot_past, krot_new, wk, e, wvt)


def _visible_blocks(q0, tq, kvalid, nkb):
    n_full = jnp.minimum((((q0 >> 6) + 1) * CHUNK) // KV_BLOCK, kvalid // KV_BLOCK)
    last = (((q0 + tq - 1) >> 6) + 1) * CHUNK
    n_vis = jnp.minimum((last + KV_BLOCK - 1) // KV_BLOCK, nkb)
    return n_full, n_vis


def _flash_t(streams, k_ref, vt_ref, m_s, acc_s, n_full, n_vis, q0, tq, kvalid):
    ones = jnp.ones((16, KV_BLOCK), BF16)
    n = len(streams)

    def step(kbs, masked):
        ss = []
        for qm, ksl, _ in streams:
            ss.append([_dot_nt(k_ref[0, pl.ds(pl.multiple_of(kb * KV_BLOCK, KV_BLOCK), KV_BLOCK), ksl], qm)
                       for kb in kbs])
        if any(masked):
            qchunk = (q0 + lax.broadcasted_iota(jnp.int32, (KV_BLOCK, tq), 1)) >> 6
            for j, kb in enumerate(kbs):
                if masked[j]:
                    kpos = kb * KV_BLOCK + lax.broadcasted_iota(jnp.int32, (KV_BLOCK, tq), 0)
                    vis = jnp.where(kpos < kvalid, kpos >> 6, jnp.int32(2 ** 30)) <= qchunk
                    for s in ss:
                        s[j] = jnp.where(vis, s[j], NEG)
        ms = []
        for i, s in enumerate(ss):
            m = m_s[i]
            for sj in s:
                m = jnp.maximum(m, jnp.max(sj, axis=0, keepdims=True))
            ms.append(m)
        ps = [[jnp.exp2(sj - m_new).astype(BF16) for sj in s] for s, m_new in zip(ss, ms)]
        for i, ((_, _, vsl), m_new, p) in enumerate(zip(streams, ms, ps)):
            acc = jnp.exp2(m_s[i] - m_new) * acc_s[i]
            for kb, pj in zip(kbs, p):
                vt = jnp.concatenate([vt_ref[0, kb, vsl, :], ones], axis=0)
                acc = acc + _dot(vt, pj)
            acc_s[i] = acc
            m_s[i] = m_new

    for i in range(n):
        m_s[i] = jnp.full((1, tq), -jnp.inf, F32)
        acc_s[i] = jnp.zeros((MLA_V + 16, tq), F32)
    n_pairs = n_full // 2

    def pair_step(j, c):
        step([2 * j, 2 * j + 1], (False, False))
        return c

    def masked_step(kb, c):
        step([kb], (True,))
        return c

    lax.fori_loop(0, n_pairs, pair_step, 0)
    mixed = jnp.logical_and(n_full - 2 * n_pairs == 1, n_vis > n_full)

    @pl.when(mixed)
    def _():
        step([n_full - 1, n_full], (False, True))

    lax.fori_loop(jnp.where(mixed, n_full + 1, 2 * n_pairs), n_vis, masked_step, 0)
    return [acc_s[i] for i in range(n)]


def _attn_mla_kernel(q_ref, k_ref, vt_ref, o_ref, m_s, acc_s, *, tq, q_off, kvalid, nkb):
    q0 = q_off + pl.program_id(2) * tq
    n_full, n_vis = _visible_blocks(q0, tq, kvalid, nkb)
    streams = []
    for hh in range(MLA_HEADS_PER_STEP):
        lanes = slice(hh * LANE, (hh + 1) * LANE)
        streams.append((q_ref[0, :, lanes], lanes, slice(hh * MLA_V, (hh + 1) * MLA_V)))
    accs = _flash_t(streams, k_ref, vt_ref, m_s, acc_s, n_full, n_vis, q0, tq, kvalid)
    outs = [acc[:MLA_V] * (1.0 / acc[MLA_V:MLA_V + 1]) for acc in accs]
    for i in range(0, MLA_HEADS_PER_STEP, 2):
        o_ref[0, :, i * MLA_V:(i + 2) * MLA_V] = jnp.concatenate(outs[i:i + 2], axis=0).T.astype(BF16)


def _attn_mla(q, k, vt, tq, q_off, kvalid):
    B, T, _ = q.shape
    Tk = k.shape[1]
    nkb = Tk // KV_BLOCK
    hps = MLA_HEADS_PER_STEP
    kern = functools.partial(_attn_mla_kernel, tq=tq, q_off=q_off, kvalid=kvalid, nkb=nkb)
    return pl.pallas_call(
        kern,
        grid=(B, MLA_HEADS // hps, T // tq),
        in_specs=[pl.BlockSpec((1, tq, hps * LANE), lambda b, p, i: (b, i, p)),
                  pl.BlockSpec((1, Tk, hps * LANE), lambda b, p, i: (b, 0, p)),
                  pl.BlockSpec((1, nkb, hps * MLA_V, KV_BLOCK), lambda b, p, i: (b, 0, p, 0))],
        out_specs=pl.BlockSpec((1, tq, hps * MLA_V), lambda b, p, i: (b, i, p)),
        out_shape=jax.ShapeDtypeStruct((B, T, MLA_HEADS * MLA_V), BF16),
        scratch_shapes=[pltpu.VMEM((hps, 1, tq), F32), pltpu.VMEM((hps, MLA_V + 16, tq), F32)],
        compiler_params=_cparams(("parallel", "parallel", "arbitrary")),
        name="attn_mla",
    )(q, k, vt)


def _attn_diff_kernel(q_ref, k_ref, vt_ref, lam_ref, g_ref, o_ref, m_s, acc_s, *, tq, q_off, kvalid, nkb,
                      lam_init):
    q0 = q_off + pl.program_id(2) * tq
    n_full, n_vis = _visible_blocks(q0, tq, kvalid, nkb)
    lamv = lam_ref[...]
    lam = (jnp.exp(jnp.sum(lamv[0:1] * lamv[1:2], axis=-1, keepdims=True))
           - jnp.exp(jnp.sum(lamv[2:3] * lamv[3:4], axis=-1, keepdims=True)) + lam_init)
    lane = lax.broadcasted_iota(jnp.int32, (1, LANE), 1)
    streams = []
    for pr in range(DF_PAIRS_PER_STEP):
        lanes = slice(pr * LANE, (pr + 1) * LANE)
        q = q_ref[0, :, lanes]
        for hh in range(2):
            for j in range(2):
                lo = hh * 2 * DF_DH + j * DF_DH
                qm = q * jnp.where((lane >= lo) & (lane < lo + DF_DH), 1.0, 0.0).astype(BF16)
                streams.append((qm, lanes, slice((2 * pr + hh) * DF_DV, (2 * pr + hh + 1) * DF_DV)))
    accs = _flash_t(streams, k_ref, vt_ref, m_s, acc_s, n_full, n_vis, q0, tq, kvalid)
    maps = [acc[:DF_DV] * (1.0 / acc[DF_DV:DF_DV + 1]) for acc in accs]
    for pr in range(DF_PAIRS_PER_STEP):
        outs = []
        for hh in range(2):
            i = 4 * pr + 2 * hh
            o = maps[i] - lam * maps[i + 1]
            outs.append(o * lax.rsqrt(jnp.mean(o * o, axis=0, keepdims=True) + EPS))
        ot = jnp.concatenate(outs, axis=0).T
        o_ref[0, :, pr * LANE:(pr + 1) * LANE] = (ot * g_ref[...] * (1.0 - lam_init)).astype(BF16)


def _attn_diff(q, k, vt, lam_rows, g2, tq, q_off, kvalid, lam_init):
    B, T, _ = q.shape
    Tk = k.shape[1]
    nkb = Tk // KV_BLOCK
    pps = DF_PAIRS_PER_STEP
    kern = functools.partial(_attn_diff_kernel, tq=tq, q_off=q_off, kvalid=kvalid, nkb=nkb,
                             lam_init=lam_init)
    return pl.pallas_call(
        kern,
        grid=(B, DF_HEADS // (2 * pps), T // tq),
        in_specs=[pl.BlockSpec((1, tq, pps * LANE), lambda b, p, i: (b, i, p)),
                  pl.BlockSpec((1, Tk, pps * LANE), lambda b, p, i: (b, 0, p)),
                  pl.BlockSpec((1, nkb, pps * 2 * DF_DV, KV_BLOCK), lambda b, p, i: (b, 0, p, 0)),
                  _const_spec(lam_rows.shape), _const_spec(g2.shape)],
        out_specs=pl.BlockSpec((1, tq, pps * 2 * DF_DV), lambda b, p, i: (b, i, p)),
        out_shape=jax.ShapeDtypeStruct((B, T, DF_HEADS * DF_DV), BF16),
        scratch_shapes=[pltpu.VMEM((4 * pps, 1, tq), F32), pltpu.VMEM((4 * pps, DF_DV + 16, tq), F32)],
        compiler_params=_cparams(("parallel", "parallel", "arbitrary")),
        name="attn_diff",
    )(q, k, vt, lam_rows, g2)


def _diff_proj_kernel(x_ref, gmix_ref, wd_ref, q_ref, dk_ref, dv_ref):
    h = _rms(x_ref[0], gmix_ref[...]).astype(BF16)
    q_ref[0], dk_ref[0], dv_ref[0] = _diff_cols(h, wd_ref)


def _diff_proj(x, gmix, wd, tm):
    B, T, D = x.shape
    n = DF_HEADS * 2 * DF_DH
    tok = lambda w: pl.BlockSpec((1, tm, w), lambda b, t: (b, t, 0))
    return pl.pallas_call(
        _diff_proj_kernel,
        grid=(B, T // tm),
        in_specs=[tok(D), _const_spec(gmix.shape), _const_spec(wd.shape)],
        out_specs=[tok(n), tok(n), tok(n)],
        out_shape=[jax.ShapeDtypeStruct((B, T, n), BF16),
                   jax.ShapeDtypeStruct((B, T, n), F32),
                   jax.ShapeDtypeStruct((B, T, n), F32)],
        compiler_params=_cparams(("parallel", "parallel")),
        name="diff_proj",
    )(x, gmix, wd)


def _diff_kv_kernel(pk_ref, nk_ref, pv_ref, nv_ref, k_ref, vt_ref, *, n_past):
    t = pl.program_id(1)

    @pl.when(t < n_past)
    def _():
        k_ref[0] = pk_ref[0, 0].astype(BF16)
        vt_ref[0, 0] = pv_ref[0, 0].T.astype(BF16)

    @pl.when(t >= n_past)
    def _():
        k_ref[0] = nk_ref[0].astype(BF16)
        vt_ref[0, 0] = nv_ref[0].T.astype(BF16)


def _diff_kv(layer, dk_past, dv_past, dk_new, dv_new):
    _, B, P, n = dk_past.shape
    n_past = P // KV_BLOCK
    nkb = n_past + 1
    return pl.pallas_call(
        functools.partial(_diff_kv_kernel, n_past=n_past),
        grid=(B, nkb),
        in_specs=[*_past_new_specs(layer, n_past, n), *_past_new_specs(layer, n_past, n)],
        out_specs=[pl.BlockSpec((1, KV_BLOCK, n), lambda b, t: (b, t, 0)),
                   pl.BlockSpec((1, 1, n, KV_BLOCK), lambda b, t: (b, t, 0, 0))],
        out_shape=[jax.ShapeDtypeStruct((B, nkb * KV_BLOCK, n), BF16),
                   jax.ShapeDtypeStruct((B, nkb, n, KV_BLOCK), BF16)],
        compiler_params=_cparams(("parallel", "parallel")),
        name="diff_kv",
    )(dk_past, dk_new, dv_past, dv_new)


def _split2(x):
    a = x.astype(BF16)
    return a, (x - a.astype(F32)).astype(BF16)


def _level_ref_rows(b, size):
    half = size // 2
    C = b.shape[0]
    pieces = [jnp.broadcast_to(b[i * size + half - 1:i * size + half, :], (size, b.shape[1]))
              for i in range(C // size)]
    return pieces[0] if len(pieces) == 1 else jnp.concatenate(pieces, axis=0)


def _base_ref_rows(b):
    C, W = b.shape
    sub = lax.broadcasted_iota(jnp.int32, (8, W), 0)
    pieces = [jnp.where(sub < 4, jnp.broadcast_to(b[8 * i:8 * i + 1, :], (8, W)),
                        jnp.broadcast_to(b[8 * i + 4:8 * i + 5, :], (8, W))) for i in range(C // 8)]
    return jnp.concatenate(pieces, axis=0)


def _hgrn_kernel(x_ref, gmix_ref, wh_ref, lbl_ref, st0_ref, gh_ref, gavg_ref, ob_ref, st_ref, z_s, b_s,
                 *, layer, tm):
    C = min(HG_CHUNK, tm)
    W = HG_HEADS * HG_DK
    ti = pl.program_id(1)

    @pl.when(ti == 0)
    def _():
        st_ref[0] = st0_ref[0]

    h = _rms(x_ref[0], gmix_ref[...]).astype(BF16)
    z_s[...] = _dot(h, wh_ref[...])

    lg = lbl_ref[...]
    e = jnp.exp(lg - jnp.max(lg, axis=0, keepdims=True))
    sm = e / jnp.sum(e, axis=0, keepdims=True)
    lb = jnp.zeros((1, W), F32)
    for i in range(1, layer + 1):
        lb = lb + sm[i:i + 1]

    row = lax.broadcasted_iota(jnp.int32, (C, C), 0)
    col = lax.broadcasted_iota(jnp.int32, (C, C), 1)
    tril_b = col <= row
    tril = tril_b.astype(BF16)
    rsub = lax.broadcasted_iota(jnp.int32, (C, LANE), 0)
    lane = lax.broadcasted_iota(jnp.int32, (C, LANE), 1)
    head_lo = lane < HG_DK
    sizes = [s for s in (64, 32, 16, 8) if s <= C]
    blk_masks = {s: (row // s) == (col // s) for s in sizes}
    base_mask = ((row // 4) == (col // 4)) & tril_b
    diag = ((lax.broadcasted_iota(jnp.int32, (LANE, LANE), 0) // HG_DK)
            == (lax.broadcasted_iota(jnp.int32, (LANE, LANE), 1) // HG_DK))
    mid = C // 2 - 1
    n_pairs = HG_HEADS // 2

    group = min(HG_CHUNKS_PER_STEP, tm // C)

    def prep(j, spread):
        for i in range(group):
            r0 = pl.multiple_of((j * group + i) * C, C)
            hq = z_s[pl.ds(r0, C), 0:W]
            hf = z_s[pl.ds(r0, C), W:2 * W]
            hg = z_s[pl.ds(r0, C), 3 * W:4 * W]
            sig, sig_neg = _sigmoid_pair(hf)
            f = sig + lb * sig_neg
            l1, l2 = _split2(jnp.log(jnp.maximum(f, F_MIN)))
            b = _dot(tril, l1) + _dot(tril, l2)
            z_s[pl.ds(r0, C), 0:W] = hq * _sigmoid(hq)
            z_s[pl.ds(r0, C), W:2 * W] = (1.0 - lb) * sig_neg
            z_s[pl.ds(r0, C), 3 * W:4 * W] = hg * _sigmoid(hg)
            b_s[pl.ds(r0, C), :] = b
            spread = jnp.maximum(spread, jnp.maximum(b[0:1] - b[mid:mid + 1], b[mid:mid + 1] - b[C - 1:C]))
        return spread

    spread = lax.fori_loop(0, tm // (C * group), prep, jnp.zeros((1, W), F32))
    single_ref_ok = jnp.max(spread) <= HG_SINGLE_REF_LIMIT

    def heads(x):
        return (jnp.where(head_lo, x, 0.0).astype(BF16), jnp.where(head_lo, 0.0, x).astype(BF16))

    def finish(cs, bs, qs, ks, vs, a_all):
        o_all = []
        for i, c in enumerate(cs):
            b, q, k, v = bs[i], qs[i], ks[i], vs[i]
            b_end = b[C - 1:C, :]
            qe = (q * jnp.exp(b)).astype(BF16)
            ke = (k * jnp.exp(b_end - b)).astype(BF16)
            d_end = jnp.exp(b_end)
            o_pairs = []
            for p in range(n_pairs):
                sl = slice(p * LANE, (p + 1) * LANE)
                v0, v1 = heads(v[:, sl])
                a0, a1 = a_all[i][p]
                st = st_ref[0, p]
                o_p = (_dot(a0.astype(BF16), v0) + _dot(a1.astype(BF16), v1)
                       + _dot_nt(qe[:, sl], st.astype(BF16)))
                upd = _dot(v[:, sl].T.astype(BF16), ke[:, sl])
                st_ref[0, p] = st * d_end[:, sl] + jnp.where(diag, upd, 0.0)
                o_pairs.append(o_p)
            o_all.append(jnp.concatenate(o_pairs, axis=1))
        for c, o in zip(cs, o_all):
            r0 = pl.multiple_of(c * C, C)
            ms = _dot((o * o).astype(BF16), gavg_ref[...])
            ob = o * lax.rsqrt(ms + EPS) * gh_ref[...] * z_s[pl.ds(r0, C), 3 * W:4 * W]
            ob_ref[0, pl.ds(r0, C), :] = ob.astype(BF16)

    def load(cs):
        out = []
        for c in cs:
            r0 = pl.multiple_of(c * C, C)
            out.append((b_s[pl.ds(r0, C), :], z_s[pl.ds(r0, C), 0:W], z_s[pl.ds(r0, C), W:2 * W],
                        z_s[pl.ds(r0, C), 2 * W:3 * W]))
        return tuple(zip(*out))

    def fast(j, carry):
        cs = [j * group + i for i in range(group)]
        bs, qs, ks, vs = load(cs)
        a_all = []
        for b, q, k in zip(bs, qs, ks):
            ref = b[mid:mid + 1, :]
            qt = q * jnp.exp(b - ref)
            kt = (k * jnp.exp(ref - b)).astype(BF16)
            a_c = []
            for p in range(n_pairs):
                sl = slice(p * LANE, (p + 1) * LANE)
                q0, q1 = heads(qt[:, sl])
                a_c.append((jnp.where(tril_b, _dot_nt(q0, kt[:, sl]), 0.0),
                            jnp.where(tril_b, _dot_nt(q1, kt[:, sl]), 0.0)))
            a_all.append(a_c)
        finish(cs, bs, qs, ks, vs, a_all)
        return carry

    def safe(c, carry):
        bs, qs, ks, vs = load([c])
        a_c = []
        for p in range(n_pairs):
            sl = slice(p * LANE, (p + 1) * LANE)
            bp, qp, kp = bs[0][:, sl], qs[0][:, sl], ks[0][:, sl]
            a_h = [jnp.zeros((C, C), F32), jnp.zeros((C, C), F32)]
            for s in sizes:
                ref = _level_ref_rows(bp, s)
                upper = (rsub & (s - 1)) >= (s // 2)
                qt = heads(qp * jnp.exp(jnp.where(upper, bp - ref, NEG)))
                kt = (kp * jnp.exp(jnp.where(upper, NEG, ref - bp))).astype(BF16)
                for hh in range(2):
                    pr = _dot_nt(qt[hh], kt)
                    a_h[hh] = a_h[hh] + (pr if s == C else jnp.where(blk_masks[s], pr, 0.0))
            ref = _base_ref_rows(bp)
            qt = heads(qp * jnp.exp(bp - ref))
            kt = (kp * jnp.exp(ref - bp)).astype(BF16)
            for hh in range(2):
                a_h[hh] = a_h[hh] + jnp.where(base_mask, _dot_nt(qt[hh], kt), 0.0)
            a_c.append(tuple(a_h))
        finish([c], bs, qs, ks, vs, [a_c])
        return carry

    @pl.when(single_ref_ok)
    def _():
        lax.fori_loop(0, tm // (C * group), fast, 0)

    @pl.when(jnp.logical_not(single_ref_ok))
    def _():
        lax.fori_loop(0, tm // C, safe, 0)


def _hgrn(x, gmix, wh, lbl, st0, gh, gavg, layer, tm):
    B, T, D = x.shape
    W = HG_HEADS * HG_DK
    st_spec = pl.BlockSpec((1, HG_HEADS // 2, LANE, LANE), lambda b, t: (b, 0, 0, 0))
    return pl.pallas_call(
        functools.partial(_hgrn_kernel, layer=layer, tm=tm),
        grid=(B, T // tm),
        in_specs=[pl.BlockSpec((1, tm, D), lambda b, t: (b, t, 0)), _const_spec(gmix.shape),
                  _const_spec(wh.shape), _const_spec(lbl.shape), st_spec, _const_spec(gh.shape),
                  _const_spec(gavg.shape)],
        out_specs=[pl.BlockSpec((1, tm, W), lambda b, t: (b, t, 0)), st_spec],
        out_shape=[jax.ShapeDtypeStruct((B, T, W), BF16),
                   jax.ShapeDtypeStruct((B, HG_HEADS // 2, LANE, LANE), F32)],
        scratch_shapes=[pltpu.VMEM((tm, 4 * W), F32), pltpu.VMEM((tm, W), F32)],
        compiler_params=_cparams(("parallel", "arbitrary")),
        name="hgrn",
    )(x, gmix, wh, lbl, st0, gh, gavg)


def _post_kernel(x_ref, oa_ref, ob_ref, oc_ref, gmix_ref, wg_ref, wb_ref, wo_ref, gffn_ref, wup_ref,
                 cw_ref, cb_ref, wdn_ref, conv0_ref, gfin_ref, y_ref, conv_ref, carry_s, *, tm, d_ff, final):
    ti = pl.program_id(1)
    x = x_ref[0]
    D = x.shape[1]
    h = _rms(x, gmix_ref[...]).astype(BF16)
    mixed = jnp.zeros((tm, D), F32)
    for n, o_ref in enumerate((oa_ref, ob_ref, oc_ref)):
        gate = _sigmoid(_dot(h, wg_ref[:, n * D:(n + 1) * D]))
        mixed = mixed + gate * _dot(o_ref[0], wb_ref[n])
    x1 = x + _dot(mixed.astype(BF16), wo_ref[...])
    xn = _rms(x1, gffn_ref[...]).astype(BF16)

    @pl.when(ti == 0)
    def _():
        carry_s[0:CONV_W - 1, :] = conv0_ref[0]

    acc = jnp.zeros((tm, D), F32)
    for f in range(0, d_ff, FF_BLOCK):
        w = min(FF_BLOCK, d_ff - f)
        cols = slice(f, f + w)
        a = _dot(xn, wup_ref[:, cols])
        vv = _dot(xn, wup_ref[:, d_ff + f:d_ff + f + w])
        p0 = carry_s[0:1, cols]
        p1 = carry_s[1:2, cols]
        r1 = pltpu.roll(a, 1, 0)
        r2 = pltpu.roll(a, 2, 0)
        last2 = r2[0:8, :]
        rows = lax.broadcasted_iota(jnp.int32, (8, w), 0)
        a1 = jnp.concatenate([jnp.where(rows < 1, p1, r1[0:8]), r1[8:]], axis=0) if tm > 8 else \
            jnp.where(rows < 1, p1, r1)
        h2 = jnp.where(rows < 1, p0, jnp.where(rows < 2, p1, last2))
        a2 = jnp.concatenate([h2, r2[8:]], axis=0) if tm > 8 else h2
        cc = (cb_ref[:, cols] + a2 * cw_ref[0:1, cols] + a1 * cw_ref[1:2, cols] + a * cw_ref[2:3, cols])
        act = (cc * _sigmoid(cc) * vv).astype(BF16)
        acc = acc + _dot(act, wdn_ref[cols, :])
        carry_s[:, cols] = last2
        conv_ref[0, :, cols] = last2[0:CONV_W - 1, :]
    x2 = x1 + acc
    y_ref[0] = _rms(x2, gfin_ref[...]) if final else x2


def _post(x, oa, ob, oc, gmix, wg, wb, wo, gffn, wup, cw, cb, wdn, conv0, gfin, tm, final):
    B, T, D = x.shape
    d_ff = wdn.shape[0]
    tok = lambda w: pl.BlockSpec((1, tm, w), lambda b, t: (b, t, 0))
    cspec = pl.BlockSpec((1, CONV_W - 1, d_ff), lambda b, t: (b, 0, 0))
    one = lambda a: pl.BlockSpec(a.shape, lambda *_: (0,) * a.ndim, pipeline_mode=pl.Buffered(1))
    return pl.pallas_call(
        functools.partial(_post_kernel, tm=tm, d_ff=d_ff, final=final),
        grid=(B, T // tm),
        in_specs=[tok(D), tok(BR_WIDTH), tok(BR_WIDTH), tok(BR_WIDTH), one(gmix), one(wg), one(wb), one(wo),
                  one(gffn), one(wup), one(cw), one(cb), one(wdn), cspec, one(gfin)],
        out_specs=[tok(D), cspec],
        out_shape=[jax.ShapeDtypeStruct((B, T, D), F32),
                   jax.ShapeDtypeStruct((B, CONV_W - 1, d_ff), F32)],
        scratch_shapes=[pltpu.VMEM((8, d_ff), F32)],
        compiler_params=_cparams(("parallel", "arbitrary")),
        name="post",
    )(x, oa, ob, oc, gmix, wg, wb, wo, gffn, wup, cw, cb, wdn, conv0, gfin)


def _rot_half_cols(w):
    half = w.shape[-1] // 2
    return jnp.concatenate([-w[..., half:], w[..., :half]], axis=-1)


def _pad_cols(w, width):
    return jnp.pad(w, [(0, 0)] * (w.ndim - 1) + [(0, width - w.shape[-1])])


def _layer_weights(w_in, w_uq, w_ukv):
    D = w_in.shape[0]
    o_q, o_kv, o_kr = 0, MLA_Q_LORA, MLA_Q_LORA + MLA_KV_LORA
    o_h = o_kr + MLA_ROPE
    o_d = o_h + 4 * HG_HEADS * HG_DK
    o_g = o_d + 3 * DF_HEADS * DF_DV
    w_kr = w_in[:, o_kr:o_h]
    wm = jnp.concatenate([w_in[:, o_q:o_kr], _pad_cols(w_kr, LANE), _pad_cols(_rot_half_cols(w_kr), LANE)],
                         axis=1)
    wh = w_in[:, o_h:o_d]
    wd = w_in[:, o_d:o_g]
    wg = w_in[:, o_g:]
    uq = w_uq.reshape(MLA_Q_LORA, MLA_HEADS, MLA_NOPE + MLA_ROPE)
    nope, rope = uq[..., :MLA_NOPE], uq[..., MLA_NOPE:]
    zr = jnp.zeros((MLA_Q_LORA, MLA_HEADS, LANE - MLA_NOPE - MLA_ROPE), w_uq.dtype)
    wq1 = jnp.concatenate([nope, rope, zr], axis=-1).reshape(MLA_Q_LORA, MLA_HEADS * LANE)
    wq2 = jnp.concatenate([jnp.zeros_like(nope), _rot_half_cols(rope), zr], axis=-1)
    wq = jnp.concatenate([wq1, wq2.reshape(MLA_Q_LORA, MLA_HEADS * LANE)], axis=1)
    ukv = w_ukv.reshape(MLA_KV_LORA, MLA_HEADS, MLA_NOPE + MLA_V)
    wk = _pad_cols(ukv[..., :MLA_NOPE], LANE).reshape(MLA_KV_LORA, MLA_HEADS * LANE)
    wvt = ukv[..., MLA_NOPE:].reshape(MLA_KV_LORA, MLA_HEADS * MLA_V).T
    c = lambda a: a.astype(BF16)
    return c(wm), c(wh), c(wd), c(wg), c(wq), c(wk), c(wvt)


def _rope_tables(pos):
    half = MLA_ROPE // 2
    inv = ROPE_THETA ** (-jnp.arange(half, dtype=F32) / half)
    ang = pos.astype(F32)[:, None] * inv[None, :]
    cos2 = jnp.concatenate([jnp.cos(ang)] * 2, axis=1)
    sin2 = jnp.concatenate([jnp.sin(ang)] * 2, axis=1)
    T = pos.shape[0]
    c = MLA_SCALE * LOG2E
    zq = jnp.zeros((T, LANE - MLA_NOPE - MLA_ROPE), F32)
    cq = jnp.concatenate([jnp.full((T, MLA_NOPE), c, F32), c * cos2, zq], axis=1)
    sq = jnp.concatenate([jnp.zeros((T, MLA_NOPE), F32), c * sin2, zq], axis=1)
    return cq, sq, _pad_cols(cos2, LANE), _pad_cols(sin2, LANE)


def _rope_place():
    r = jnp.arange(MLA_ROPE)
    e = jnp.zeros((MLA_ROPE, MLA_HEADS, LANE), F32)
    e = e.at[r, :, MLA_NOPE + r].set(1.0)
    return e.reshape(MLA_ROPE, MLA_HEADS * LANE).astype(BF16)


def _pair_state(s):
    B = s.shape[0]
    st = jnp.swapaxes(s, -1, -2).reshape(B, HG_HEADS // 2, 2, HG_DV, HG_DK)
    z = jnp.zeros_like(st[:, :, 0])
    top = jnp.concatenate([st[:, :, 0], z], axis=-1)
    bot = jnp.concatenate([z, st[:, :, 1]], axis=-1)
    return jnp.concatenate([top, bot], axis=-2)


def _unpair_state(sp):
    a = sp[:, :, :HG_DV, :HG_DK]
    b = sp[:, :, HG_DV:, HG_DK:]
    st = jnp.stack([a, b], axis=2).reshape(sp.shape[0], HG_HEADS, HG_DV, HG_DK)
    return jnp.swapaxes(st, -1, -2)


def _pad_rows(a, n):
    return jnp.pad(a, ((0, 0), (0, n - a.shape[1]), (0, 0)))


def _layer(l, depth, x, pos, past, prev, s0, conv0, lw, prm, final):
    (norm_mix_g, mla_q_norm_g, mla_kv_norm_g, hgrn_lb_logits, hgrn_norm_g, diff_lambda, diff_norm_g,
     w_branch, w_out, norm_ffn_g, ffn_w_up, ffn_conv_w, ffn_conv_b, ffn_w_down, norm_final_g) = prm
    wm, wh, wd, wg, wq, wk, wvt = lw
    B, T, D = x.shape
    row = lambda a: a.reshape(1, -1)
    gmix = row(norm_mix_g)
    tm = min(512, T)
    place = _rope_place()

    if past is None:
        q_off, kvalid = 0, T
        tq = min(KV_BLOCK, T)
        cq, sq, ck, sk = _rope_tables(pos)
        (q_in, k_mla, vt_mla, dq_in, k_d, vt_d), new_rows = _proj(
            x, gmix, wm, row(mla_q_norm_g), wq, row(mla_kv_norm_g), cq, sq, ck, sk, wk, place, wvt, wd,
            l, depth, prev, tm)
    else:
        n_tok = B * T
        xf = x.reshape(1, n_tok, D)
        cq, sq, ck, sk = _rope_tables(jnp.tile(pos, B))
        q, ckv, krot = _mla_proj(xf, gmix, wm, row(mla_q_norm_g), wq, row(mla_kv_norm_g), cq, sq, ck, sk, n_tok)
        dq, dk, dv = _diff_proj(xf, gmix, wd, n_tok)
        q, ckv, krot, dq, dk, dv = (a.reshape(B, T, -1) for a in (q, ckv, krot, dq, dk, dv))
        P = past[0].shape[2]
        assert P % KV_BLOCK == 0 and T <= KV_BLOCK, (P, T)
        q_off, kvalid = P, P + T
        tq = LANE
        q_in, dq_in = _pad_rows(q, tq), _pad_rows(dq, tq)
        flat = lambda a: a.reshape(a.shape[:3] + (-1,))
        blk = lambda a: _pad_rows(a, KV_BLOCK)
        k_mla, vt_mla = _mla_kv(l, past[0], past[1], blk(ckv), blk(krot), wk, place, wvt)
        k_d, vt_d = _diff_kv(l, flat(past[2]), flat(past[3]), blk(dk), blk(dv))
        new_rows = (ckv, krot, dk, dv)
    o_a = _attn_mla(q_in, k_mla, vt_mla, tq, q_off, kvalid)[:, :T]
    lam_init = 0.8 - 0.6 * math.exp(-0.3 * l)
    o_c = _attn_diff(dq_in, k_d, vt_d, diff_lambda, row(jnp.tile(diff_norm_g, 2)), tq, q_off, kvalid,
                     lam_init)[:, :T]

    gavg = jnp.kron(jnp.eye(HG_HEADS, dtype=F32), jnp.full((HG_DV, HG_DV), 1.0 / HG_DV, F32)).astype(BF16)
    o_b, st = _hgrn(x, gmix, wh, hgrn_lb_logits, _pair_state(s0), row(jnp.tile(hgrn_norm_g, HG_HEADS)),
                    gavg, l, tm)

    y, conv_new = _post(x, o_a, o_b, o_c, gmix, wg, w_branch.astype(BF16), w_out.astype(BF16),
                        row(norm_ffn_g), ffn_w_up.astype(BF16), ffn_conv_w, row(ffn_conv_b),
                        ffn_w_down.astype(BF16), conv0, row(norm_final_g), tm, final)
    return y, new_rows, (_unpair_state(st), conv_new)


def kernel(x_prompt, x_sample, cache_mla_ckv, cache_mla_krope, cache_diff_k, cache_diff_v, state_hgrn,
           state_ffn_conv, norm_mix_g, w_in, mla_q_norm_g, mla_w_uq, mla_kv_norm_g, mla_w_ukv, hgrn_lb_logits,
           hgrn_norm_g, diff_lambda, diff_norm_g, w_branch, w_out, norm_ffn_g, ffn_w_up, ffn_conv_w,
           ffn_conv_b, ffn_w_down, norm_final_g):
    depth = w_in.shape[0]
    B, T, _ = x_prompt.shape
    Bs, Ts, _ = x_sample.shape
    P = cache_mla_ckv.shape[2]
    d_ff = ffn_w_down.shape[1]
    pos_p = jnp.arange(T)
    pos_s = P + jnp.arange(Ts)
    yp, ys = x_prompt, x_sample
    p_rows = None
    s_rows, p_small, s_small = [], [], []
    for l in range(depth):
        lw = _layer_weights(w_in[l], mla_w_uq[l], mla_w_ukv[l])
        prm = (norm_mix_g[l], mla_q_norm_g[l], mla_kv_norm_g[l], hgrn_lb_logits, hgrn_norm_g[l], diff_lambda[l],
               diff_norm_g[l], w_branch[l], w_out[l], norm_ffn_g[l], ffn_w_up[l], ffn_conv_w[l], ffn_conv_b[l],
               ffn_w_down[l], norm_final_g)
        final = l == depth - 1
        yp, p_rows, small = _layer(l, depth, yp, pos_p, None, p_rows, jnp.zeros((B, HG_HEADS, HG_DK, HG_DV), F32),
                                   jnp.zeros((B, CONV_W - 1, d_ff), F32), lw, prm, final)
        p_small.append(small)
        ys, rows, small = _layer(l, depth, ys, pos_s,
                                 (cache_mla_ckv, cache_mla_krope, cache_diff_k, cache_diff_v), None,
                                 state_hgrn[l], state_ffn_conv[l], lw, prm, final)
        s_rows.append(rows)
        s_small.append(small)
    stk = lambda states, i: jnp.stack([s[i] for s in states], axis=0)
    heads = lambda a, w: a.reshape(a.shape[:3] + (-1, w))
    s_rows = [stk(s_rows, i) for i in range(4)]
    out_rows = lambda r: (r[0], r[1], heads(r[2], 2 * DF_DH), heads(r[3], DF_DV))
    return ((yp, ys) + out_rows(p_rows) + (stk(p_small, 0), stk(p_small, 1))
            + out_rows(s_rows) + (stk(s_small, 0), stk(s_small, 1)))
```

```python
import functools
import math

import jax
import jax.numpy as jnp
from jax import lax
from jax.experimental import pallas as pl
from jax.experimental.pallas import tpu as pltpu

CHUNK = 64
EPS = 1e-6
NEG = -1e30
F_MIN = 1e-12
MLA_HEADS = 8
MLA_NOPE = 64
MLA_ROPE = 32
MLA_V = 64
MLA_Q_LORA = 384
MLA_KV_LORA = 256
ROPE_THETA = 10000.0
MLA_SCALE = (MLA_NOPE + MLA_ROPE) ** -0.5
HG_HEADS = 8
HG_DK = 64
HG_DV = 64
DF_HEADS = 8
DF_DH = 32
DF_DV = 2 * DF_DH
DF_SCALE = DF_DH ** -0.5
N_BRANCH = 3
BR_WIDTH = 512
CONV_W = 3
LOG2E = 1.4426950408889634

LANE = 128
VMEM_LIMIT = 56 * 1024 * 1024
KV_BLOCK = 256
HG_CHUNK = 64
HG_CHUNKS_PER_STEP = 8
HG_SINGLE_REF_LIMIT = 43.0
FF_BLOCK = 1024
MLA_HEADS_PER_STEP = 8
DF_PAIRS_PER_STEP = 2

BF16 = jnp.bfloat16
F32 = jnp.float32


def _cparams(sem):
    return pltpu.CompilerParams(dimension_semantics=sem, vmem_limit_bytes=VMEM_LIMIT)


def _dot(a, b):
    return jnp.dot(a, b, preferred_element_type=F32)


def _dot_nt(a, b):
    return lax.dot_general(a, b, (((1,), (1,)), ((), ())), preferred_element_type=F32)


def _rms(x, g):
    r = lax.rsqrt(jnp.mean(x * x, axis=-1, keepdims=True) + EPS)
    return x * r * g


def _sigmoid(x):
    return 0.5 * jnp.tanh(0.5 * x) + 0.5


def _sigmoid_pair(x):
    e = jnp.exp(-jnp.abs(x))
    r = 1.0 / (1.0 + e)
    er = e * r
    pos = x >= 0.0
    return jnp.where(pos, r, er), jnp.where(pos, er, r)


def _const_spec(shape):
    nd = len(shape)
    return pl.BlockSpec(shape, lambda *_: (0,) * nd)


def _mla_qkv(h, wm_ref, gq_ref, wq_ref, gkv_ref, cq_ref, sq_ref, ck_ref, sk_ref):
    z = _dot(h, wm_ref[...])
    qn = _rms(z[:, :MLA_Q_LORA], gq_ref[...]).astype(BF16)
    q2 = _dot(qn, wq_ref[...])
    nq = MLA_HEADS * LANE
    cq = jnp.concatenate([cq_ref[...]] * MLA_HEADS, axis=1)
    sq = jnp.concatenate([sq_ref[...]] * MLA_HEADS, axis=1)
    q = (q2[:, :nq] * cq + q2[:, nq:] * sq).astype(BF16)
    ckv = _rms(z[:, MLA_Q_LORA:MLA_Q_LORA + MLA_KV_LORA], gkv_ref[...])
    o = MLA_Q_LORA + MLA_KV_LORA
    kr = z[:, o:o + LANE] * ck_ref[...] + z[:, o + LANE:o + 2 * LANE] * sk_ref[...]
    return q, ckv, kr[:, :MLA_ROPE]


def _mla_expand(ckv, krot, wk_ref, e_ref, wvt_ref):
    c = ckv.astype(BF16)
    k = (_dot(c, wk_ref[...]) + _dot(krot.astype(BF16), e_ref[...])).astype(BF16)
    return k, _dot_nt(wvt_ref[...], c).astype(BF16)


def _diff_cols(h, wd_ref):
    z = _dot(h, wd_ref[...])
    n = DF_HEADS * 2 * DF_DH
    return (z[:, :n] * (DF_SCALE * LOG2E)).astype(BF16), z[:, n:2 * n], z[:, 2 * n:]


def _proj_kernel(*refs, tm, aliased):
    (x_ref, gmix_ref, wm_ref, gq_ref, wq_ref, gkv_ref, cq_ref, sq_ref, ck_ref, sk_ref, wk_ref, e_ref, wvt_ref,
     wd_ref) = refs[:14]
    (q_ref, kc_ref, vtm_ref, dq_ref, kd_ref, vtd_ref, ckv_ref, krot_ref, dk_ref, dv_ref) = refs[14 + aliased:]
    h = _rms(x_ref[0], gmix_ref[...]).astype(BF16)
    q, ckv, krot = _mla_qkv(h, wm_ref, gq_ref, wq_ref, gkv_ref, cq_ref, sq_ref, ck_ref, sk_ref)
    k, vt = _mla_expand(ckv, krot, wk_ref, e_ref, wvt_ref)
    dq, dk, dv = _diff_cols(h, wd_ref)
    dvt = dv.T.astype(BF16)
    q_ref[0] = q
    kc_ref[0] = k
    dq_ref[0] = dq
    kd_ref[0] = dk.astype(BF16)
    for i in range(tm // KV_BLOCK):
        vtm_ref[0, i] = vt[:, i * KV_BLOCK:(i + 1) * KV_BLOCK]
        vtd_ref[0, i] = dvt[:, i * KV_BLOCK:(i + 1) * KV_BLOCK]
    ckv_ref[0, 0] = ckv
    krot_ref[0, 0] = krot
    dk_ref[0, 0] = dk
    dv_ref[0, 0] = dv


def _proj(x, gmix, wm, gq, wq, gkv, cq, sq, ck, sk, wk, e, wvt, wd, layer, depth, prev, tm):
    B, T, D = x.shape
    nd = DF_HEADS * 2 * DF_DH
    nk = tm // KV_BLOCK
    tok = lambda w: pl.BlockSpec((1, tm, w), lambda b, t: (b, t, 0))
    tab = pl.BlockSpec((tm, LANE), lambda b, t: (t, 0))
    vts = lambda r: pl.BlockSpec((1, nk, r, KV_BLOCK), lambda b, t: (b, t, 0, 0))
    st = lambda w: pl.BlockSpec((1, 1, tm, w), lambda b, t: (layer, b, t, 0))
    consts = (gmix, wm, gq, wq, gkv)
    consts2 = (wk, e, wvt, wd)
    in_specs = ([tok(D)] + [_const_spec(a.shape) for a in consts] + [tab] * 4
                + [_const_spec(a.shape) for a in consts2])
    args = (x,) + consts + (cq, sq, ck, sk) + consts2
    aliases = {}
    if prev is not None:
        in_specs += [pl.BlockSpec(memory_space=pl.ANY)] * 4
        aliases = {len(args) + i: 6 + i for i in range(4)}
        args += tuple(prev)
    widths = (MLA_KV_LORA, MLA_ROPE, nd, nd)
    outs = pl.pallas_call(
        functools.partial(_proj_kernel, tm=tm, aliased=4 if prev is not None else 0),
        grid=(B, T // tm),
        in_specs=in_specs,
        out_specs=[tok(MLA_HEADS * LANE), tok(MLA_HEADS * LANE), vts(MLA_HEADS * MLA_V), tok(nd), tok(nd), vts(nd)]
        + [st(w) for w in widths],
        out_shape=[jax.ShapeDtypeStruct((B, T, MLA_HEADS * LANE), BF16),
                   jax.ShapeDtypeStruct((B, T, MLA_HEADS * LANE), BF16),
                   jax.ShapeDtypeStruct((B, T // KV_BLOCK, MLA_HEADS * MLA_V, KV_BLOCK), BF16),
                   jax.ShapeDtypeStruct((B, T, nd), BF16),
                   jax.ShapeDtypeStruct((B, T, nd), BF16),
                   jax.ShapeDtypeStruct((B, T // KV_BLOCK, nd, KV_BLOCK), BF16)]
        + [jax.ShapeDtypeStruct((depth, B, T, w), F32) for w in widths],
        input_output_aliases=aliases,
        compiler_params=_cparams(("parallel", "parallel")),
        name="proj",
    )(*args)
    return outs[:6], outs[6:]


def _mla_proj_kernel(x_ref, gmix_ref, wm_ref, gq_ref, wq_ref, gkv_ref, cq_ref, sq_ref, ck_ref, sk_ref,
                     q_ref, ckv_ref, krot_ref):
    h = _rms(x_ref[0], gmix_ref[...]).astype(BF16)
    q_ref[0], ckv_ref[0], krot_ref[0] = _mla_qkv(h, wm_ref, gq_ref, wq_ref, gkv_ref, cq_ref, sq_ref, ck_ref,
                                                  sk_ref)


def _mla_proj(x, gmix, wm, gq, wq, gkv, cq, sq, ck, sk, tm):
    B, T, D = x.shape
    grid = (B, T // tm)
    tok = lambda w: pl.BlockSpec((1, tm, w), lambda b, t: (b, t, 0))
    tab = pl.BlockSpec((tm, LANE), lambda b, t: (t, 0))
    return pl.pallas_call(
        _mla_proj_kernel,
        grid=grid,
        in_specs=[tok(D), _const_spec(gmix.shape), _const_spec(wm.shape), _const_spec(gq.shape),
                  _const_spec(wq.shape), _const_spec(gkv.shape), tab, tab, tab, tab],
        out_specs=[tok(MLA_HEADS * LANE), tok(MLA_KV_LORA), tok(MLA_ROPE)],
        out_shape=[jax.ShapeDtypeStruct((B, T, MLA_HEADS * LANE), BF16),
                   jax.ShapeDtypeStruct((B, T, MLA_KV_LORA), F32),
                   jax.ShapeDtypeStruct((B, T, MLA_ROPE), F32)],
        compiler_params=_cparams(("parallel", "parallel")),
        name="mla_proj",
    )(x, gmix, wm, gq, wq, gkv, cq, sq, ck, sk)


def _past_new_specs(layer, n_past, width):
    return (pl.BlockSpec((1, 1, KV_BLOCK, width), lambda b, t: (layer, b, jnp.minimum(t, n_past - 1), 0)),
            pl.BlockSpec((1, KV_BLOCK, width), lambda b, t: (b, 0, 0)))


def _mla_kv_kernel(pc_ref, nc_ref, pr_ref, nr_ref, wk_ref, e_ref, wvt_ref, k_ref, vt_ref, *, n_past):
    t = pl.program_id(1)

    @pl.when(t < n_past)
    def _():
        k_ref[0], vt_ref[0, 0] = _mla_expand(pc_ref[0, 0], pr_ref[0, 0], wk_ref, e_ref, wvt_ref)

    @pl.when(t >= n_past)
    def _():
        k_ref[0], vt_ref[0, 0] = _mla_expand(nc_ref[0], nr_ref[0], wk_ref, e_ref, wvt_ref)


def _mla_kv(layer, ckv_past, krot_past, ckv_new, krot_new, wk, e, wvt):
    _, B, P, _ = ckv_past.shape
    n_past = P // KV_BLOCK
    nkb = n_past + 1
    return pl.pallas_call(
        functools.partial(_mla_kv_kernel, n_past=n_past),
        grid=(B, nkb),
        in_specs=[*_past_new_specs(layer, n_past, MLA_KV_LORA), *_past_new_specs(layer, n_past, MLA_ROPE),
                  _const_spec(wk.shape), _const_spec(e.shape), _const_spec(wvt.shape)],
        out_specs=[pl.BlockSpec((1, KV_BLOCK, MLA_HEADS * LANE), lambda b, t: (b, t, 0)),
                   pl.BlockSpec((1, 1, MLA_HEADS * MLA_V, KV_BLOCK), lambda b, t: (b, t, 0, 0))],
        out_shape=[jax.ShapeDtypeStruct((B, nkb * KV_BLOCK, MLA_HEADS * LANE), BF16),
                   jax.ShapeDtypeStruct((B, nkb, MLA_HEADS * MLA_V, KV_BLOCK), BF16)],
        compiler_params=_cparams(("parallel", "parallel")),
        name="mla_kv",
    )(ckv_past, ckv_new, krot_past, krot_new, wk, e, wvt)


def _visible_blocks(q0, tq, kvalid, nkb):
    n_full = jnp.minimum((((q0 >> 6) + 1) * CHUNK) // KV_BLOCK, kvalid // KV_BLOCK)
    last = (((q0 + tq - 1) >> 6) + 1) * CHUNK
    n_vis = jnp.minimum((last + KV_BLOCK - 1) // KV_BLOCK, nkb)
    return n_full, n_vis


def _flash_t(streams, k_ref, vt_ref, m_s, acc_s, n_full, n_vis, q0, tq, kvalid):
    ones = jnp.ones((16, KV_BLOCK), BF16)
    n = len(streams)

    def step(kbs, masked):
        ss = []
        for qm, ksl, _ in streams:
            ss.append([_dot_nt(k_ref[0, pl.ds(pl.multiple_of(kb * KV_BLOCK, KV_BLOCK), KV_BLOCK), ksl], qm)
                       for kb in kbs])
        if any(masked):
            qchunk = (q0 + lax.broadcasted_iota(jnp.int32, (KV_BLOCK, tq), 1)) >> 6
            for j, kb in enumerate(kbs):
                if masked[j]:
                    kpos = kb * KV_BLOCK + lax.broadcasted_iota(jnp.int32, (KV_BLOCK, tq), 0)
                    vis = jnp.where(kpos < kvalid, kpos >> 6, jnp.int32(2 ** 30)) <= qchunk
                    for s in ss:
                        s[j] = jnp.where(vis, s[j], NEG)
        ms = []
        for i, s in enumerate(ss):
            m = m_s[i]
            for sj in s:
                m = jnp.maximum(m, jnp.max(sj, axis=0, keepdims=True))
            ms.append(m)
        ps = [[jnp.exp2(sj - m_new).astype(BF16) for sj in s] for s, m_new in zip(ss, ms)]
        for i, ((_, _, vsl), m_new, p) in enumerate(zip(streams, ms, ps)):
            acc = jnp.exp2(m_s[i] - m_new) * acc_s[i]
            for kb, pj in zip(kbs, p):
                vt = jnp.concatenate([vt_ref[0, kb, vsl, :], ones], axis=0)
                acc = acc + _dot(vt, pj)
            acc_s[i] = acc
            m_s[i] = m_new

    for i in range(n):
        m_s[i] = jnp.full((1, tq), -jnp.inf, F32)
        acc_s[i] = jnp.zeros((MLA_V + 16, tq), F32)
    n_pairs = n_full // 2

    def pair_step(j, c):
        step([2 * j, 2 * j + 1], (False, False))
        return c

    def masked_step(kb, c):
        step([kb], (True,))
        return c

    def quad_step(j, c):
        step([4 * j, 4 * j + 1], (False, False))
        step([4 * j + 2, 4 * j + 3], (False, False))
        return c

    n_quads = n_pairs // 2
    lax.fori_loop(0, n_quads, quad_step, 0)
    lax.fori_loop(2 * n_quads, n_pairs, pair_step, 0)
    mixed = jnp.logical_and(n_full - 2 * n_pairs == 1, n_vis > n_full)

    @pl.when(mixed)
    def _():
        step([n_full - 1, n_full], (False, True))

    lax.fori_loop(jnp.where(mixed, n_full + 1, 2 * n_pairs), n_vis, masked_step, 0)
    return [acc_s[i] for i in range(n)]


def _attn_mla_kernel(q_ref, k_ref, vt_ref, o_ref, m_s, acc_s, *, tq, q_off, kvalid, nkb):
    q0 = q_off + pl.program_id(2) * tq
    n_full, n_vis = _visible_blocks(q0, tq, kvalid, nkb)
    streams = []
    for hh in range(MLA_HEADS_PER_STEP):
        lanes = slice(hh * LANE, (hh + 1) * LANE)
        streams.append((q_ref[0, :, lanes], lanes, slice(hh * MLA_V, (hh + 1) * MLA_V)))
    accs = _flash_t(streams, k_ref, vt_ref, m_s, acc_s, n_full, n_vis, q0, tq, kvalid)
    outs = [acc[:MLA_V] * (1.0 / acc[MLA_V:MLA_V + 1]) for acc in accs]
    for i in range(0, MLA_HEADS_PER_STEP, 2):
        o_ref[0, :, i * MLA_V:(i + 2) * MLA_V] = jnp.concatenate(outs[i:i + 2], axis=0).T.astype(BF16)


def _attn_mla(q, k, vt, tq, q_off, kvalid):
    B, T, _ = q.shape
    Tk = k.shape[1]
    nkb = Tk // KV_BLOCK
    hps = MLA_HEADS_PER_STEP
    kern = functools.partial(_attn_mla_kernel, tq=tq, q_off=q_off, kvalid=kvalid, nkb=nkb)
    return pl.pallas_call(
        kern,
        grid=(B, MLA_HEADS // hps, T // tq),
        in_specs=[pl.BlockSpec((1, tq, hps * LANE), lambda b, p, i: (b, i, p)),
                  pl.BlockSpec((1, Tk, hps * LANE), lambda b, p, i: (b, 0, p)),
                  pl.BlockSpec((1, nkb, hps * MLA_V, KV_BLOCK), lambda b, p, i: (b, 0, p, 0))],
        out_specs=pl.BlockSpec((1, tq, hps * MLA_V), lambda b, p, i: (b, i, p)),
        out_shape=jax.ShapeDtypeStruct((B, T, MLA_HEADS * MLA_V), BF16),
        scratch_shapes=[pltpu.VMEM((hps, 1, tq), F32), pltpu.VMEM((hps, MLA_V + 16, tq), F32)],
        compiler_params=_cparams(("parallel", "parallel", "arbitrary")),
        name="attn_mla",
    )(q, k, vt)


def _attn_diff_kernel(q_ref, k_ref, vt_ref, lam_ref, g_ref, o_ref, m_s, acc_s, *, tq, q_off, kvalid, nkb,
                      lam_init):
    q0 = q_off + pl.program_id(2) * tq
    n_full, n_vis = _visible_blocks(q0, tq, kvalid, nkb)
    lamv = lam_ref[...]
    lam = (jnp.exp(jnp.sum(lamv[0:1] * lamv[1:2], axis=-1, keepdims=True))
           - jnp.exp(jnp.sum(lamv[2:3] * lamv[3:4], axis=-1, keepdims=True)) + lam_init)
    lane = lax.broadcasted_iota(jnp.int32, (1, LANE), 1)
    streams = []
    for pr in range(DF_PAIRS_PER_STEP):
        lanes = slice(pr * LANE, (pr + 1) * LANE)
        q = q_ref[0, :, lanes]
        for hh in range(2):
            for j in range(2):
                lo = hh * 2 * DF_DH + j * DF_DH
                qm = q * jnp.where((lane >= lo) & (lane < lo + DF_DH), 1.0, 0.0).astype(BF16)
                streams.append((qm, lanes, slice((2 * pr + hh) * DF_DV, (2 * pr + hh + 1) * DF_DV)))
    accs = _flash_t(streams, k_ref, vt_ref, m_s, acc_s, n_full, n_vis, q0, tq, kvalid)
    maps = [acc[:DF_DV] * (1.0 / acc[DF_DV:DF_DV + 1]) for acc in accs]
    for pr in range(DF_PAIRS_PER_STEP):
        outs = []
        for hh in range(2):
            i = 4 * pr + 2 * hh
            o = maps[i] - lam * maps[i + 1]
            outs.append(o * lax.rsqrt(jnp.mean(o * o, axis=0, keepdims=True) + EPS))
        ot = jnp.concatenate(outs, axis=0).T
        o_ref[0, :, pr * LANE:(pr + 1) * LANE] = (ot * g_ref[...] * (1.0 - lam_init)).astype(BF16)


def _attn_diff(q, k, vt, lam_rows, g2, tq, q_off, kvalid, lam_init):
    B, T, _ = q.shape
    Tk = k.shape[1]
    nkb = Tk // KV_BLOCK
    pps = DF_PAIRS_PER_STEP
    kern = functools.partial(_attn_diff_kernel, tq=tq, q_off=q_off, kvalid=kvalid, nkb=nkb,
                             lam_init=lam_init)
    return pl.pallas_call(
        kern,
        grid=(B, DF_HEADS // (2 * pps), T // tq),
        in_specs=[pl.BlockSpec((1, tq, pps * LANE), lambda b, p, i: (b, i, p)),
                  pl.BlockSpec((1, Tk, pps * LANE), lambda b, p, i: (b, 0, p)),
                  pl.BlockSpec((1, nkb, pps * 2 * DF_DV, KV_BLOCK), lambda b, p, i: (b, 0, p, 0)),
                  _const_spec(lam_rows.shape), _const_spec(g2.shape)],
        out_specs=pl.BlockSpec((1, tq, pps * 2 * DF_DV), lambda b, p, i: (b, i, p)),
        out_shape=jax.ShapeDtypeStruct((B, T, DF_HEADS * DF_DV), BF16),
        scratch_shapes=[pltpu.VMEM((4 * pps, 1, tq), F32), pltpu.VMEM((4 * pps, DF_DV + 16, tq), F32)],
        compiler_params=_cparams(("parallel", "parallel", "arbitrary")),
        name="attn_diff",
    )(q, k, vt, lam_rows, g2)


def _diff_proj_kernel(x_ref, gmix_ref, wd_ref, q_ref, dk_ref, dv_ref):
    h = _rms(x_ref[0], gmix_ref[...]).astype(BF16)
    q_ref[0], dk_ref[0], dv_ref[0] = _diff_cols(h, wd_ref)


def _diff_proj(x, gmix, wd, tm):
    B, T, D = x.shape
    n = DF_HEADS * 2 * DF_DH
    tok = lambda w: pl.BlockSpec((1, tm, w), lambda b, t: (b, t, 0))
    return pl.pallas_call(
        _diff_proj_kernel,
        grid=(B, T // tm),
        in_specs=[tok(D), _const_spec(gmix.shape), _const_spec(wd.shape)],
        out_specs=[tok(n), tok(n), tok(n)],
        out_shape=[jax.ShapeDtypeStruct((B, T, n), BF16),
                   jax.ShapeDtypeStruct((B, T, n), F32),
                   jax.ShapeDtypeStruct((B, T, n), F32)],
        compiler_params=_cparams(("parallel", "parallel")),
        name="diff_proj",
    )(x, gmix, wd)


def _diff_kv_kernel(pk_ref, nk_ref, pv_ref, nv_ref, k_ref, vt_ref, *, n_past):
    t = pl.program_id(1)

    @pl.when(t < n_past)
    def _():
        k_ref[0] = pk_ref[0, 0].astype(BF16)
        vt_ref[0, 0] = pv_ref[0, 0].T.astype(BF16)

    @pl.when(t >= n_past)
    def _():
        k_ref[0] = nk_ref[0].astype(BF16)
        vt_ref[0, 0] = nv_ref[0].T.astype(BF16)


def _diff_kv(layer, dk_past, dv_past, dk_new, dv_new):
    _, B, P, n = dk_past.shape
    n_past = P // KV_BLOCK
    nkb = n_past + 1
    return pl.pallas_call(
        functools.partial(_diff_kv_kernel, n_past=n_past),
        grid=(B, nkb),
        in_specs=[*_past_new_specs(layer, n_past, n), *_past_new_specs(layer, n_past, n)],
        out_specs=[pl.BlockSpec((1, KV_BLOCK, n), lambda b, t: (b, t, 0)),
                   pl.BlockSpec((1, 1, n, KV_BLOCK), lambda b, t: (b, t, 0, 0))],
        out_shape=[jax.ShapeDtypeStruct((B, nkb * KV_BLOCK, n), BF16),
                   jax.ShapeDtypeStruct((B, nkb, n, KV_BLOCK), BF16)],
        compiler_params=_cparams(("parallel", "parallel")),
        name="diff_kv",
    )(dk_past, dk_new, dv_past, dv_new)


def _split2(x):
    a = x.astype(BF16)
    return a, (x - a.astype(F32)).astype(BF16)


def _level_ref_rows(b, size):
    half = size // 2
    C = b.shape[0]
    pieces = [jnp.broadcast_to(b[i * size + half - 1:i * size + half, :], (size, b.shape[1]))
              for i in range(C // size)]
    return pieces[0] if len(pieces) == 1 else jnp.concatenate(pieces, axis=0)


def _base_ref_rows(b):
    C, W = b.shape
    sub = lax.broadcasted_iota(jnp.int32, (8, W), 0)
    pieces = [jnp.where(sub < 4, jnp.broadcast_to(b[8 * i:8 * i + 1, :], (8, W)),
                        jnp.broadcast_to(b[8 * i + 4:8 * i + 5, :], (8, W))) for i in range(C // 8)]
    return jnp.concatenate(pieces, axis=0)


def _hgrn_kernel(x_ref, gmix_ref, wh_ref, lbl_ref, st0_ref, gh_ref, gavg_ref, ob_ref, st_ref, z_s, b_s,
                 *, layer, tm):
    C = min(HG_CHUNK, tm)
    W = HG_HEADS * HG_DK
    ti = pl.program_id(1)

    @pl.when(ti == 0)
    def _():
        st_ref[0] = st0_ref[0]

    h = _rms(x_ref[0], gmix_ref[...]).astype(BF16)
    z_s[...] = _dot(h, wh_ref[...])

    lg = lbl_ref[...]
    e = jnp.exp(lg - jnp.max(lg, axis=0, keepdims=True))
    sm = e / jnp.sum(e, axis=0, keepdims=True)
    lb = jnp.zeros((1, W), F32)
    for i in range(1, layer + 1):
        lb = lb + sm[i:i + 1]

    row = lax.broadcasted_iota(jnp.int32, (C, C), 0)
    col = lax.broadcasted_iota(jnp.int32, (C, C), 1)
    tril_b = col <= row
    tril = tril_b.astype(BF16)
    rsub = lax.broadcasted_iota(jnp.int32, (C, LANE), 0)
    lane = lax.broadcasted_iota(jnp.int32, (C, LANE), 1)
    head_lo = lane < HG_DK
    sizes = [s for s in (64, 32, 16, 8) if s <= C]
    blk_masks = {s: (row // s) == (col // s) for s in sizes}
    base_mask = ((row // 4) == (col // 4)) & tril_b
    diag = ((lax.broadcasted_iota(jnp.int32, (LANE, LANE), 0) // HG_DK)
            == (lax.broadcasted_iota(jnp.int32, (LANE, LANE), 1) // HG_DK))
    mid = C // 2 - 1
    n_pairs = HG_HEADS // 2

    group = min(HG_CHUNKS_PER_STEP, tm // C)

    def prep(j, spread):
        for i in range(group):
            r0 = pl.multiple_of((j * group + i) * C, C)
            hq = z_s[pl.ds(r0, C), 0:W]
            hf = z_s[pl.ds(r0, C), W:2 * W]
            hg = z_s[pl.ds(r0, C), 3 * W:4 * W]
            sig, sig_neg = _sigmoid_pair(hf)
            f = sig + lb * sig_neg
            l1, l2 = _split2(jnp.log(jnp.maximum(f, F_MIN)))
            b = _dot(tril, l1) + _dot(tril, l2)
            z_s[pl.ds(r0, C), 0:W] = hq * _sigmoid(hq)
            z_s[pl.ds(r0, C), W:2 * W] = (1.0 - lb) * sig_neg
            z_s[pl.ds(r0, C), 3 * W:4 * W] = hg * _sigmoid(hg)
            b_s[pl.ds(r0, C), :] = b
            spread = jnp.maximum(spread, jnp.maximum(b[0:1] - b[mid:mid + 1], b[mid:mid + 1] - b[C - 1:C]))
        return spread

    spread = lax.fori_loop(0, tm // (C * group), prep, jnp.zeros((1, W), F32))
    single_ref_ok = jnp.max(spread) <= HG_SINGLE_REF_LIMIT

    def heads(x):
        return (jnp.where(head_lo, x, 0.0).astype(BF16), jnp.where(head_lo, 0.0, x).astype(BF16))

    def finish(cs, bs, qs, ks, vs, a_all):
        o_all = []
        for i, c in enumerate(cs):
            b, q, k, v = bs[i], qs[i], ks[i], vs[i]
            b_end = b[C - 1:C, :]
            qe = (q * jnp.exp(b)).astype(BF16)
            ke = (k * jnp.exp(b_end - b)).astype(BF16)
            d_end = jnp.exp(b_end)
            o_pairs = []
            for p in range(n_pairs):
                sl = slice(p * LANE, (p + 1) * LANE)
                v0, v1 = heads(v[:, sl])
                a0, a1 = a_all[i][p]
                st = st_ref[0, p]
                o_p = (_dot(a0.astype(BF16), v0) + _dot(a1.astype(BF16), v1)
                       + _dot_nt(qe[:, sl], st.astype(BF16)))
                upd = _dot(v[:, sl].T.astype(BF16), ke[:, sl])
                st_ref[0, p] = st * d_end[:, sl] + jnp.where(diag, upd, 0.0)
                o_pairs.append(o_p)
            o_all.append(jnp.concatenate(o_pairs, axis=1))
        for c, o in zip(cs, o_all):
            r0 = pl.multiple_of(c * C, C)
            ms = _dot((o * o).astype(BF16), gavg_ref[...])
            ob = o * lax.rsqrt(ms + EPS) * gh_ref[...] * z_s[pl.ds(r0, C), 3 * W:4 * W]
            ob_ref[0, pl.ds(r0, C), :] = ob.astype(BF16)

    def load(cs):
        out = []
        for c in cs:
            r0 = pl.multiple_of(c * C, C)
            out.append((b_s[pl.ds(r0, C), :], z_s[pl.ds(r0, C), 0:W], z_s[pl.ds(r0, C), W:2 * W],
                        z_s[pl.ds(r0, C), 2 * W:3 * W]))
        return tuple(zip(*out))

    def fast(j, carry):
        cs = [j * group + i for i in range(group)]
        bs, qs, ks, vs = load(cs)
        a_all = []
        for b, q, k in zip(bs, qs, ks):
            ref = b[mid:mid + 1, :]
            qt = q * jnp.exp(b - ref)
            kt = (k * jnp.exp(ref - b)).astype(BF16)
            a_c = []
            for p in range(n_pairs):
                sl = slice(p * LANE, (p + 1) * LANE)
                q0, q1 = heads(qt[:, sl])
                a_c.append((jnp.where(tril_b, _dot_nt(q0, kt[:, sl]), 0.0),
                            jnp.where(tril_b, _dot_nt(q1, kt[:, sl]), 0.0)))
            a_all.append(a_c)
        finish(cs, bs, qs, ks, vs, a_all)
        return carry

    def safe(c, carry):
        bs, qs, ks, vs = load([c])
        a_c = []
        for p in range(n_pairs):
            sl = slice(p * LANE, (p + 1) * LANE)
            bp, qp, kp = bs[0][:, sl], qs[0][:, sl], ks[0][:, sl]
            a_h = [jnp.zeros((C, C), F32), jnp.zeros((C, C), F32)]
            for s in sizes:
                ref = _level_ref_rows(bp, s)
                upper = (rsub & (s - 1)) >= (s // 2)
                qt = heads(qp * jnp.exp(jnp.where(upper, bp - ref, NEG)))
                kt = (kp * jnp.exp(jnp.where(upper, NEG, ref - bp))).astype(BF16)
                for hh in range(2):
                    pr = _dot_nt(qt[hh], kt)
                    a_h[hh] = a_h[hh] + (pr if s == C else jnp.where(blk_masks[s], pr, 0.0))
            ref = _base_ref_rows(bp)
            qt = heads(qp * jnp.exp(bp - ref))
            kt = (kp * jnp.exp(ref - bp)).astype(BF16)
            for hh in range(2):
                a_h[hh] = a_h[hh] + jnp.where(base_mask, _dot_nt(qt[hh], kt), 0.0)
            a_c.append(tuple(a_h))
        finish([c], bs, qs, ks, vs, [a_c])
        return carry

    @pl.when(single_ref_ok)
    def _():
        lax.fori_loop(0, tm // (C * group), fast, 0)

    @pl.when(jnp.logical_not(single_ref_ok))
    def _():
        lax.fori_loop(0, tm // C, safe, 0)


def _hgrn(x, gmix, wh, lbl, st0, gh, gavg, layer, tm):
    B, T, D = x.shape
    W = HG_HEADS * HG_DK
    st_spec = pl.BlockSpec((1, HG_HEADS // 2, LANE, LANE), lambda b, t: (b, 0, 0, 0))
    return pl.pallas_call(
        functools.partial(_hgrn_kernel, layer=layer, tm=tm),
        grid=(B, T // tm),
        in_specs=[pl.BlockSpec((1, tm, D), lambda b, t: (b, t, 0)), _const_spec(gmix.shape),
                  _const_spec(wh.shape), _const_spec(lbl.shape), st_spec, _const_spec(gh.shape),
                  _const_spec(gavg.shape)],
        out_specs=[pl.BlockSpec((1, tm, W), lambda b, t: (b, t, 0)), st_spec],
        out_shape=[jax.ShapeDtypeStruct((B, T, W), BF16),
                   jax.ShapeDtypeStruct((B, HG_HEADS // 2, LANE, LANE), F32)],
        scratch_shapes=[pltpu.VMEM((tm, 4 * W), F32), pltpu.VMEM((tm, W), F32)],
        compiler_params=_cparams(("parallel", "arbitrary")),
        name="hgrn",
    )(x, gmix, wh, lbl, st0, gh, gavg)


def _post_kernel(x_ref, oa_ref, ob_ref, oc_ref, gmix_ref, wg_ref, wb_ref, wo_ref, gffn_ref, wup_ref,
                 cw_ref, cb_ref, wdn_ref, conv0_ref, gfin_ref, y_ref, conv_ref, carry_s, *, tm, d_ff, final):
    ti = pl.program_id(1)
    x = x_ref[0]
    D = x.shape[1]
    h = _rms(x, gmix_ref[...]).astype(BF16)
    mixed = jnp.zeros((tm, D), F32)
    for n, o_ref in enumerate((oa_ref, ob_ref, oc_ref)):
        gate = _sigmoid(_dot(h, wg_ref[:, n * D:(n + 1) * D]))
        mixed = mixed + gate * _dot(o_ref[0], wb_ref[n])
    x1 = x + _dot(mixed.astype(BF16), wo_ref[...])
    xn = _rms(x1, gffn_ref[...]).astype(BF16)

    @pl.when(ti == 0)
    def _():
        carry_s[0:CONV_W - 1, :] = conv0_ref[0]

    acc = jnp.zeros((tm, D), F32)
    for f in range(0, d_ff, FF_BLOCK):
        w = min(FF_BLOCK, d_ff - f)
        cols = slice(f, f + w)
        a = _dot(xn, wup_ref[:, cols])
        vv = _dot(xn, wup_ref[:, d_ff + f:d_ff + f + w])
        p0 = carry_s[0:1, cols]
        p1 = carry_s[1:2, cols]
        r1 = pltpu.roll(a, 1, 0)
        r2 = pltpu.roll(a, 2, 0)
        last2 = r2[0:8, :]
        rows = lax.broadcasted_iota(jnp.int32, (8, w), 0)
        a1 = jnp.concatenate([jnp.where(rows < 1, p1, r1[0:8]), r1[8:]], axis=0) if tm > 8 else \
            jnp.where(rows < 1, p1, r1)
        h2 = jnp.where(rows < 1, p0, jnp.where(rows < 2, p1, last2))
        a2 = jnp.concatenate([h2, r2[8:]], axis=0) if tm > 8 else h2
        cc = (cb_ref[:, cols] + a2 * cw_ref[0:1, cols] + a1 * cw_ref[1:2, cols] + a * cw_ref[2:3, cols])
        act = (cc * _sigmoid(cc) * vv).astype(BF16)
        acc = acc + _dot(act, wdn_ref[cols, :])
        carry_s[:, cols] = last2
        conv_ref[0, :, cols] = last2[0:CONV_W - 1, :]
    x2 = x1 + acc
    y_ref[0] = _rms(x2, gfin_ref[...]) if final else x2


def _post(x, oa, ob, oc, gmix, wg, wb, wo, gffn, wup, cw, cb, wdn, conv0, gfin, tm, final):
    B, T, D = x.shape
    d_ff = wdn.shape[0]
    tok = lambda w: pl.BlockSpec((1, tm, w), lambda b, t: (b, t, 0))
    cspec = pl.BlockSpec((1, CONV_W - 1, d_ff), lambda b, t: (b, 0, 0))
    one = lambda a: pl.BlockSpec(a.shape, lambda *_: (0,) * a.ndim, pipeline_mode=pl.Buffered(1))
    return pl.pallas_call(
        functools.partial(_post_kernel, tm=tm, d_ff=d_ff, final=final),
        grid=(B, T // tm),
        in_specs=[tok(D), tok(BR_WIDTH), tok(BR_WIDTH), tok(BR_WIDTH), one(gmix), one(wg), one(wb), one(wo),
                  one(gffn), one(wup), one(cw), one(cb), one(wdn), cspec, one(gfin)],
        out_specs=[tok(D), cspec],
        out_shape=[jax.ShapeDtypeStruct((B, T, D), F32),
                   jax.ShapeDtypeStruct((B, CONV_W - 1, d_ff), F32)],
        scratch_shapes=[pltpu.VMEM((8, d_ff), F32)],
        compiler_params=_cparams(("parallel", "arbitrary")),
        name="post",
    )(x, oa, ob, oc, gmix, wg, wb, wo, gffn, wup, cw, cb, wdn, conv0, gfin)


def _rot_half_cols(w):
    half = w.shape[-1] // 2
    return jnp.concatenate([-w[..., half:], w[..., :half]], axis=-1)


def _pad_cols(w, width):
    return jnp.pad(w, [(0, 0)] * (w.ndim - 1) + [(0, width - w.shape[-1])])


def _layer_weights(w_in, w_uq, w_ukv):
    D = w_in.shape[0]
    o_q, o_kv, o_kr = 0, MLA_Q_LORA, MLA_Q_LORA + MLA_KV_LORA
    o_h = o_kr + MLA_ROPE
    o_d = o_h + 4 * HG_HEADS * HG_DK
    o_g = o_d + 3 * DF_HEADS * DF_DV
    w_kr = w_in[:, o_kr:o_h]
    wm = jnp.concatenate([w_in[:, o_q:o_kr], _pad_cols(w_kr, LANE), _pad_cols(_rot_half_cols(w_kr), LANE)],
                         axis=1)
    wh = w_in[:, o_h:o_d]
    wd = w_in[:, o_d:o_g]
    wg = w_in[:, o_g:]
    uq = w_uq.reshape(MLA_Q_LORA, MLA_HEADS, MLA_NOPE + MLA_ROPE)
    nope, rope = uq[..., :MLA_NOPE], uq[..., MLA_NOPE:]
    zr = jnp.zeros((MLA_Q_LORA, MLA_HEADS, LANE - MLA_NOPE - MLA_ROPE), w_uq.dtype)
    wq1 = jnp.concatenate([nope, rope, zr], axis=-1).reshape(MLA_Q_LORA, MLA_HEADS * LANE)
    wq2 = jnp.concatenate([jnp.zeros_like(nope), _rot_half_cols(rope), zr], axis=-1)
    wq = jnp.concatenate([wq1, wq2.reshape(MLA_Q_LORA, MLA_HEADS * LANE)], axis=1)
    ukv = w_ukv.reshape(MLA_KV_LORA, MLA_HEADS, MLA_NOPE + MLA_V)
    wk = _pad_cols(ukv[..., :MLA_NOPE], LANE).reshape(MLA_KV_LORA, MLA_HEADS * LANE)
    wvt = ukv[..., MLA_NOPE:].reshape(MLA_KV_LORA, MLA_HEADS * MLA_V).T
    c = lambda a: a.astype(BF16)
    return c(wm), c(wh), c(wd), c(wg), c(wq), c(wk), c(wvt)


def _rope_tables(pos):
    half = MLA_ROPE // 2
    inv = ROPE_THETA ** (-jnp.arange(half, dtype=F32) / half)
    ang = pos.astype(F32)[:, None] * inv[None, :]
    cos2 = jnp.concatenate([jnp.cos(ang)] * 2, axis=1)
    sin2 = jnp.concatenate([jnp.sin(ang)] * 2, axis=1)
    T = pos.shape[0]
    c = MLA_SCALE * LOG2E
    zq = jnp.zeros((T, LANE - MLA_NOPE - MLA_ROPE), F32)
    cq = jnp.concatenate([jnp.full((T, MLA_NOPE), c, F32), c * cos2, zq], axis=1)
    sq = jnp.concatenate([jnp.zeros((T, MLA_NOPE), F32), c * sin2, zq], axis=1)
    return cq, sq, _pad_cols(cos2, LANE), _pad_cols(sin2, LANE)


def _rope_place():
    r = jnp.arange(MLA_ROPE)
    e = jnp.zeros((MLA_ROPE, MLA_HEADS, LANE), F32)
    e = e.at[r, :, MLA_NOPE + r].set(1.0)
    return e.reshape(MLA_ROPE, MLA_HEADS * LANE).astype(BF16)


def _pair_state(s):
    B = s.shape[0]
    st = jnp.swapaxes(s, -1, -2).reshape(B, HG_HEADS // 2, 2, HG_DV, HG_DK)
    z = jnp.zeros_like(st[:, :, 0])
    top = jnp.concatenate([st[:, :, 0], z], axis=-1)
    bot = jnp.concatenate([z, st[:, :, 1]], axis=-1)
    return jnp.concatenate([top, bot], axis=-2)


def _unpair_state(sp):
    a = sp[:, :, :HG_DV, :HG_DK]
    b = sp[:, :, HG_DV:, HG_DK:]
    st = jnp.stack([a, b], axis=2).reshape(sp.shape[0], HG_HEADS, HG_DV, HG_DK)
    return jnp.swapaxes(st, -1, -2)


def _pad_rows(a, n):
    return jnp.pad(a, ((0, 0), (0, n - a.shape[1]), (0, 0)))


def _layer(l, depth, x, pos, past, prev, s0, conv0, lw, prm, final):
    (norm_mix_g, mla_q_norm_g, mla_kv_norm_g, hgrn_lb_logits, hgrn_norm_g, diff_lambda, diff_norm_g,
     w_branch, w_out, norm_ffn_g, ffn_w_up, ffn_conv_w, ffn_conv_b, ffn_w_down, norm_final_g) = prm
    wm, wh, wd, wg, wq, wk, wvt = lw
    B, T, D = x.shape
    row = lambda a: a.reshape(1, -1)
    gmix = row(norm_mix_g)
    tm = min(512, T)
    place = _rope_place()

    if past is None:
        q_off, kvalid = 0, T
        tq = min(KV_BLOCK, T)
        cq, sq, ck, sk = _rope_tables(pos)
        (q_in, k_mla, vt_mla, dq_in, k_d, vt_d), new_rows = _proj(
            x, gmix, wm, row(mla_q_norm_g), wq, row(mla_kv_norm_g), cq, sq, ck, sk, wk, place, wvt, wd,
            l, depth, prev, tm)
    else:
        n_tok = B * T
        xf = x.reshape(1, n_tok, D)
        cq, sq, ck, sk = _rope_tables(jnp.tile(pos, B))
        q, ckv, krot = _mla_proj(xf, gmix, wm, row(mla_q_norm_g), wq, row(mla_kv_norm_g), cq, sq, ck, sk, n_tok)
        dq, dk, dv = _diff_proj(xf, gmix, wd, n_tok)
        q, ckv, krot, dq, dk, dv = (a.reshape(B, T, -1) for a in (q, ckv, krot, dq, dk, dv))
        P = past[0].shape[2]
        assert P % KV_BLOCK == 0 and T <= KV_BLOCK, (P, T)
        q_off, kvalid = P, P + T
        tq = LANE
        q_in, dq_in = _pad_rows(q, tq), _pad_rows(dq, tq)
        flat = lambda a: a.reshape(a.shape[:3] + (-1,))
        blk = lambda a: _pad_rows(a, KV_BLOCK)
        k_mla, vt_mla = _mla_kv(l, past[0], past[1], blk(ckv), blk(krot), wk, place, wvt)
        k_d, vt_d = _diff_kv(l, flat(past[2]), flat(past[3]), blk(dk), blk(dv))
        new_rows = (ckv, krot, dk, dv)
    o_a = _attn_mla(q_in, k_mla, vt_mla, tq, q_off, kvalid)[:, :T]
    lam_init = 0.8 - 0.6 * math.exp(-0.3 * l)
    o_c = _attn_diff(dq_in, k_d, vt_d, diff_lambda, row(jnp.tile(diff_norm_g, 2)), tq, q_off, kvalid,
                     lam_init)[:, :T]

    gavg = jnp.kron(jnp.eye(HG_HEADS, dtype=F32), jnp.full((HG_DV, HG_DV), 1.0 / HG_DV, F32)).astype(BF16)
    o_b, st = _hgrn(x, gmix, wh, hgrn_lb_logits, _pair_state(s0), row(jnp.tile(hgrn_norm_g, HG_HEADS)),
                    gavg, l, tm)

    y, conv_new = _post(x, o_a, o_b, o_c, gmix, wg, w_branch.astype(BF16), w_out.astype(BF16),
                        row(norm_ffn_g), ffn_w_up.astype(BF16), ffn_conv_w, row(ffn_conv_b),
                        ffn_w_down.astype(BF16), conv0, row(norm_final_g), tm, final)
    return y, new_rows, (_unpair_state(st), conv_new)


def kernel(x_prompt, x_sample, cache_mla_ckv, cache_mla_krope, cache_diff_k, cache_diff_v, state_hgrn,
           state_ffn_conv, norm_mix_g, w_in, mla_q_norm_g, mla_w_uq, mla_kv_norm_g, mla_w_ukv, hgrn_lb_logits,
           hgrn_norm_g, diff_lambda, diff_norm_g, w_branch, w_out, norm_ffn_g, ffn_w_up, ffn_conv_w,
           ffn_conv_b, ffn_w_down, norm_final_g):
    depth = w_in.shape[0]
    B, T, _ = x_prompt.shape
    Bs, Ts, _ = x_sample.shape
    P = cache_mla_ckv.shape[2]
    d_ff = ffn_w_down.shape[1]
    pos_p = jnp.arange(T)
    pos_s = P + jnp.arange(Ts)
    yp, ys = x_prompt, x_sample
    p_rows = None
    s_rows, p_small, s_small = [], [], []
    for l in range(depth):
        lw = _layer_weights(w_in[l], mla_w_uq[l], mla_w_ukv[l])
        prm = (norm_mix_g[l], mla_q_norm_g[l], mla_kv_norm_g[l], hgrn_lb_logits, hgrn_norm_g[l], diff_lambda[l],
               diff_norm_g[l], w_branch[l], w_out[l], norm_ffn_g[l], ffn_w_up[l], ffn_conv_w[l], ffn_conv_b[l],
               ffn_w_down[l], norm_final_g)
        final = l == depth - 1
        yp, p_rows, small = _layer(l, depth, yp, pos_p, None, p_rows, jnp.zeros((B, HG_HEADS, HG_DK, HG_DV), F32),
                                   jnp.zeros((B, CONV_W - 1, d_ff), F32), lw, prm, final)
        p_small.append(small)
        ys, rows, small = _layer(l, depth, ys, pos_s,
                                 (cache_mla_ckv, cache_mla_krope, cache_diff_k, cache_diff_v), None,
                                 state_hgrn[l], state_ffn_conv[l], lw, prm, final)
        s_rows.append(rows)
        s_small.append(small)
    stk = lambda states, i: jnp.stack([s[i] for s in states], axis=0)
    heads = lambda a, w: a.reshape(a.shape[:3] + (-1, w))
    s_rows = [stk(s_rows, i) for i in range(4)]
    out_rows = lambda r: (r[0], r[1], heads(r[2], 2 * DF_DH), heads(r[3], DF_DV))
    return ((yp, ys) + out_rows(p_rows) + (stk(p_small, 0), stk(p_small, 1))
            + out_rows(s_rows) + (stk(s_small, 0), stk(s_small, 1)))
```

```python
import functools
import math

import jax
import jax.numpy as jnp
from jax import lax
from jax.experimental import pallas as pl
from jax.experimental.pallas import tpu as pltpu

CHUNK = 64
EPS = 1e-6
NEG = -1e30
F_MIN = 1e-12
MLA_HEADS = 8
MLA_NOPE = 64
MLA_ROPE = 32
MLA_V = 64
MLA_Q_LORA = 384
MLA_KV_LORA = 256
ROPE_THETA = 10000.0
MLA_SCALE = (MLA_NOPE + MLA_ROPE) ** -0.5
HG_HEADS = 8
HG_DK = 64
HG_DV = 64
DF_HEADS = 8
DF_DH = 32
DF_DV = 2 * DF_DH
DF_SCALE = DF_DH ** -0.5
N_BRANCH = 3
BR_WIDTH = 512
CONV_W = 3
LOG2E = 1.4426950408889634

LANE = 128
VMEM_LIMIT = 56 * 1024 * 1024
KV_BLOCK = 256
HG_CHUNK = 64
HG_CHUNKS_PER_STEP = 8
HG_SINGLE_REF_LIMIT = 43.0
FF_BLOCK = 1024
MLA_HEADS_PER_STEP = 8
DF_PAIRS_PER_STEP = 2

BF16 = jnp.bfloat16
F32 = jnp.float32


def _cparams(sem):
    return pltpu.CompilerParams(dimension_semantics=sem, vmem_limit_bytes=VMEM_LIMIT)


def _dot(a, b):
    return jnp.dot(a, b, preferred_element_type=F32)


def _dot_nt(a, b):
    return lax.dot_general(a, b, (((1,), (1,)), ((), ())), preferred_element_type=F32)


def _rms(x, g):
    r = lax.rsqrt(jnp.mean(x * x, axis=-1, keepdims=True) + EPS)
    return x * r * g


def _sigmoid(x):
    return 0.5 * jnp.tanh(0.5 * x) + 0.5


def _sigmoid_pair(x):
    e = jnp.exp(-jnp.abs(x))
    r = 1.0 / (1.0 + e)
    er = e * r
    pos = x >= 0.0
    return jnp.where(pos, r, er), jnp.where(pos, er, r)


def _const_spec(shape):
    nd = len(shape)
    return pl.BlockSpec(shape, lambda *_: (0,) * nd)


def _mla_qkv(h, wm_ref, gq_ref, wq_ref, gkv_ref, cq_ref, sq_ref, ck_ref, sk_ref):
    z = _dot(h, wm_ref[...])
    qn = _rms(z[:, :MLA_Q_LORA], gq_ref[...]).astype(BF16)
    q2 = _dot(qn, wq_ref[...])
    nq = MLA_HEADS * LANE
    cq = jnp.concatenate([cq_ref[...]] * MLA_HEADS, axis=1)
    sq = jnp.concatenate([sq_ref[...]] * MLA_HEADS, axis=1)
    q = (q2[:, :nq] * cq + q2[:, nq:] * sq).astype(BF16)
    ckv = _rms(z[:, MLA_Q_LORA:MLA_Q_LORA + MLA_KV_LORA], gkv_ref[...])
    o = MLA_Q_LORA + MLA_KV_LORA
    kr = z[:, o:o + LANE] * ck_ref[...] + z[:, o + LANE:o + 2 * LANE] * sk_ref[...]
    return q, ckv, kr[:, :MLA_ROPE]


def _mla_expand(ckv, krot, wk_ref, e_ref, wvt_ref):
    c = ckv.astype(BF16)
    k = (_dot(c, wk_ref[...]) + _dot(krot.astype(BF16), e_ref[...])).astype(BF16)
    return k, _dot_nt(wvt_ref[...], c).astype(BF16)


def _diff_cols(h, wd_ref):
    z = _dot(h, wd_ref[...])
    n = DF_HEADS * 2 * DF_DH
    return (z[:, :n] * (DF_SCALE * LOG2E)).astype(BF16), z[:, n:2 * n], z[:, 2 * n:]


def _proj_kernel(*refs, tm, aliased):
    (x_ref, gmix_ref, wm_ref, gq_ref, wq_ref, gkv_ref, cq_ref, sq_ref, ck_ref, sk_ref, wk_ref, e_ref, wvt_ref,
     wd_ref) = refs[:14]
    (q_ref, kc_ref, vtm_ref, dq_ref, kd_ref, vtd_ref, ckv_ref, krot_ref, dk_ref, dv_ref) = refs[14 + aliased:]
    h = _rms(x_ref[0], gmix_ref[...]).astype(BF16)
    q, ckv, krot = _mla_qkv(h, wm_ref, gq_ref, wq_ref, gkv_ref, cq_ref, sq_ref, ck_ref, sk_ref)
    k, vt = _mla_expand(ckv, krot, wk_ref, e_ref, wvt_ref)
    dq, dk, dv = _diff_cols(h, wd_ref)
    dvt = dv.T.astype(BF16)
    q_ref[0] = q
    kc_ref[0] = k
    dq_ref[0] = dq
    kd_ref[0] = dk.astype(BF16)
    for i in range(tm // KV_BLOCK):
        vtm_ref[0, i] = vt[:, i * KV_BLOCK:(i + 1) * KV_BLOCK]
        vtd_ref[0, i] = dvt[:, i * KV_BLOCK:(i + 1) * KV_BLOCK]
    ckv_ref[0, 0] = ckv
    krot_ref[0, 0] = krot
    dk_ref[0, 0] = dk
    dv_ref[0, 0] = dv


def _proj(x, gmix, wm, gq, wq, gkv, cq, sq, ck, sk, wk, e, wvt, wd, layer, depth, prev, tm):
    B, T, D = x.shape
    nd = DF_HEADS * 2 * DF_DH
    nk = tm // KV_BLOCK
    tok = lambda w: pl.BlockSpec((1, tm, w), lambda b, t: (b, t, 0))
    tab = pl.BlockSpec((tm, LANE), lambda b, t: (t, 0))
    vts = lambda r: pl.BlockSpec((1, nk, r, KV_BLOCK), lambda b, t: (b, t, 0, 0))
    st = lambda w: pl.BlockSpec((1, 1, tm, w), lambda b, t: (layer, b, t, 0))
    consts = (gmix, wm, gq, wq, gkv)
    consts2 = (wk, e, wvt, wd)
    in_specs = ([tok(D)] + [_const_spec(a.shape) for a in consts] + [tab] * 4
                + [_const_spec(a.shape) for a in consts2])
    args = (x,) + consts + (cq, sq, ck, sk) + consts2
    aliases = {}
    if prev is not None:
        in_specs += [pl.BlockSpec(memory_space=pl.ANY)] * 4
        aliases = {len(args) + i: 6 + i for i in range(4)}
        args += tuple(prev)
    widths = (MLA_KV_LORA, MLA_ROPE, nd, nd)
    outs = pl.pallas_call(
        functools.partial(_proj_kernel, tm=tm, aliased=4 if prev is not None else 0),
        grid=(B, T // tm),
        in_specs=in_specs,
        out_specs=[tok(MLA_HEADS * LANE), tok(MLA_HEADS * LANE), vts(MLA_HEADS * MLA_V), tok(nd), tok(nd), vts(nd)]
        + [st(w) for w in widths],
        out_shape=[jax.ShapeDtypeStruct((B, T, MLA_HEADS * LANE), BF16),
                   jax.ShapeDtypeStruct((B, T, MLA_HEADS * LANE), BF16),
                   jax.ShapeDtypeStruct((B, T // KV_BLOCK, MLA_HEADS * MLA_V, KV_BLOCK), BF16),
                   jax.ShapeDtypeStruct((B, T, nd), BF16),
                   jax.ShapeDtypeStruct((B, T, nd), BF16),
                   jax.ShapeDtypeStruct((B, T // KV_BLOCK, nd, KV_BLOCK), BF16)]
        + [jax.ShapeDtypeStruct((depth, B, T, w), F32) for w in widths],
        input_output_aliases=aliases,
        compiler_params=_cparams(("parallel", "parallel")),
        name="proj",
    )(*args)
    return outs[:6], outs[6:]


def _mla_proj_kernel(x_ref, gmix_ref, wm_ref, gq_ref, wq_ref, gkv_ref, cq_ref, sq_ref, ck_ref, sk_ref,
                     q_ref, ckv_ref, krot_ref):
    h = _rms(x_ref[0], gmix_ref[...]).astype(BF16)
    q_ref[0], ckv_ref[0], krot_ref[0] = _mla_qkv(h, wm_ref, gq_ref, wq_ref, gkv_ref, cq_ref, sq_ref, ck_ref,
                                                  sk_ref)


def _mla_proj(x, gmix, wm, gq, wq, gkv, cq, sq, ck, sk, tm):
    B, T, D = x.shape
    grid = (B, T // tm)
    tok = lambda w: pl.BlockSpec((1, tm, w), lambda b, t: (b, t, 0))
    tab = pl.BlockSpec((tm, LANE), lambda b, t: (t, 0))
    return pl.pallas_call(
        _mla_proj_kernel,
        grid=grid,
        in_specs=[tok(D), _const_spec(gmix.shape), _const_spec(wm.shape), _const_spec(gq.shape),
                  _const_spec(wq.shape), _const_spec(gkv.shape), tab, tab, tab, tab],
        out_specs=[tok(MLA_HEADS * LANE), tok(MLA_KV_LORA), tok(MLA_ROPE)],
        out_shape=[jax.ShapeDtypeStruct((B, T, MLA_HEADS * LANE), BF16),
                   jax.ShapeDtypeStruct((B, T, MLA_KV_LORA), F32),
                   jax.ShapeDtypeStruct((B, T, MLA_ROPE), F32)],
        compiler_params=_cparams(("parallel", "parallel")),
        name="mla_proj",
    )(x, gmix, wm, gq, wq, gkv, cq, sq, ck, sk)


def _past_new_specs(layer, n_past, width):
    return (pl.BlockSpec((1, 1, KV_BLOCK, width), lambda b, t: (layer, b, jnp.minimum(t, n_past - 1), 0)),
            pl.BlockSpec((1, KV_BLOCK, width), lambda b, t: (b, 0, 0)))


def _mla_kv_kernel(pc_ref, nc_ref, pr_ref, nr_ref, wk_ref, e_ref, wvt_ref, k_ref, vt_ref, *, n_past):
    t = pl.program_id(1)

    @pl.when(t < n_past)
    def _():
        k_ref[0], vt_ref[0, 0] = _mla_expand(pc_ref[0, 0], pr_ref[0, 0], wk_ref, e_ref, wvt_ref)

    @pl.when(t >= n_past)
    def _():
        k_ref[0], vt_ref[0, 0] = _mla_expand(nc_ref[0], nr_ref[0], wk_ref, e_ref, wvt_ref)


def _mla_kv(layer, ckv_past, krot_past, ckv_new, krot_new, wk, e, wvt):
    _, B, P, _ = ckv_past.shape
    n_past = P // KV_BLOCK
    nkb = n_past + 1
    return pl.pallas_call(
        functools.partial(_mla_kv_kernel, n_past=n_past),
        grid=(B, nkb),
        in_specs=[*_past_new_specs(layer, n_past, MLA_KV_LORA), *_past_new_specs(layer, n_past, MLA_ROPE),
                  _const_spec(wk.shape), _const_spec(e.shape), _const_spec(wvt.shape)],
        out_specs=[pl.BlockSpec((1, KV_BLOCK, MLA_HEADS * LANE), lambda b, t: (b, t, 0)),
                   pl.BlockSpec((1, 1, MLA_HEADS * MLA_V, KV_BLOCK), lambda b, t: (b, t, 0, 0))],
        out_shape=[jax.ShapeDtypeStruct((B, nkb * KV_BLOCK, MLA_HEADS * LANE), BF16),
                   jax.ShapeDtypeStruct((B, nkb, MLA_HEADS * MLA_V, KV_BLOCK), BF16)],
        compiler_params=_cparams(("parallel", "parallel")),
        name="mla_kv",
    )(ckv_past, ckv_new, krot_past, krot_new, wk, e, wvt)


def _visible_blocks(q0, tq, kvalid, nkb):
    n_full = jnp.minimum((((q0 >> 6) + 1) * CHUNK) // KV_BLOCK, kvalid // KV_BLOCK)
    last = (((q0 + tq - 1) >> 6) + 1) * CHUNK
    n_vis = jnp.minimum((last + KV_BLOCK - 1) // KV_BLOCK, nkb)
    return n_full, n_vis


def _flash_t(streams, k_ref, vt_ref, m_s, acc_s, n_full, n_vis, q0, tq, kvalid):
    ones = jnp.ones((16, KV_BLOCK), BF16)
    n = len(streams)

    def step(kbs, masked):
        ss = []
        for qm, ksl, _ in streams:
            ss.append([_dot_nt(k_ref[0, pl.ds(pl.multiple_of(kb * KV_BLOCK, KV_BLOCK), KV_BLOCK), ksl], qm)
                       for kb in kbs])
        if any(masked):
            qchunk = (q0 + lax.broadcasted_iota(jnp.int32, (KV_BLOCK, tq), 1)) >> 6
            for j, kb in enumerate(kbs):
                if masked[j]:
                    kpos = kb * KV_BLOCK + lax.broadcasted_iota(jnp.int32, (KV_BLOCK, tq), 0)
                    vis = jnp.where(kpos < kvalid, kpos >> 6, jnp.int32(2 ** 30)) <= qchunk
                    for s in ss:
                        s[j] = jnp.where(vis, s[j], NEG)
        ms = []
        for i, s in enumerate(ss):
            m = m_s[i]
            for sj in s:
                m = jnp.maximum(m, jnp.max(sj, axis=0, keepdims=True))
            ms.append(m)
        ps = [[jnp.exp2(sj - m_new).astype(BF16) for sj in s] for s, m_new in zip(ss, ms)]
        for i, ((_, _, vsl), m_new, p) in enumerate(zip(streams, ms, ps)):
            acc = jnp.exp2(m_s[i] - m_new) * acc_s[i]
            for kb, pj in zip(kbs, p):
                vt = jnp.concatenate([vt_ref[0, kb, vsl, :], ones], axis=0)
                acc = acc + _dot(vt, pj)
            acc_s[i] = acc
            m_s[i] = m_new

    for i in range(n):
        m_s[i] = jnp.full((1, tq), -jnp.inf, F32)
        acc_s[i] = jnp.zeros((MLA_V + 16, tq), F32)
    n_pairs = n_full // 2

    def pair_step(j, c):
        step([2 * j, 2 * j + 1], (False, False))
        return c

    def masked_step(kb, c):
        step([kb], (True,))
        return c

    def quad_step(j, c):
        step([4 * j, 4 * j + 1], (False, False))
        step([4 * j + 2, 4 * j + 3], (False, False))
        return c

    n_quads = n_pairs // 2
    has_masked = n_vis > n_full
    tail = jnp.logical_and(jnp.logical_and(n_full == 2 * n_pairs, n_pairs - 2 * n_quads == 1), has_masked)
    mixed = jnp.logical_and(n_full - 2 * n_pairs == 1, has_masked)
    lax.fori_loop(0, n_quads, quad_step, 0)
    lax.fori_loop(2 * n_quads, n_pairs - tail.astype(jnp.int32), pair_step, 0)

    @pl.when(tail)
    def _():
        step([n_full - 2, n_full - 1], (False, False))
        step([n_full], (True,))

    @pl.when(mixed)
    def _():
        step([n_full - 1, n_full], (False, True))

    first_single = jnp.where(jnp.logical_or(mixed, tail), n_full + 1, 2 * n_pairs)
    lax.fori_loop(first_single, n_vis, masked_step, 0)
    return [acc_s[i] for i in range(n)]


def _attn_mla_kernel(q_ref, k_ref, vt_ref, o_ref, m_s, acc_s, *, tq, q_off, kvalid, nkb):
    q0 = q_off + pl.program_id(2) * tq
    n_full, n_vis = _visible_blocks(q0, tq, kvalid, nkb)
    streams = []
    for hh in range(MLA_HEADS_PER_STEP):
        lanes = slice(hh * LANE, (hh + 1) * LANE)
        streams.append((q_ref[0, :, lanes], lanes, slice(hh * MLA_V, (hh + 1) * MLA_V)))
    accs = _flash_t(streams, k_ref, vt_ref, m_s, acc_s, n_full, n_vis, q0, tq, kvalid)
    outs = [acc[:MLA_V] * (1.0 / acc[MLA_V:MLA_V + 1]) for acc in accs]
    for i in range(0, MLA_HEADS_PER_STEP, 2):
        o_ref[0, :, i * MLA_V:(i + 2) * MLA_V] = jnp.concatenate(outs[i:i + 2], axis=0).T.astype(BF16)


def _attn_mla(q, k, vt, tq, q_off, kvalid):
    B, T, _ = q.shape
    Tk = k.shape[1]
    nkb = Tk // KV_BLOCK
    hps = MLA_HEADS_PER_STEP
    kern = functools.partial(_attn_mla_kernel, tq=tq, q_off=q_off, kvalid=kvalid, nkb=nkb)
    return pl.pallas_call(
        kern,
        grid=(B, MLA_HEADS // hps, T // tq),
        in_specs=[pl.BlockSpec((1, tq, hps * LANE), lambda b, p, i: (b, i, p)),
                  pl.BlockSpec((1, Tk, hps * LANE), lambda b, p, i: (b, 0, p)),
                  pl.BlockSpec((1, nkb, hps * MLA_V, KV_BLOCK), lambda b, p, i: (b, 0, p, 0))],
        out_specs=pl.BlockSpec((1, tq, hps * MLA_V), lambda b, p, i: (b, i, p)),
        out_shape=jax.ShapeDtypeStruct((B, T, MLA_HEADS * MLA_V), BF16),
        scratch_shapes=[pltpu.VMEM((hps, 1, tq), F32), pltpu.VMEM((hps, MLA_V + 16, tq), F32)],
        compiler_params=_cparams(("parallel", "parallel", "arbitrary")),
        name="attn_mla",
    )(q, k, vt)


def _attn_diff_kernel(q_ref, k_ref, vt_ref, lam_ref, g_ref, o_ref, m_s, acc_s, *, tq, q_off, kvalid, nkb,
                      lam_init):
    q0 = q_off + pl.program_id(2) * tq
    n_full, n_vis = _visible_blocks(q0, tq, kvalid, nkb)
    lamv = lam_ref[...]
    lam = (jnp.exp(jnp.sum(lamv[0:1] * lamv[1:2], axis=-1, keepdims=True))
           - jnp.exp(jnp.sum(lamv[2:3] * lamv[3:4], axis=-1, keepdims=True)) + lam_init)
    lane = lax.broadcasted_iota(jnp.int32, (1, LANE), 1)
    streams = []
    for pr in range(DF_PAIRS_PER_STEP):
        lanes = slice(pr * LANE, (pr + 1) * LANE)
        q = q_ref[0, :, lanes]
        for hh in range(2):
            for j in range(2):
                lo = hh * 2 * DF_DH + j * DF_DH
                qm = q * jnp.where((lane >= lo) & (lane < lo + DF_DH), 1.0, 0.0).astype(BF16)
                streams.append((qm, lanes, slice((2 * pr + hh) * DF_DV, (2 * pr + hh + 1) * DF_DV)))
    accs = _flash_t(streams, k_ref, vt_ref, m_s, acc_s, n_full, n_vis, q0, tq, kvalid)
    maps = [acc[:DF_DV] * (1.0 / acc[DF_DV:DF_DV + 1]) for acc in accs]
    for pr in range(DF_PAIRS_PER_STEP):
        outs = []
        for hh in range(2):
            i = 4 * pr + 2 * hh
            o = maps[i] - lam * maps[i + 1]
            outs.append(o * lax.rsqrt(jnp.mean(o * o, axis=0, keepdims=True) + EPS))
        ot = jnp.concatenate(outs, axis=0).T
        o_ref[0, :, pr * LANE:(pr + 1) * LANE] = (ot * g_ref[...] * (1.0 - lam_init)).astype(BF16)


def _attn_diff(q, k, vt, lam_rows, g2, tq, q_off, kvalid, lam_init):
    B, T, _ = q.shape
    Tk = k.shape[1]
    nkb = Tk // KV_BLOCK
    pps = DF_PAIRS_PER_STEP
    kern = functools.partial(_attn_diff_kernel, tq=tq, q_off=q_off, kvalid=kvalid, nkb=nkb,
                             lam_init=lam_init)
    return pl.pallas_call(
        kern,
        grid=(B, DF_HEADS // (2 * pps), T // tq),
        in_specs=[pl.BlockSpec((1, tq, pps * LANE), lambda b, p, i: (b, i, p)),
                  pl.BlockSpec((1, Tk, pps * LANE), lambda b, p, i: (b, 0, p)),
                  pl.BlockSpec((1, nkb, pps * 2 * DF_DV, KV_BLOCK), lambda b, p, i: (b, 0, p, 0)),
                  _const_spec(lam_rows.shape), _const_spec(g2.shape)],
        out_specs=pl.BlockSpec((1, tq, pps * 2 * DF_DV), lambda b, p, i: (b, i, p)),
        out_shape=jax.ShapeDtypeStruct((B, T, DF_HEADS * DF_DV), BF16),
        scratch_shapes=[pltpu.VMEM((4 * pps, 1, tq), F32), pltpu.VMEM((4 * pps, DF_DV + 16, tq), F32)],
        compiler_params=_cparams(("parallel", "parallel", "arbitrary")),
        name="attn_diff",
    )(q, k, vt, lam_rows, g2)


def _diff_proj_kernel(x_ref, gmix_ref, wd_ref, q_ref, dk_ref, dv_ref):
    h = _rms(x_ref[0], gmix_ref[...]).astype(BF16)
    q_ref[0], dk_ref[0], dv_ref[0] = _diff_cols(h, wd_ref)


def _diff_proj(x, gmix, wd, tm):
    B, T, D = x.shape
    n = DF_HEADS * 2 * DF_DH
    tok = lambda w: pl.BlockSpec((1, tm, w), lambda b, t: (b, t, 0))
    return pl.pallas_call(
        _diff_proj_kernel,
        grid=(B, T // tm),
        in_specs=[tok(D), _const_spec(gmix.shape), _const_spec(wd.shape)],
        out_specs=[tok(n), tok(n), tok(n)],
        out_shape=[jax.ShapeDtypeStruct((B, T, n), BF16),
                   jax.ShapeDtypeStruct((B, T, n), F32),
                   jax.ShapeDtypeStruct((B, T, n), F32)],
        compiler_params=_cparams(("parallel", "parallel")),
        name="diff_proj",
    )(x, gmix, wd)


def _diff_kv_kernel(pk_ref, nk_ref, pv_ref, nv_ref, k_ref, vt_ref, *, n_past):
    t = pl.program_id(1)

    @pl.when(t < n_past)
    def _():
        k_ref[0] = pk_ref[0, 0].astype(BF16)
        vt_ref[0, 0] = pv_ref[0, 0].T.astype(BF16)

    @pl.when(t >= n_past)
    def _():
        k_ref[0] = nk_ref[0].astype(BF16)
        vt_ref[0, 0] = nv_ref[0].T.astype(BF16)


def _diff_kv(layer, dk_past, dv_past, dk_new, dv_new):
    _, B, P, n = dk_past.shape
    n_past = P // KV_BLOCK
    nkb = n_past + 1
    return pl.pallas_call(
        functools.partial(_diff_kv_kernel, n_past=n_past),
        grid=(B, nkb),
        in_specs=[*_past_new_specs(layer, n_past, n), *_past_new_specs(layer, n_past, n)],
        out_specs=[pl.BlockSpec((1, KV_BLOCK, n), lambda b, t: (b, t, 0)),
                   pl.BlockSpec((1, 1, n, KV_BLOCK), lambda b, t: (b, t, 0, 0))],
        out_shape=[jax.ShapeDtypeStruct((B, nkb * KV_BLOCK, n), BF16),
                   jax.ShapeDtypeStruct((B, nkb, n, KV_BLOCK), BF16)],
        compiler_params=_cparams(("parallel", "parallel")),
        name="diff_kv",
    )(dk_past, dk_new, dv_past, dv_new)


def _split2(x):
    a = x.astype(BF16)
    return a, (x - a.astype(F32)).astype(BF16)


def _level_ref_rows(b, size):
    half = size // 2
    C = b.shape[0]
    pieces = [jnp.broadcast_to(b[i * size + half - 1:i * size + half, :], (size, b.shape[1]))
              for i in range(C // size)]
    return pieces[0] if len(pieces) == 1 else jnp.concatenate(pieces, axis=0)


def _base_ref_rows(b):
    C, W = b.shape
    sub = lax.broadcasted_iota(jnp.int32, (8, W), 0)
    pieces = [jnp.where(sub < 4, jnp.broadcast_to(b[8 * i:8 * i + 1, :], (8, W)),
                        jnp.broadcast_to(b[8 * i + 4:8 * i + 5, :], (8, W))) for i in range(C // 8)]
    return jnp.concatenate(pieces, axis=0)


def _hgrn_kernel(x_ref, gmix_ref, wh_ref, lbl_ref, st0_ref, gh_ref, gavg_ref, ob_ref, st_ref, z_s, b_s,
                 *, layer, tm):
    C = min(HG_CHUNK, tm)
    W = HG_HEADS * HG_DK
    ti = pl.program_id(1)

    @pl.when(ti == 0)
    def _():
        st_ref[0] = st0_ref[0]

    h = _rms(x_ref[0], gmix_ref[...]).astype(BF16)
    z_s[...] = _dot(h, wh_ref[...])

    lg = lbl_ref[...]
    e = jnp.exp(lg - jnp.max(lg, axis=0, keepdims=True))
    sm = e / jnp.sum(e, axis=0, keepdims=True)
    lb = jnp.zeros((1, W), F32)
    for i in range(1, layer + 1):
        lb = lb + sm[i:i + 1]

    row = lax.broadcasted_iota(jnp.int32, (C, C), 0)
    col = lax.broadcasted_iota(jnp.int32, (C, C), 1)
    tril_b = col <= row
    tril = tril_b.astype(BF16)
    rsub = lax.broadcasted_iota(jnp.int32, (C, LANE), 0)
    lane = lax.broadcasted_iota(jnp.int32, (C, LANE), 1)
    head_lo = lane < HG_DK
    sizes = [s for s in (64, 32, 16, 8) if s <= C]
    blk_masks = {s: (row // s) == (col // s) for s in sizes}
    base_mask = ((row // 4) == (col // 4)) & tril_b
    diag = ((lax.broadcasted_iota(jnp.int32, (LANE, LANE), 0) // HG_DK)
            == (lax.broadcasted_iota(jnp.int32, (LANE, LANE), 1) // HG_DK))
    mid = C // 2 - 1
    n_pairs = HG_HEADS // 2

    group = min(HG_CHUNKS_PER_STEP, tm // C)

    def prep(j, spread):
        for i in range(group):
            r0 = pl.multiple_of((j * group + i) * C, C)
            hq = z_s[pl.ds(r0, C), 0:W]
            hf = z_s[pl.ds(r0, C), W:2 * W]
            hg = z_s[pl.ds(r0, C), 3 * W:4 * W]
            sig, sig_neg = _sigmoid_pair(hf)
            f = sig + lb * sig_neg
            l1, l2 = _split2(jnp.log(jnp.maximum(f, F_MIN)))
            b = _dot(tril, l1) + _dot(tril, l2)
            z_s[pl.ds(r0, C), 0:W] = hq * _sigmoid(hq)
            z_s[pl.ds(r0, C), W:2 * W] = (1.0 - lb) * sig_neg
            z_s[pl.ds(r0, C), 3 * W:4 * W] = hg * _sigmoid(hg)
            b_s[pl.ds(r0, C), :] = b
            spread = jnp.maximum(spread, jnp.maximum(b[0:1] - b[mid:mid + 1], b[mid:mid + 1] - b[C - 1:C]))
        return spread

    spread = lax.fori_loop(0, tm // (C * group), prep, jnp.zeros((1, W), F32))
    single_ref_ok = jnp.max(spread) <= HG_SINGLE_REF_LIMIT

    def heads(x):
        return (jnp.where(head_lo, x, 0.0).astype(BF16), jnp.where(head_lo, 0.0, x).astype(BF16))

    def finish(cs, bs, qs, ks, vs, a_all):
        o_all = []
        for i, c in enumerate(cs):
            b, q, k, v = bs[i], qs[i], ks[i], vs[i]
            b_end = b[C - 1:C, :]
            qe = (q * jnp.exp(b)).astype(BF16)
            ke = (k * jnp.exp(b_end - b)).astype(BF16)
            d_end = jnp.exp(b_end)
            o_pairs = []
            for p in range(n_pairs):
                sl = slice(p * LANE, (p + 1) * LANE)
                v0, v1 = heads(v[:, sl])
                a0, a1 = a_all[i][p]
                st = st_ref[0, p]
                o_p = (_dot(a0.astype(BF16), v0) + _dot(a1.astype(BF16), v1)
                       + _dot_nt(qe[:, sl], st.astype(BF16)))
                upd = _dot(v[:, sl].T.astype(BF16), ke[:, sl])
                st_ref[0, p] = st * d_end[:, sl] + jnp.where(diag, upd, 0.0)
                o_pairs.append(o_p)
            o_all.append(jnp.concatenate(o_pairs, axis=1))
        for c, o in zip(cs, o_all):
            r0 = pl.multiple_of(c * C, C)
            ms = _dot((o * o).astype(BF16), gavg_ref[...])
            ob = o * lax.rsqrt(ms + EPS) * gh_ref[...] * z_s[pl.ds(r0, C), 3 * W:4 * W]
            ob_ref[0, pl.ds(r0, C), :] = ob.astype(BF16)

    def load(cs):
        out = []
        for c in cs:
            r0 = pl.multiple_of(c * C, C)
            out.append((b_s[pl.ds(r0, C), :], z_s[pl.ds(r0, C), 0:W], z_s[pl.ds(r0, C), W:2 * W],
                        z_s[pl.ds(r0, C), 2 * W:3 * W]))
        return tuple(zip(*out))

    def fast(j, carry):
        cs = [j * group + i for i in range(group)]
        bs, qs, ks, vs = load(cs)
        a_all = []
        for b, q, k in zip(bs, qs, ks):
            ref = b[mid:mid + 1, :]
            qt = q * jnp.exp(b - ref)
            kt = (k * jnp.exp(ref - b)).astype(BF16)
            a_c = []
            for p in range(n_pairs):
                sl = slice(p * LANE, (p + 1) * LANE)
                q0, q1 = heads(qt[:, sl])
                a_c.append((jnp.where(tril_b, _dot_nt(q0, kt[:, sl]), 0.0),
                            jnp.where(tril_b, _dot_nt(q1, kt[:, sl]), 0.0)))
            a_all.append(a_c)
        finish(cs, bs, qs, ks, vs, a_all)
        return carry

    def safe(c, carry):
        bs, qs, ks, vs = load([c])
        a_c = []
        for p in range(n_pairs):
            sl = slice(p * LANE, (p + 1) * LANE)
            bp, qp, kp = bs[0][:, sl], qs[0][:, sl], ks[0][:, sl]
            a_h = [jnp.zeros((C, C), F32), jnp.zeros((C, C), F32)]
            for s in sizes:
                ref = _level_ref_rows(bp, s)
                upper = (rsub & (s - 1)) >= (s // 2)
                qt = heads(qp * jnp.exp(jnp.where(upper, bp - ref, NEG)))
                kt = (kp * jnp.exp(jnp.where(upper, NEG, ref - bp))).astype(BF16)
                for hh in range(2):
                    pr = _dot_nt(qt[hh], kt)
                    a_h[hh] = a_h[hh] + (pr if s == C else jnp.where(blk_masks[s], pr, 0.0))
            ref = _base_ref_rows(bp)
            qt = heads(qp * jnp.exp(bp - ref))
            kt = (kp * jnp.exp(ref - bp)).astype(BF16)
            for hh in range(2):
                a_h[hh] = a_h[hh] + jnp.where(base_mask, _dot_nt(qt[hh], kt), 0.0)
            a_c.append(tuple(a_h))
        finish([c], bs, qs, ks, vs, [a_c])
        return carry

    @pl.when(single_ref_ok)
    def _():
        lax.fori_loop(0, tm // (C * group), fast, 0)

    @pl.when(jnp.logical_not(single_ref_ok))
    def _():
        lax.fori_loop(0, tm // C, safe, 0)


def _hgrn(x, gmix, wh, lbl, st0, gh, gavg, layer, tm):
    B, T, D = x.shape
    W = HG_HEADS * HG_DK
    st_spec = pl.BlockSpec((1, HG_HEADS // 2, LANE, LANE), lambda b, t: (b, 0, 0, 0))
    return pl.pallas_call(
        functools.partial(_hgrn_kernel, layer=layer, tm=tm),
        grid=(B, T // tm),
        in_specs=[pl.BlockSpec((1, tm, D), lambda b, t: (b, t, 0)), _const_spec(gmix.shape),
                  _const_spec(wh.shape), _const_spec(lbl.shape), st_spec, _const_spec(gh.shape),
                  _const_spec(gavg.shape)],
        out_specs=[pl.BlockSpec((1, tm, W), lambda b, t: (b, t, 0)), st_spec],
        out_shape=[jax.ShapeDtypeStruct((B, T, W), BF16),
                   jax.ShapeDtypeStruct((B, HG_HEADS // 2, LANE, LANE), F32)],
        scratch_shapes=[pltpu.VMEM((tm, 4 * W), F32), pltpu.VMEM((tm, W), F32)],
        compiler_params=_cparams(("parallel", "arbitrary")),
        name="hgrn",
    )(x, gmix, wh, lbl, st0, gh, gavg)


def _post_kernel(x_ref, oa_ref, ob_ref, oc_ref, gmix_ref, wg_ref, wb_ref, wo_ref, gffn_ref, wup_ref,
                 cw_ref, cb_ref, wdn_ref, conv0_ref, gfin_ref, y_ref, conv_ref, carry_s, *, tm, d_ff, final):
    ti = pl.program_id(1)
    x = x_ref[0]
    D = x.shape[1]
    h = _rms(x, gmix_ref[...]).astype(BF16)
    mixed = jnp.zeros((tm, D), F32)
    for n, o_ref in enumerate((oa_ref, ob_ref, oc_ref)):
        gate = _sigmoid(_dot(h, wg_ref[:, n * D:(n + 1) * D]))
        mixed = mixed + gate * _dot(o_ref[0], wb_ref[n])
    x1 = x + _dot(mixed.astype(BF16), wo_ref[...])
    xn = _rms(x1, gffn_ref[...]).astype(BF16)

    @pl.when(ti == 0)
    def _():
        carry_s[0:CONV_W - 1, :] = conv0_ref[0]

    acc = jnp.zeros((tm, D), F32)
    for f in range(0, d_ff, FF_BLOCK):
        w = min(FF_BLOCK, d_ff - f)
        cols = slice(f, f + w)
        a = _dot(xn, wup_ref[:, cols])
        vv = _dot(xn, wup_ref[:, d_ff + f:d_ff + f + w])
        p0 = carry_s[0:1, cols]
        p1 = carry_s[1:2, cols]
        r1 = pltpu.roll(a, 1, 0)
        r2 = pltpu.roll(a, 2, 0)
        last2 = r2[0:8, :]
        rows = lax.broadcasted_iota(jnp.int32, (8, w), 0)
        a1 = jnp.concatenate([jnp.where(rows < 1, p1, r1[0:8]), r1[8:]], axis=0) if tm > 8 else \
            jnp.where(rows < 1, p1, r1)
        h2 = jnp.where(rows < 1, p0, jnp.where(rows < 2, p1, last2))
        a2 = jnp.concatenate([h2, r2[8:]], axis=0) if tm > 8 else h2
        cc = (cb_ref[:, cols] + a2 * cw_ref[0:1, cols] + a1 * cw_ref[1:2, cols] + a * cw_ref[2:3, cols])
        act = (cc * _sigmoid(cc) * vv).astype(BF16)
        acc = acc + _dot(act, wdn_ref[cols, :])
        carry_s[:, cols] = last2
        conv_ref[0, :, cols] = last2[0:CONV_W - 1, :]
    x2 = x1 + acc
    y_ref[0] = _rms(x2, gfin_ref[...]) if final else x2


def _post(x, oa, ob, oc, gmix, wg, wb, wo, gffn, wup, cw, cb, wdn, conv0, gfin, tm, final):
    B, T, D = x.shape
    d_ff = wdn.shape[0]
    tok = lambda w: pl.BlockSpec((1, tm, w), lambda b, t: (b, t, 0))
    cspec = pl.BlockSpec((1, CONV_W - 1, d_ff), lambda b, t: (b, 0, 0))
    one = lambda a: pl.BlockSpec(a.shape, lambda *_: (0,) * a.ndim, pipeline_mode=pl.Buffered(1))
    return pl.pallas_call(
        functools.partial(_post_kernel, tm=tm, d_ff=d_ff, final=final),
        grid=(B, T // tm),
        in_specs=[tok(D), tok(BR_WIDTH), tok(BR_WIDTH), tok(BR_WIDTH), one(gmix), one(wg), one(wb), one(wo),
                  one(gffn), one(wup), one(cw), one(cb), one(wdn), cspec, one(gfin)],
        out_specs=[tok(D), cspec],
        out_shape=[jax.ShapeDtypeStruct((B, T, D), F32),
                   jax.ShapeDtypeStruct((B, CONV_W - 1, d_ff), F32)],
        scratch_shapes=[pltpu.VMEM((8, d_ff), F32)],
        compiler_params=_cparams(("parallel", "arbitrary")),
        name="post",
    )(x, oa, ob, oc, gmix, wg, wb, wo, gffn, wup, cw, cb, wdn, conv0, gfin)


def _rot_half_cols(w):
    half = w.shape[-1] // 2
    return jnp.concatenate([-w[..., half:], w[..., :half]], axis=-1)


def _pad_cols(w, width):
    return jnp.pad(w, [(0, 0)] * (w.ndim - 1) + [(0, width - w.shape[-1])])


def _layer_weights(w_in, w_uq, w_ukv):
    D = w_in.shape[0]
    o_q, o_kv, o_kr = 0, MLA_Q_LORA, MLA_Q_LORA + MLA_KV_LORA
    o_h = o_kr + MLA_ROPE
    o_d = o_h + 4 * HG_HEADS * HG_DK
    o_g = o_d + 3 * DF_HEADS * DF_DV
    w_kr = w_in[:, o_kr:o_h]
    wm = jnp.concatenate([w_in[:, o_q:o_kr], _pad_cols(w_kr, LANE), _pad_cols(_rot_half_cols(w_kr), LANE)],
                         axis=1)
    wh = w_in[:, o_h:o_d]
    wd = w_in[:, o_d:o_g]
    wg = w_in[:, o_g:]
    uq = w_uq.reshape(MLA_Q_LORA, MLA_HEADS, MLA_NOPE + MLA_ROPE)
    nope, rope = uq[..., :MLA_NOPE], uq[..., MLA_NOPE:]
    zr = jnp.zeros((MLA_Q_LORA, MLA_HEADS, LANE - MLA_NOPE - MLA_ROPE), w_uq.dtype)
    wq1 = jnp.concatenate([nope, rope, zr], axis=-1).reshape(MLA_Q_LORA, MLA_HEADS * LANE)
    wq2 = jnp.concatenate([jnp.zeros_like(nope), _rot_half_cols(rope), zr], axis=-1)
    wq = jnp.concatenate([wq1, wq2.reshape(MLA_Q_LORA, MLA_HEADS * LANE)], axis=1)
    ukv = w_ukv.reshape(MLA_KV_LORA, MLA_HEADS, MLA_NOPE + MLA_V)
    wk = _pad_cols(ukv[..., :MLA_NOPE], LANE).reshape(MLA_KV_LORA, MLA_HEADS * LANE)
    wvt = ukv[..., MLA_NOPE:].reshape(MLA_KV_LORA, MLA_HEADS * MLA_V).T
    c = lambda a: a.astype(BF16)
    return c(wm), c(wh), c(wd), c(wg), c(wq), c(wk), c(wvt)


def _rope_tables(pos):
    half = MLA_ROPE // 2
    inv = ROPE_THETA ** (-jnp.arange(half, dtype=F32) / half)
    ang = pos.astype(F32)[:, None] * inv[None, :]
    cos2 = jnp.concatenate([jnp.cos(ang)] * 2, axis=1)
    sin2 = jnp.concatenate([jnp.sin(ang)] * 2, axis=1)
    T = pos.shape[0]
    c = MLA_SCALE * LOG2E
    zq = jnp.zeros((T, LANE - MLA_NOPE - MLA_ROPE), F32)
    cq = jnp.concatenate([jnp.full((T, MLA_NOPE), c, F32), c * cos2, zq], axis=1)
    sq = jnp.concatenate([jnp.zeros((T, MLA_NOPE), F32), c * sin2, zq], axis=1)
    return cq, sq, _pad_cols(cos2, LANE), _pad_cols(sin2, LANE)


def _rope_place():
    r = jnp.arange(MLA_ROPE)
    e = jnp.zeros((MLA_ROPE, MLA_HEADS, LANE), F32)
    e = e.at[r, :, MLA_NOPE + r].set(1.0)
    return e.reshape(MLA_ROPE, MLA_HEADS * LANE).astype(BF16)


def _pair_state(s):
    B = s.shape[0]
    st = jnp.swapaxes(s, -1, -2).reshape(B, HG_HEADS // 2, 2, HG_DV, HG_DK)
    z = jnp.zeros_like(st[:, :, 0])
    top = jnp.concatenate([st[:, :, 0], z], axis=-1)
    bot = jnp.concatenate([z, st[:, :, 1]], axis=-1)
    return jnp.concatenate([top, bot], axis=-2)


def _unpair_state(sp):
    a = sp[:, :, :HG_DV, :HG_DK]
    b = sp[:, :, HG_DV:, HG_DK:]
    st = jnp.stack([a, b], axis=2).reshape(sp.shape[0], HG_HEADS, HG_DV, HG_DK)
    return jnp.swapaxes(st, -1, -2)


def _pad_rows(a, n):
    return jnp.pad(a, ((0, 0), (0, n - a.shape[1]), (0, 0)))


def _layer(l, depth, x, pos, past, prev, s0, conv0, lw, prm, final):
    (norm_mix_g, mla_q_norm_g, mla_kv_norm_g, hgrn_lb_logits, hgrn_norm_g, diff_lambda, diff_norm_g,
     w_branch, w_out, norm_ffn_g, ffn_w_up, ffn_conv_w, ffn_conv_b, ffn_w_down, norm_final_g) = prm
    wm, wh, wd, wg, wq, wk, wvt = lw
    B, T, D = x.shape
    row = lambda a: a.reshape(1, -1)
    gmix = row(norm_mix_g)
    tm = min(512, T)
    place = _rope_place()

    if past is None:
        q_off, kvalid = 0, T
        tq = min(KV_BLOCK, T)
        cq, sq, ck, sk = _rope_tables(pos)
        (q_in, k_mla, vt_mla, dq_in, k_d, vt_d), new_rows = _proj(
            x, gmix, wm, row(mla_q_norm_g), wq, row(mla_kv_norm_g), cq, sq, ck, sk, wk, place, wvt, wd,
            l, depth, prev, tm)
    else:
        n_tok = B * T
        xf = x.reshape(1, n_tok, D)
        cq, sq, ck, sk = _rope_tables(jnp.tile(pos, B))
        q, ckv, krot = _mla_proj(xf, gmix, wm, row(mla_q_norm_g), wq, row(mla_kv_norm_g), cq, sq, ck, sk, n_tok)
        dq, dk, dv = _diff_proj(xf, gmix, wd, n_tok)
        q, ckv, krot, dq, dk, dv = (a.reshape(B, T, -1) for a in (q, ckv, krot, dq, dk, dv))
        P = past[0].shape[2]
        assert P % KV_BLOCK == 0 and T <= KV_BLOCK, (P, T)
        q_off, kvalid = P, P + T
        tq = LANE
        q_in, dq_in = _pad_rows(q, tq), _pad_rows(dq, tq)
        flat = lambda a: a.reshape(a.shape[:3] + (-1,))
        blk = lambda a: _pad_rows(a, KV_BLOCK)
        k_mla, vt_mla = _mla_kv(l, past[0], past[1], blk(ckv), blk(krot), wk, place, wvt)
        k_d, vt_d = _diff_kv(l, flat(past[2]), flat(past[3]), blk(dk), blk(dv))
        new_rows = (ckv, krot, dk, dv)
    o_a = _attn_mla(q_in, k_mla, vt_mla, tq, q_off, kvalid)[:, :T]
    lam_init = 0.8 - 0.6 * math.exp(-0.3 * l)
    o_c = _attn_diff(dq_in, k_d, vt_d, diff_lambda, row(jnp.tile(diff_norm_g, 2)), tq, q_off, kvalid,
                     lam_init)[:, :T]

    gavg = jnp.kron(jnp.eye(HG_HEADS, dtype=F32), jnp.full((HG_DV, HG_DV), 1.0 / HG_DV, F32)).astype(BF16)
    o_b, st = _hgrn(x, gmix, wh, hgrn_lb_logits, _pair_state(s0), row(jnp.tile(hgrn_norm_g, HG_HEADS)),
                    gavg, l, tm)

    y, conv_new = _post(x, o_a, o_b, o_c, gmix, wg, w_branch.astype(BF16), w_out.astype(BF16),
                        row(norm_ffn_g), ffn_w_up.astype(BF16), ffn_conv_w, row(ffn_conv_b),
                        ffn_w_down.astype(BF16), conv0, row(norm_final_g), tm, final)
    return y, new_rows, (_unpair_state(st), conv_new)


def kernel(x_prompt, x_sample, cache_mla_ckv, cache_mla_krope, cache_diff_k, cache_diff_v, state_hgrn,
           state_ffn_conv, norm_mix_g, w_in, mla_q_norm_g, mla_w_uq, mla_kv_norm_g, mla_w_ukv, hgrn_lb_logits,
           hgrn_norm_g, diff_lambda, diff_norm_g, w_branch, w_out, norm_ffn_g, ffn_w_up, ffn_conv_w,
           ffn_conv_b, ffn_w_down, norm_final_g):
    depth = w_in.shape[0]
    B, T, _ = x_prompt.shape
    Bs, Ts, _ = x_sample.shape
    P = cache_mla_ckv.shape[2]
    d_ff = ffn_w_down.shape[1]
    pos_p = jnp.arange(T)
    pos_s = P + jnp.arange(Ts)
    yp, ys = x_prompt, x_sample
    p_rows = None
    s_rows, p_small, s_small = [], [], []
    for l in range(depth):
        lw = _layer_weights(w_in[l], mla_w_uq[l], mla_w_ukv[l])
        prm = (norm_mix_g[l], mla_q_norm_g[l], mla_kv_norm_g[l], hgrn_lb_logits, hgrn_norm_g[l], diff_lambda[l],
               diff_norm_g[l], w_branch[l], w_out[l], norm_ffn_g[l], ffn_w_up[l], ffn_conv_w[l], ffn_conv_b[l],
               ffn_w_down[l], norm_final_g)
        final = l == depth - 1
        yp, p_rows, small = _layer(l, depth, yp, pos_p, None, p_rows, jnp.zeros((B, HG_HEADS, HG_DK, HG_DV), F32),
                                   jnp.zeros((B, CONV_W - 1, d_ff), F32), lw, prm, final)
        p_small.append(small)
        ys, rows, small = _layer(l, depth, ys, pos_s,
                                 (cache_mla_ckv, cache_mla_krope, cache_diff_k, cache_diff_v), None,
                                 state_hgrn[l], state_ffn_conv[l], lw, prm, final)
        s_rows.append(rows)
        s_small.append(small)
    stk = lambda states, i: jnp.stack([s[i] for s in states], axis=0)
    heads = lambda a, w: a.reshape(a.shape[:3] + (-1, w))
    s_rows = [stk(s_rows, i) for i in range(4)]
    out_rows = lambda r: (r[0], r[1], heads(r[2], 2 * DF_DH), heads(r[3], DF_DV))
    return ((yp, ys) + out_rows(p_rows) + (stk(p_small, 0), stk(p_small, 1))
            + out_rows(s_rows) + (stk(s_small, 0), stk(s_small, 1)))
```
